```python
import math
import jax, jax.numpy as jnp
from jax import lax
import numpy as np

D_MODEL = 2048
BATCH = 4
SEQ = 2048
DEPTH = 2
DEC_BATCH = 128
DEC_SEQ = 8
PAST_LEN = 16384
PAGE_SIZE = 128

N_MIXERS = 2
N_RET = (DEPTH + 1) // 2
N_GDN = DEPTH // 2
RET_HEADS = D_MODEL // 256
RET_DK = D_MODEL // RET_HEADS
RET_DV = 2 * D_MODEL // RET_HEADS
RET_QK = RET_HEADS * RET_DK
RET_V = RET_HEADS * RET_DV
RET_IN = 2 * RET_QK + 2 * RET_V
ROPE_BASE = 10000.0
GDN_DK = 128
GDN_DV = 128
GDN_K_HEADS = D_MODEL // 128
GDN_V_HEADS = 2 * GDN_K_HEADS
GDN_QK = GDN_K_HEADS * GDN_DK
GDN_V = GDN_V_HEADS * GDN_DV
GDN_CONV_CH = 2 * GDN_QK + GDN_V
GDN_IN = GDN_CONV_CH + GDN_V + 2 * GDN_V_HEADS
CONV_W = 4
D_FF = 5632
FFN_RES = 0.5
N_SUB = 3
CHUNK = 64
EPS = 1e-6

kernel_name = 'retnet_gated_deltanet_macaron_adaln_step'


def rmsnorm(x, w=None):
    xf = x.astype(jnp.float32)
    y = xf * lax.rsqrt(jnp.mean(xf * xf, -1, keepdims=True) + EPS)
    if w is not None:
        y = y * w.astype(jnp.float32)
    return y.astype(x.dtype)


def l2norm(x):
    return x * lax.rsqrt(jnp.sum(x * x, -1, keepdims=True) + EPS)


def swiglu(h, w_gu, w_down):
    gate, up = jnp.split(h @ w_gu, 2, axis=-1)
    return (jax.nn.silu(gate) * up) @ w_down


def adaln_in(x, mod, j, w):
    return rmsnorm(x, w) * (1 + mod[:, j, 1]) + mod[:, j, 0]


def adaln_out(y, mod, j, w):
    return mod[:, j, 2] * rmsnorm(y, w)


def rotary(x, pos):
    half = x.shape[-1] // 2
    inv = 1.0 / (ROPE_BASE ** jnp.linspace(0.0, 1.0, half, dtype=jnp.float32))
    ang = pos.astype(jnp.float32)[:, None] * inv[None, :]
    cos = jnp.cos(ang)[None, :, None, :]
    sin = jnp.sin(ang)[None, :, None, :]
    x1, x2 = x[..., :half], x[..., half:]
    return jnp.concatenate([x1 * cos - x2 * sin, x1 * sin + x2 * cos], axis=-1)


def to_chunks(t, n, c):
    b, _, h = t.shape[:3]
    t = t.reshape((b, n, c, h) + t.shape[3:])
    perm = (1, 0, 3, 2) + tuple(range(4, t.ndim))
    return t.transpose(perm)


def from_chunks(t):
    n, b, h, c, d = t.shape
    return t.transpose(1, 0, 3, 2, 4).reshape(b, n * c, h, d)


def retention_chunked(q, k, v, s0):
    _, L, H, _ = q.shape
    C = math.gcd(CHUNK, L)
    N = L // C
    log_g = jnp.log1p(-jnp.exp2(-5.0 - jnp.arange(H, dtype=jnp.float32)))
    idx = jnp.arange(C, dtype=jnp.float32)
    diff = idx[:, None] - idx[None, :]
    causal = diff >= 0
    dmask = jnp.where(causal[None], jnp.exp(log_g[:, None, None] * jnp.where(causal, diff, 0.0)[None]), 0.0)
    q_decay = jnp.exp(log_g[:, None] * (idx[None, :] + 1.0))[None, :, :, None]
    k_decay = jnp.exp(log_g[:, None] * (C - 1.0 - idx)[None, :])[None, :, :, None]
    chunk_decay = jnp.exp(log_g * C)[None, :, None, None]

    def step(s, inp):
        qi, ki, vi = inp
        scores = jnp.einsum('bhid,bhjd->bhij', qi, ki) * dmask
        o = jnp.einsum('bhij,bhjv->bhiv', scores, vi) + jnp.einsum('bhid,bhdv->bhiv', qi, s) * q_decay
        s = s * chunk_decay + jnp.einsum('bhjd,bhjv->bhdv', ki * k_decay, vi)
        return s, o

    s_fin, out = lax.scan(step, s0, (to_chunks(q, N, C), to_chunks(k, N, C), to_chunks(v, N, C)))
    return from_chunks(out), s_fin


def retention_mixer(h, w_in, w_out, s0, pos0):
    B, L, _ = h.shape
    f32 = jnp.float32
    q, k, v, g = jnp.split(h @ w_in, [RET_QK, 2 * RET_QK, 2 * RET_QK + RET_V], axis=-1)
    pos = pos0 + jnp.arange(L)
    q = rotary(q.reshape(B, L, RET_HEADS, RET_DK).astype(f32), pos) * (RET_DK ** -0.5)
    k = rotary(k.reshape(B, L, RET_HEADS, RET_DK).astype(f32), pos)
    v = v.reshape(B, L, RET_HEADS, RET_DV).astype(f32)
    o, s = retention_chunked(q, k, v, s0.astype(f32))
    o = rmsnorm(o).reshape(B, L, RET_V)
    o = jax.nn.silu(g.astype(f32)) * o
    return o.astype(h.dtype) @ w_out, s.astype(s0.dtype)


def causal_conv(u, buf, w):
    L = u.shape[1]
    full = jnp.concatenate([buf.astype(u.dtype), u], axis=1)
    out = full[:, 0:L] * w[0]
    for j in range(1, CONV_W):
        out = out + full[:, j:j + L] * w[j]
    return jax.nn.silu(out), full[:, -(CONV_W - 1):]


def gated_delta_chunked(q, k, v, g, beta, s0):
    _, L, H, dk = q.shape
    dv = v.shape[-1]
    C = math.gcd(CHUNK, L)
    N = L // C
    qc, kc, vc = to_chunks(q, N, C), to_chunks(k, N, C), to_chunks(v, N, C)
    bc = to_chunks(beta, N, C)[..., None]
    gc = jnp.cumsum(to_chunks(g, N, C), axis=-1)
    tri = jnp.tril(jnp.ones((C, C), dtype=bool))
    strict = jnp.tril(jnp.ones((C, C), dtype=bool), -1)
    decay = jnp.exp(jnp.where(tri, gc[..., :, None] - gc[..., None, :], -jnp.inf))
    kb = kc * bc
    lmat = jnp.where(strict, jnp.einsum('nbhid,nbhjd->nbhij', kb, kc) * decay, 0.0)
    a_mat = lmat + jnp.eye(C, dtype=lmat.dtype)
    rhs = jnp.concatenate([vc * bc, kb * jnp.exp(gc)[..., None]], axis=-1)
    sol = lax.linalg.triangular_solve(a_mat, rhs, left_side=True, lower=True, unit_diagonal=True)
    uc, wc = sol[..., :dv], sol[..., dv:]
    attn = jnp.where(tri, jnp.einsum('nbhid,nbhjd->nbhij', qc, kc) * decay, 0.0)

    def step(s, inp):
        qi, ki, ui, wi, gi, ai = inp
        v_new = ui - jnp.einsum('bhcd,bhdv->bhcv', wi, s)
        o = jnp.einsum('bhcd,bhdv->bhcv', qi * jnp.exp(gi)[..., None], s) + jnp.einsum('bhij,bhjv->bhiv', ai, v_new)
        g_last = gi[..., -1:]
        s = s * jnp.exp(g_last)[..., None] + jnp.einsum('bhcd,bhcv->bhdv', ki * jnp.exp(g_last - gi)[..., None], v_new)
        return s, o

    s_fin, out = lax.scan(step, s0, (qc, kc, uc, wc, gc, attn))
    return from_chunks(out), s_fin


def gdn_mixer(h, w_in, conv_w, a_log, dt_bias, norm_w, w_out, s0, buf0):
    B, L, _ = h.shape
    f32 = jnp.float32
    qkv, z, a, b = jnp.split(h @ w_in, [GDN_CONV_CH, GDN_CONV_CH + GDN_V, GDN_CONV_CH + GDN_V + GDN_V_HEADS], axis=-1)
    qkv_c, buf = causal_conv(qkv, buf0, conv_w)
    q, k, v = jnp.split(qkv_c.astype(f32), [GDN_QK, 2 * GDN_QK], axis=-1)
    rep = GDN_V_HEADS // GDN_K_HEADS
    q = jnp.repeat(l2norm(q.reshape(B, L, GDN_K_HEADS, GDN_DK)), rep, axis=2) * (GDN_DK ** -0.5)
    k = jnp.repeat(l2norm(k.reshape(B, L, GDN_K_HEADS, GDN_DK)), rep, axis=2)
    v = v.reshape(B, L, GDN_V_HEADS, GDN_DV)
    g = -jnp.exp(a_log.astype(f32)) * jax.nn.softplus(a.astype(f32) + dt_bias.astype(f32))
    beta = jax.nn.sigmoid(b.astype(f32))
    o, s = gated_delta_chunked(q, k, v, g, beta, s0.astype(f32))
    o = rmsnorm(o, norm_w) * jax.nn.silu(z.reshape(B, L, GDN_V_HEADS, GDN_DV).astype(f32))
    return o.reshape(B, L, GDN_V).astype(h.dtype) @ w_out, s.astype(s0.dtype), buf.astype(buf0.dtype)


def trunk(x, c, pos0, ret_state, gdn_state, conv_state, w_ada, b_ada, norm_pre, norm_post,
          ffn_w_gu, ffn_w_down, ret_w_in, ret_w_out, gdn_w_in, gdn_conv_w, gdn_a_log,
          gdn_dt_bias, gdn_norm_w, gdn_w_out):
    B = x.shape[0]
    cs = jax.nn.silu(c)
    new_ret, new_gdn, new_conv = [], [], []
    for i in range(DEPTH):
        mod = (cs @ w_ada[i] + b_ada[i]).reshape(B, N_SUB, 3, 1, D_MODEL)
        h = adaln_in(x, mod, 0, norm_pre[i, 0])
        x = x + FFN_RES * adaln_out(swiglu(h, ffn_w_gu[i, 0], ffn_w_down[i, 0]), mod, 0, norm_post[i, 0])
        h = adaln_in(x, mod, 1, norm_pre[i, 1])
        r = i // N_MIXERS
        if i % N_MIXERS == 0:
            y, s = retention_mixer(h, ret_w_in[r], ret_w_out[r], ret_state[r], pos0)
            new_ret.append(s)
        else:
            y, s, buf = gdn_mixer(h, gdn_w_in[r], gdn_conv_w[r], gdn_a_log[r], gdn_dt_bias[r],
                                  gdn_norm_w[r], gdn_w_out[r], gdn_state[r], conv_state[r])
            new_gdn.append(s)
            new_conv.append(buf)
        x = x + adaln_out(y, mod, 1, norm_post[i, 1])
        h = adaln_in(x, mod, 2, norm_pre[i, 2])
        x = x + FFN_RES * adaln_out(swiglu(h, ffn_w_gu[i, 1], ffn_w_down[i, 1]), mod, 2, norm_post[i, 2])
    return x, jnp.stack(new_ret), jnp.stack(new_gdn), jnp.stack(new_conv)


def setup_inputs(seed: int = 0) -> dict:
    key = jax.random.key(seed)
    ks = jax.random.split(key, 24)
    f32 = jnp.float32
    nrm = lambda k, shape, s: jax.random.normal(k, shape, f32) * s
    dt = jnp.exp(jax.random.uniform(ks[20], (N_GDN, GDN_V_HEADS), f32, math.log(0.001), math.log(0.1)))
    return {
        'x_prompt': nrm(ks[0], (BATCH, SEQ, D_MODEL), 1.0),
        'x_sample': nrm(ks[1], (DEC_BATCH, DEC_SEQ, D_MODEL), 1.0),
        'c_prompt': nrm(ks[2], (BATCH, D_MODEL), 1.0),
        'c_sample': nrm(ks[3], (DEC_BATCH, D_MODEL), 1.0),
        'state_ret': nrm(ks[4], (N_RET, DEC_BATCH, RET_HEADS, RET_DK, RET_DV), 1.0),
        'state_gdn': nrm(ks[5], (N_GDN, DEC_BATCH, GDN_V_HEADS, GDN_DK, GDN_DV), 0.3),
        'state_conv': nrm(ks[6], (N_GDN, DEC_BATCH, CONV_W - 1, GDN_CONV_CH), 1.0),
        'w_ada': nrm(ks[7], (DEPTH, D_MODEL, N_SUB * 3 * D_MODEL), D_MODEL ** -0.5),
        'b_ada': nrm(ks[8], (DEPTH, N_SUB * 3 * D_MODEL), 0.02),
        'norm_pre': 1.0 + nrm(ks[9], (DEPTH, N_SUB, D_MODEL), 0.05),
        'norm_post': 1.0 + nrm(ks[10], (DEPTH, N_SUB, D_MODEL), 0.05),
        'ffn_w_gu': nrm(ks[11], (DEPTH, 2, D_MODEL, 2 * D_FF), D_MODEL ** -0.5),
        'ffn_w_down': nrm(ks[12], (DEPTH, 2, D_FF, D_MODEL), D_FF ** -0.5),
        'ret_w_in': nrm(ks[13], (N_RET, D_MODEL, RET_IN), D_MODEL ** -0.5),
        'ret_w_out': nrm(ks[14], (N_RET, RET_V, D_MODEL), RET_V ** -0.5),
        'gdn_w_in': nrm(ks[15], (N_GDN, D_MODEL, GDN_IN), D_MODEL ** -0.5),
        'gdn_conv_w': nrm(ks[16], (N_GDN, CONV_W, GDN_CONV_CH), CONV_W ** -0.5),
        'gdn_a_log': jnp.log(jax.random.uniform(ks[17], (N_GDN, GDN_V_HEADS), f32, 1.0, 16.0)),
        'gdn_dt_bias': dt + jnp.log(-jnp.expm1(-dt)),
        'gdn_norm_w': 1.0 + nrm(ks[18], (N_GDN, GDN_DV), 0.05),
        'gdn_w_out': nrm(ks[19], (N_GDN, GDN_V, D_MODEL), GDN_V ** -0.5),
    }


def reference(x_prompt, x_sample, c_prompt, c_sample, state_ret, state_gdn, state_conv,
              w_ada, b_ada, norm_pre, norm_post, ffn_w_gu, ffn_w_down, ret_w_in, ret_w_out,
              gdn_w_in, gdn_conv_w, gdn_a_log, gdn_dt_bias, gdn_norm_w, gdn_w_out):
    B = x_prompt.shape[0]
    dt = x_prompt.dtype
    ret0 = jnp.zeros((N_RET, B, RET_HEADS, RET_DK, RET_DV), dt)
    gdn0 = jnp.zeros((N_GDN, B, GDN_V_HEADS, GDN_DK, GDN_DV), dt)
    conv0 = jnp.zeros((N_GDN, B, CONV_W - 1, GDN_CONV_CH), dt)
    weights = (w_ada, b_ada, norm_pre, norm_post, ffn_w_gu, ffn_w_down, ret_w_in, ret_w_out,
               gdn_w_in, gdn_conv_w, gdn_a_log, gdn_dt_bias, gdn_norm_w, gdn_w_out)
    y_prompt, ret_p, gdn_p, conv_p = trunk(x_prompt, c_prompt, 0, ret0, gdn0, conv0, *weights)
    y_sample, ret_s, gdn_s, conv_s = trunk(x_sample, c_sample, PAST_LEN, state_ret, state_gdn, state_conv, *weights)
    return (y_prompt, y_sample, ret_p, ret_s, gdn_p, gdn_s, conv_p, conv_s)
```

```python
import functools
import math

import jax
import jax.numpy as jnp
from jax import lax
from jax.experimental import pallas as pl
from jax.experimental.pallas import tpu as pltpu

F32 = jnp.float32
BF16 = jnp.bfloat16
HIGHEST = lax.Precision.HIGHEST

EPS = 1e-6
ROPE_BASE = 10000.0
FFN_RES = 0.5
N_SUB = 3
CONV_W = 4
PAST_LEN = 16384

RET_HEADS = 8
RET_DK = 256
RET_DV = 512
GDN_K_HEADS = 16
GDN_V_HEADS = 32
GDN_DK = 128
GDN_DV = 128
GDN_QK = GDN_K_HEADS * GDN_DK
GDN_V = GDN_V_HEADS * GDN_DV
GDN_CONV_CH = 2 * GDN_QK + GDN_V

LANES = 128
SUBLANES = 8
VMEM_LIMIT = 56 * 1024 * 1024


def _params(sem):
    return pltpu.CompilerParams(dimension_semantics=sem, vmem_limit_bytes=VMEM_LIMIT)


def _silu(x):
    return x * jax.nn.sigmoid(x)


def _dot(a, b):
    return jnp.dot(a.astype(BF16), b.astype(BF16), preferred_element_type=F32)


def _dot_nt(a, b):
    return lax.dot_general(a.astype(BF16), b.astype(BF16), (((1,), (1,)), ((), ())),
                           preferred_element_type=F32)


def _dot_tn(a, b):
    return lax.dot_general(a.astype(BF16), b.astype(BF16), (((0,), (0,)), ((), ())),
                           preferred_element_type=F32)


def _ada_kernel(cs_ref, cp_ref, w_ref, b_ref, os_ref, op_ref):
    w = w_ref[...].astype(BF16)
    b = b_ref[...]
    os_ref[...] = jnp.dot(_silu(cs_ref[...]).astype(BF16), w, preferred_element_type=F32) + b
    op_ref[...] = jnp.dot(_silu(cp_ref[...]).astype(BF16), w, preferred_element_type=F32) + b


def _ada(cs, cp, w, b, tn=512):
    ms, d = cs.shape
    mp = cp.shape[0]
    n = w.shape[1]
    return pl.pallas_call(
        _ada_kernel,
        grid=(n // tn,),
        in_specs=[pl.BlockSpec((ms, d), lambda j: (0, 0)),
                  pl.BlockSpec((mp, d), lambda j: (0, 0)),
                  pl.BlockSpec((d, tn), lambda j: (0, j)),
                  pl.BlockSpec((1, tn), lambda j: (0, j))],
        out_specs=[pl.BlockSpec((ms, tn), lambda j: (0, j)),
                   pl.BlockSpec((mp, tn), lambda j: (0, j))],
        out_shape=[jax.ShapeDtypeStruct((ms, n), F32), jax.ShapeDtypeStruct((mp, n), F32)],
        compiler_params=_params(("arbitrary",)),
        name="ada_table",
    )(cs, cp, w, b.reshape(1, n))


def _mod_rows(ref, per_row, tiles_per_batch):
    if per_row:
        return ref[...]
    return ref[pl.ds(pl.program_id(0) // tiles_per_batch, 1), :]


def _mod_mm_kernel(x_ref, sh_ref, sc_ref, nw_ref, *rest, glu, per_row, tiles_per_batch):
    if glu:
        wg_ref, wu_ref, o_ref, h_scr = rest
    else:
        w_ref, o_ref, h_scr = rest

    @pl.when(pl.program_id(1) == 0)
    def _():
        x = x_ref[...]
        y = x * lax.rsqrt(jnp.mean(x * x, -1, keepdims=True) + EPS) * nw_ref[...]
        sc = _mod_rows(sc_ref, per_row, tiles_per_batch)
        sh = _mod_rows(sh_ref, per_row, tiles_per_batch)
        h_scr[...] = (y * (1.0 + sc) + sh).astype(BF16)

    h = h_scr[...]
    if glu:
        g = jnp.dot(h, wg_ref[...], preferred_element_type=F32)
        u = jnp.dot(h, wu_ref[...], preferred_element_type=F32)
        o_ref[...] = (_silu(g) * u).astype(o_ref.dtype)
    else:
        o_ref[...] = jnp.dot(h, w_ref[...], preferred_element_type=F32).astype(o_ref.dtype)


def _mod_mm(x, mod, sub, nw, w, *, glu, per_row, rows_per_batch, out_dtype, tm, tn):
    t, d = x.shape
    n = w.shape[1] // 2 if glu else w.shape[1]
    nj = n // tn
    tiles_per_batch = max(rows_per_batch // tm, 1)
    mrows = tm if per_row else mod.shape[0]

    def mod_spec(c):
        col = sub * 3 + c
        if per_row:
            return pl.BlockSpec((mrows, d), lambda i, j: (i, col))
        return pl.BlockSpec((mrows, d), lambda i, j: (0, col))

    in_specs = [pl.BlockSpec((tm, d), lambda i, j: (i, 0)),
                mod_spec(0), mod_spec(1),
                pl.BlockSpec((1, d), lambda i, j: (0, 0))]
    args = [x, mod, mod, nw.reshape(1, d)]
    if glu:
        in_specs += [pl.BlockSpec((d, tn), lambda i, j: (0, j)),
                     pl.BlockSpec((d, tn), lambda i, j: (0, j + nj))]
        args += [w, w]
    else:
        in_specs += [pl.BlockSpec((d, tn), lambda i, j: (0, j))]
        args += [w]
    return pl.pallas_call(
        functools.partial(_mod_mm_kernel, glu=glu, per_row=per_row, tiles_per_batch=tiles_per_batch),
        grid=(t // tm, nj),
        in_specs=in_specs,
        out_specs=pl.BlockSpec((tm, tn), lambda i, j: (i, j)),
        out_shape=jax.ShapeDtypeStruct((t, n), out_dtype),
        scratch_shapes=[pltpu.VMEM((tm, d), BF16)],
        compiler_params=_params(("parallel", "arbitrary")),
        name="mod_mm_glu" if glu else "mod_mm",
    )(*args)


def _mm_out_kernel(a_ref, w_ref, x_ref, g_ref, nw_ref, o_ref, acc, *, res_scale, per_row, tiles_per_batch):
    k = pl.program_id(1)

    @pl.when(k == 0)
    def _():
        acc[...] = jnp.zeros_like(acc)

    acc[...] += jnp.dot(a_ref[...].astype(BF16), w_ref[...], preferred_element_type=F32)

    @pl.when(k == pl.num_programs(1) - 1)
    def _():
        y = acc[...]
        yn = y * lax.rsqrt(jnp.mean(y * y, -1, keepdims=True) + EPS) * nw_ref[...]
        gate = _mod_rows(g_ref, per_row, tiles_per_batch)
        o_ref[...] = x_ref[...] + res_scale * (gate * yn)


def _mm_out(a, w, x, mod, sub, nw, *, res_scale, per_row, rows_per_batch, tm, tk):
    t, kdim = a.shape
    d = w.shape[1]
    tiles_per_batch = max(rows_per_batch // tm, 1)
    col = sub * 3 + 2
    if per_row:
        g_spec = pl.BlockSpec((tm, d), lambda i, k: (i, col))
    else:
        g_spec = pl.BlockSpec((mod.shape[0], d), lambda i, k: (0, col))
    return pl.pallas_call(
        functools.partial(_mm_out_kernel, res_scale=res_scale, per_row=per_row,
                          tiles_per_batch=tiles_per_batch),
        grid=(t // tm, kdim // tk),
        in_specs=[pl.BlockSpec((tm, tk), lambda i, k: (i, k)),
                  pl.BlockSpec((tk, d), lambda i, k: (k, 0)),
                  pl.BlockSpec((tm, d), lambda i, k: (i, 0)),
                  g_spec,
                  pl.BlockSpec((1, d), lambda i, k: (0, 0))],
        out_specs=pl.BlockSpec((tm, d), lambda i, k: (i, 0)),
        out_shape=jax.ShapeDtypeStruct((t, d), F32),
        scratch_shapes=[pltpu.VMEM((tm, d), F32)],
        compiler_params=_params(("parallel", "arbitrary")),
        name="mm_out",
    )(a, w, x, mod, nw.reshape(1, d))


def _rotate(x, cos, sin):
    half = x.shape[-1] // 2
    x1, x2 = x[:, :half], x[:, half:]
    return jnp.concatenate([x1 * cos - x2 * sin, x1 * sin + x2 * cos], axis=-1)


def _ret_kernel(lg_ref, cos_ref, sin_ref, q_ref, k_ref, v_ref, g_ref, *rest, c, has_state):
    if has_state:
        s0_ref, o_ref, sout_ref, s_scr = rest
    else:
        o_ref, sout_ref, s_scr = rest
    n = pl.program_id(2)

    @pl.when(n == 0)
    def _():
        if has_state:
            s_scr[...] = s0_ref[...]
        else:
            s_scr[...] = jnp.zeros_like(s_scr)

    lg = lg_ref[0:1, 0:1]
    row = lax.broadcasted_iota(jnp.int32, (c, c), 0)
    colm = lax.broadcasted_iota(jnp.int32, (c, c), 1)
    causal = row >= colm
    diff = jnp.where(causal, row - colm, 0).astype(F32)
    dmask = jnp.where(causal, jnp.exp(lg * diff), 0.0)
    idx = lax.broadcasted_iota(jnp.int32, (c, 1), 0).astype(F32)
    q_decay = jnp.exp(lg * (idx + 1.0))
    k_decay = jnp.exp(lg * (c - 1.0 - idx))
    chunk_decay = jnp.exp(lg * c)

    cos, sin = cos_ref[...], sin_ref[...]
    q = _rotate(q_ref[...], cos, sin) * (RET_DK ** -0.5)
    k = _rotate(k_ref[...], cos, sin)
    v = v_ref[...]
    s = s_scr[...]
    scores = _dot_nt(q, k) * dmask
    o = _dot(scores, v) + _dot(q, s) * q_decay
    s_new = s * chunk_decay + _dot_tn(k * k_decay, v)
    s_scr[...] = s_new

    on = o * lax.rsqrt(jnp.mean(o * o, -1, keepdims=True) + EPS)
    o_ref[...] = (_silu(g_ref[...]) * on).astype(o_ref.dtype)

    @pl.when(n == pl.num_programs(2) - 1)
    def _():
        sout_ref[...] = s_new


def _retention(proj, s0, pos0, batch, seq, c):
    h, dk, dv = RET_HEADS, RET_DK, RET_DV
    nc = seq // c
    half = dk // 2
    log_g = jnp.log1p(-jnp.exp2(-5.0 - jnp.arange(h, dtype=F32)))
    lg_tab = jnp.broadcast_to(log_g[:, None, None], (h, SUBLANES, LANES))
    inv = 1.0 / (ROPE_BASE ** jnp.linspace(0.0, 1.0, half, dtype=F32))
    ang = (pos0 + jnp.arange(seq)).astype(F32)[:, None] * inv[None, :]
    cos, sin = jnp.cos(ang), jnp.sin(ang)
    has_state = s0 is not None
    in_specs = [pl.BlockSpec((None, SUBLANES, LANES), lambda b, hh, n: (hh, 0, 0)),
                pl.BlockSpec((c, half), lambda b, hh, n: (n, 0)),
                pl.BlockSpec((c, half), lambda b, hh, n: (n, 0)),
                pl.BlockSpec((c, dk), lambda b, hh, n: (b * nc + n, hh)),
                pl.BlockSpec((c, dk), lambda b, hh, n: (b * nc + n, h + hh)),
                pl.BlockSpec((c, dv), lambda b, hh, n: (b * nc + n, h + hh)),
                pl.BlockSpec((c, dv), lambda b, hh, n: (b * nc + n, 2 * h + hh))]
    args = [lg_tab, cos, sin, proj, proj, proj, proj]
    if has_state:
        in_specs.append(pl.BlockSpec((None, None, dk, dv), lambda b, hh, n: (b, hh, 0, 0)))
        args.append(s0)
    return pl.pallas_call(
        functools.partial(_ret_kernel, c=c, has_state=has_state),
        grid=(batch, h, nc),
        in_specs=in_specs,
        out_specs=[pl.BlockSpec((c, dv), lambda b, hh, n: (b * nc + n, hh)),
                   pl.BlockSpec((None, None, dk, dv), lambda b, hh, n: (b, hh, 0, 0))],
        out_shape=[jax.ShapeDtypeStruct((batch * seq, h * dv), F32),
                   jax.ShapeDtypeStruct((batch, h, dk, dv), F32)],
        scratch_shapes=[pltpu.VMEM((dk, dv), F32)],
        compiler_params=_params(("parallel", "parallel", "arbitrary")),
        name="retention",
    )(*args)


def _unit_lower_inverse(lmat, c):
    row = lax.broadcasted_iota(jnp.int32, (c, c), 0)
    colm = lax.broadcasted_iota(jnp.int32, (c, c), 1)
    eye = (row == colm).astype(F32)
    x = eye - lmat
    p = lmat
    for _ in range(int(math.log2(c)) - 1):
        p = jnp.dot(p, p, precision=HIGHEST, preferred_element_type=F32)
        x = x + jnp.dot(x, p, precision=HIGHEST, preferred_element_type=F32)
    return x


def _conv_silu(buf_scr, u, w, c):
    buf_scr[pl.ds(SUBLANES, c), :] = u
    out = buf_scr[pl.ds(SUBLANES - 3, c), :] * w[0:1, :]
    for j in range(1, CONV_W):
        out = out + buf_scr[pl.ds(SUBLANES - 3 + j, c), :] * w[j:j + 1, :]
    buf_scr[pl.ds(SUBLANES - 3, 3), :] = buf_scr[pl.ds(SUBLANES + c - 3, 3), :]
    return _silu(out)


def _gdn_kernel(q_ref, k_ref, v_ref, z_ref, ab_ref, wq_ref, wk_ref, wv_ref, alog_ref, dt_ref, nw_ref,
                *rest, c, has_state):
    if has_state:
        s0_ref, cq_ref, ck_ref, cv_ref, o_ref, sout_ref, s_scr, bq, bk, bv = rest
    else:
        o_ref, sout_ref, s_scr, bq, bk, bv = rest
    j = pl.program_id(1)
    n = pl.program_id(2)

    @pl.when(n == 0)
    def _():
        if has_state:
            s_scr[...] = s0_ref[...]
            bq[pl.ds(SUBLANES - 3, 3), :] = cq_ref[...]
            bk[pl.ds(SUBLANES - 3, 3), :] = ck_ref[...]
            bv[pl.ds(SUBLANES - 3, 3), :] = cv_ref[...]
        else:
            s_scr[...] = jnp.zeros_like(s_scr)
            bq[pl.ds(0, SUBLANES), :] = jnp.zeros((SUBLANES, GDN_DK), F32)
            bk[pl.ds(0, SUBLANES), :] = jnp.zeros((SUBLANES, GDN_DK), F32)
            bv[pl.ds(0, SUBLANES), :] = jnp.zeros((SUBLANES, 2 * GDN_DV), F32)

    q = _conv_silu(bq, q_ref[...], wq_ref[...], c)
    k = _conv_silu(bk, k_ref[...], wk_ref[...], c)
    v2 = _conv_silu(bv, v_ref[...], wv_ref[...], c)
    q = q * lax.rsqrt(jnp.sum(q * q, -1, keepdims=True) + EPS) * (GDN_DK ** -0.5)
    k = k * lax.rsqrt(jnp.sum(k * k, -1, keepdims=True) + EPS)

    ab = ab_ref[...]
    g_all = -jnp.exp(alog_ref[...]) * jax.nn.softplus(ab + dt_ref[...])
    beta_all = jax.nn.sigmoid(ab)
    row = lax.broadcasted_iota(jnp.int32, (c, c), 0)
    colm = lax.broadcasted_iota(jnp.int32, (c, c), 1)
    tri = row >= colm
    strict = row > colm
    gc_all = jnp.dot(tri.astype(F32), g_all, precision=HIGHEST, preferred_element_type=F32)
    lane = lax.broadcasted_iota(jnp.int32, (1, LANES), 1)
    sub8 = lax.broadcasted_iota(jnp.int32, (SUBLANES, LANES), 0)
    lane8 = lax.broadcasted_iota(jnp.int32, (SUBLANES, LANES), 1)
    sel = (lane8 == 2 * j + sub8).astype(F32)
    gc_rows = lax.dot_general(sel, gc_all, (((1,), (1,)), ((), ())), precision=HIGHEST,
                              preferred_element_type=F32)
    zg = z_ref[...]
    nw = nw_ref[...]
    for e in range(2):
        pick = (lane == 2 * j + e).astype(F32)
        gc = jnp.sum(gc_all * pick, -1, keepdims=True)
        beta = jnp.sum(beta_all * (lane == GDN_V_HEADS + 2 * j + e).astype(F32), -1, keepdims=True)
        gc_row = gc_rows[e:e + 1, :]
        decay = jnp.where(tri, jnp.exp(jnp.where(tri, gc - gc_row, 0.0)), 0.0)
        v = v2[:, e * GDN_DV:(e + 1) * GDN_DV]
        kb = k * beta
        lmat = jnp.where(strict, _dot_nt(kb, k) * decay, 0.0)
        rhs = jnp.concatenate([v * beta, kb * jnp.exp(gc)], axis=-1)
        sol = jnp.dot(_unit_lower_inverse(lmat, c), rhs, precision=HIGHEST, preferred_element_type=F32)
        u, w = sol[:, :GDN_DV], sol[:, GDN_DV:]
        attn = jnp.where(tri, _dot_nt(q, k) * decay, 0.0)

        s = s_scr[e]
        v_new = u - _dot(w, s)
        o = _dot(q * jnp.exp(gc), s) + _dot(attn, v_new)
        g_last = gc[c - 1:c, :]
        s_new = s * jnp.exp(g_last) + _dot_tn(k * jnp.exp(g_last - gc), v_new)
        s_scr[e] = s_new

        on = o * lax.rsqrt(jnp.mean(o * o, -1, keepdims=True) + EPS) * nw
        o_ref[:, e * GDN_DV:(e + 1) * GDN_DV] = (
            on * _silu(zg[:, e * GDN_DV:(e + 1) * GDN_DV])).astype(o_ref.dtype)

    @pl.when(n == pl.num_programs(2) - 1)
    def _():
        sout_ref[...] = s_scr[...]


def _gdn(proj, conv_w, a_log, dt_bias, norm_w, s0, buf0, batch, seq, c):
    hk, dk, dv = GDN_K_HEADS, GDN_DK, GDN_DV
    nc = seq // c
    kq0 = GDN_QK // dk
    v0 = 2 * GDN_QK // (2 * dv)
    z0 = GDN_CONV_CH // (2 * dv)
    ab0 = (GDN_CONV_CH + GDN_V) // LANES
    pad = LANES - 2 * GDN_V_HEADS
    alog_row = jnp.concatenate([a_log, jnp.zeros((GDN_V_HEADS + pad,), F32)]).reshape(1, LANES)
    dt_row = jnp.concatenate([dt_bias, jnp.zeros((GDN_V_HEADS + pad,), F32)]).reshape(1, LANES)
    has_state = s0 is not None
    tok = lambda b, j, n: b * nc + n
    in_specs = [pl.BlockSpec((c, dk), lambda b, j, n: (tok(b, j, n), j)),
                pl.BlockSpec((c, dk), lambda b, j, n: (tok(b, j, n), kq0 + j)),
                pl.BlockSpec((c, 2 * dv), lambda b, j, n: (tok(b, j, n), v0 + j)),
                pl.BlockSpec((c, 2 * dv), lambda b, j, n: (tok(b, j, n), z0 + j)),
                pl.BlockSpec((c, LANES), lambda b, j, n: (tok(b, j, n), ab0)),
                pl.BlockSpec((CONV_W, dk), lambda b, j, n: (0, j)),
                pl.BlockSpec((CONV_W, dk), lambda b, j, n: (0, kq0 + j)),
                pl.BlockSpec((CONV_W, 2 * dv), lambda b, j, n: (0, v0 + j)),
                pl.BlockSpec((1, LANES), lambda b, j, n: (0, 0)),
                pl.BlockSpec((1, LANES), lambda b, j, n: (0, 0)),
                pl.BlockSpec((1, dv), lambda b, j, n: (0, 0))]
    args = [proj, proj, proj, proj, proj, conv_w, conv_w, conv_w, alog_row, dt_row, norm_w.reshape(1, dv)]
    if has_state:
        in_specs += [pl.BlockSpec((None, 2, dk, dv), lambda b, j, n: (b, j, 0, 0)),
                     pl.BlockSpec((None, CONV_W - 1, dk), lambda b, j, n: (b, 0, j)),
                     pl.BlockSpec((None, CONV_W - 1, dk), lambda b, j, n: (b, 0, kq0 + j)),
                     pl.BlockSpec((None, CONV_W - 1, 2 * dv), lambda b, j, n: (b, 0, v0 + j))]
        args += [s0, buf0, buf0, buf0]
    return pl.pallas_call(
        functools.partial(_gdn_kernel, c=c, has_state=has_state),
        grid=(batch, hk, nc),
        in_specs=in_specs,
        out_specs=[pl.BlockSpec((c, 2 * dv), lambda b, j, n: (tok(b, j, n), j)),
                   pl.BlockSpec((None, 2, dk, dv), lambda b, j, n: (b, j, 0, 0))],
        out_shape=[jax.ShapeDtypeStruct((batch * seq, GDN_V), F32),
                   jax.ShapeDtypeStruct((batch, GDN_V_HEADS, dk, dv), F32)],
        scratch_shapes=[pltpu.VMEM((2, dk, dv), F32),
                        pltpu.VMEM((SUBLANES + c, dk), F32),
                        pltpu.VMEM((SUBLANES + c, dk), F32),
                        pltpu.VMEM((SUBLANES + c, 2 * dv), F32)],
        compiler_params=_params(("parallel", "parallel", "arbitrary")),
        name="gated_delta",
    )(*args)


def _trunk(x, mod_fn, per_row, batch, seq, pos0, ret_state, gdn_state, conv_state, wts, tm):
    (norm_pre, norm_post, w_gu, w_down, ret_w_in, ret_w_out, gdn_w_in, gdn_conv_w, gdn_a_log,
     gdn_dt_bias, gdn_norm_w, gdn_w_out) = wts
    kw = dict(per_row=per_row, rows_per_batch=seq)
    depth = norm_pre.shape[0]
    new_ret = new_gdn = new_conv = None
    for i in range(depth):
        mod = mod_fn(i)

        def ffn(x, sub, f):
            a = _mod_mm(x, mod, sub, norm_pre[i, sub], w_gu[i, f], glu=True, out_dtype=BF16,
                        tm=tm, tn=512, **kw)
            return _mm_out(a, w_down[i, f], x, mod, sub, norm_post[i, sub], res_scale=FFN_RES,
                           tm=min(tm, 512), tk=512, **kw)

        x = ffn(x, 0, 0)
        if i % 2 == 0:
            proj = _mod_mm(x, mod, 1, norm_pre[i, 1], ret_w_in, glu=False, out_dtype=F32,
                           tm=tm, tn=512, **kw)
            y, new_ret = _retention(proj, ret_state, pos0, batch, seq, c=math.gcd(256, seq))
            x = _mm_out(y, ret_w_out, x, mod, 1, norm_post[i, 1], res_scale=1.0,
                        tm=min(tm, 512), tk=512, **kw)
        else:
            proj = _mod_mm(x, mod, 1, norm_pre[i, 1], gdn_w_in, glu=False, out_dtype=F32,
                           tm=tm, tn=512, **kw)
            y, new_gdn = _gdn(proj, gdn_conv_w, gdn_a_log, gdn_dt_bias, gdn_norm_w, gdn_state,
                              conv_state, batch, seq, c=math.gcd(64, seq))
            new_conv = proj.reshape(batch, seq, -1)[:, seq - (CONV_W - 1):, :GDN_CONV_CH]
            x = _mm_out(y, gdn_w_out, x, mod, 1, norm_post[i, 1], res_scale=1.0,
                        tm=min(tm, 512), tk=512, **kw)
        x = ffn(x, 2, 1)
    return x, new_ret, new_gdn, new_conv


def kernel(x_prompt, x_sample, c_prompt, c_sample, state_ret, state_gdn, state_conv, w_ada, b_ada,
           norm_pre, norm_post, ffn_w_gu, ffn_w_down, ret_w_in, ret_w_out, gdn_w_in, gdn_conv_w,
           gdn_a_log, gdn_dt_bias, gdn_norm_w, gdn_w_out):
    bp, lp, d = x_prompt.shape
    bs, ls, _ = x_sample.shape
    n_in = gdn_w_in.shape[-1]
    n_in_pad = -(-n_in // 512) * 512
    wts = (norm_pre, norm_post, ffn_w_gu.astype(BF16), ffn_w_down.astype(BF16),
           ret_w_in[0].astype(BF16), ret_w_out[0].astype(BF16),
           jnp.pad(gdn_w_in[0], ((0, 0), (0, n_in_pad - n_in))).astype(BF16),
           gdn_conv_w[0], gdn_a_log[0], gdn_dt_bias[0], gdn_norm_w[0], gdn_w_out[0].astype(BF16))

    cs_rows = jnp.repeat(c_sample, ls, axis=0)
    cp_rows = jnp.pad(c_prompt, ((0, SUBLANES - bp), (0, 0)))
    mods = [_ada(cs_rows, cp_rows, w_ada[i], b_ada[i]) for i in range(w_ada.shape[0])]

    y_p, ret_p, gdn_p, conv_p = _trunk(
        x_prompt.reshape(bp * lp, d), lambda i: mods[i][1], False, bp, lp, 0,
        None, None, None, wts, tm=1024)
    y_s, ret_s, gdn_s, conv_s = _trunk(
        x_sample.reshape(bs * ls, d), lambda i: mods[i][0], True, bs, ls, PAST_LEN,
        state_ret[0], state_gdn[0], state_conv[0], wts, tm=256)
    return (y_p.reshape(bp, lp, d), y_s.reshape(bs, ls, d), ret_p[None], ret_s[None],
            gdn_p[None], gdn_s[None], conv_p[None], conv_s[None])
```

```python
import functools
import math

import jax
import jax.numpy as jnp
from jax import lax
from jax.experimental import pallas as pl
from jax.experimental.pallas import tpu as pltpu

F32 = jnp.float32
BF16 = jnp.bfloat16
HIGHEST = lax.Precision.HIGHEST

EPS = 1e-6
ROPE_BASE = 10000.0
FFN_RES = 0.5
N_SUB = 3
CONV_W = 4
PAST_LEN = 16384

RET_HEADS = 8
RET_DK = 256
RET_DV = 512
GDN_K_HEADS = 16
GDN_V_HEADS = 32
GDN_DK = 128
GDN_DV = 128
GDN_QK = GDN_K_HEADS * GDN_DK
GDN_V = GDN_V_HEADS * GDN_DV
GDN_CONV_CH = 2 * GDN_QK + GDN_V

LANES = 128
SUBLANES = 8
VMEM_LIMIT = 56 * 1024 * 1024


def _params(sem):
    return pltpu.CompilerParams(dimension_semantics=sem, vmem_limit_bytes=VMEM_LIMIT)


def _silu(x):
    return x * jax.nn.sigmoid(x)


def _dot(a, b):
    return jnp.dot(a.astype(BF16), b.astype(BF16), preferred_element_type=F32)


def _dot_nt(a, b):
    return lax.dot_general(a.astype(BF16), b.astype(BF16), (((1,), (1,)), ((), ())),
                           preferred_element_type=F32)


def _dot_tn(a, b):
    return lax.dot_general(a.astype(BF16), b.astype(BF16), (((0,), (0,)), ((), ())),
                           preferred_element_type=F32)


def _ada_kernel(cs_ref, cp_ref, w_ref, b_ref, os_ref, op_ref):
    w = w_ref[...].astype(BF16)
    b = b_ref[...]
    os_ref[...] = jnp.dot(_silu(cs_ref[...]).astype(BF16), w, preferred_element_type=F32) + b
    op_ref[...] = jnp.dot(_silu(cp_ref[...]).astype(BF16), w, preferred_element_type=F32) + b


def _ada(cs, cp, w, b, layer, tn=512):
    ms, d = cs.shape
    mp = cp.shape[0]
    depth, _, n = w.shape
    return pl.pallas_call(
        _ada_kernel,
        grid=(n // tn,),
        in_specs=[pl.BlockSpec((ms, d), lambda j: (0, 0)),
                  pl.BlockSpec((mp, d), lambda j: (0, 0)),
                  pl.BlockSpec((None, d, tn), lambda j: (layer, 0, j)),
                  pl.BlockSpec((None, 1, tn), lambda j: (layer, 0, j))],
        out_specs=[pl.BlockSpec((ms, tn), lambda j: (0, j)),
                   pl.BlockSpec((mp, tn), lambda j: (0, j))],
        out_shape=[jax.ShapeDtypeStruct((ms, n), F32), jax.ShapeDtypeStruct((mp, n), F32)],
        compiler_params=_params(("arbitrary",)),
        name="ada_table",
    )(cs, cp, w, b.reshape(depth, 1, n))


def _mod_rows(ref, per_row, tiles_per_batch):
    if per_row:
        return ref[...]
    return ref[pl.ds(pl.program_id(0) // tiles_per_batch, 1), :]


def _mod_mm_kernel(x_ref, sh_ref, sc_ref, nw_ref, *rest, glu, per_row, tiles_per_batch):
    if glu:
        wg_ref, wu_ref, o_ref, h_scr = rest
    else:
        w_ref, o_ref, h_scr = rest

    @pl.when(pl.program_id(1) == 0)
    def _():
        x = x_ref[...]
        y = x * lax.rsqrt(jnp.mean(x * x, -1, keepdims=True) + EPS) * nw_ref[...]
        sc = _mod_rows(sc_ref, per_row, tiles_per_batch)
        sh = _mod_rows(sh_ref, per_row, tiles_per_batch)
        h_scr[...] = (y * (1.0 + sc) + sh).astype(BF16)

    h = h_scr[...]
    if glu:
        g = jnp.dot(h, wg_ref[...], preferred_element_type=F32)
        u = jnp.dot(h, wu_ref[...], preferred_element_type=F32)
        o_ref[...] = (_silu(g) * u).astype(o_ref.dtype)
    else:
        o_ref[...] = jnp.dot(h, w_ref[...], preferred_element_type=F32).astype(o_ref.dtype)


def _mod_mm(x, mod, sub, nw, w, widx, *, glu, per_row, rows_per_batch, out_dtype, tm, tn):
    t, d = x.shape
    n = w.shape[-1] // 2 if glu else w.shape[-1]
    nj = n // tn
    lead = (None,) * len(widx)
    tiles_per_batch = max(rows_per_batch // tm, 1)
    mrows = tm if per_row else mod.shape[0]

    def mod_spec(c):
        col = sub * 3 + c
        if per_row:
            return pl.BlockSpec((mrows, d), lambda i, j: (i, col))
        return pl.BlockSpec((mrows, d), lambda i, j: (0, col))

    in_specs = [pl.BlockSpec((tm, d), lambda i, j: (i, 0)),
                mod_spec(0), mod_spec(1),
                pl.BlockSpec((1, d), lambda i, j: (0, 0))]
    args = [x, mod, mod, nw.reshape(1, d)]
    if glu:
        in_specs += [pl.BlockSpec(lead + (d, tn), lambda i, j: widx + (0, j)),
                     pl.BlockSpec(lead + (d, tn), lambda i, j: widx + (0, j + nj))]
        args += [w, w]
    else:
        in_specs += [pl.BlockSpec(lead + (d, tn), lambda i, j: widx + (0, j))]
        args += [w]
    return pl.pallas_call(
        functools.partial(_mod_mm_kernel, glu=glu, per_row=per_row, tiles_per_batch=tiles_per_batch),
        grid=(t // tm, nj),
        in_specs=in_specs,
        out_specs=pl.BlockSpec((tm, tn), lambda i, j: (i, j)),
        out_shape=jax.ShapeDtypeStruct((t, n), out_dtype),
        scratch_shapes=[pltpu.VMEM((tm, d), BF16)],
        compiler_params=_params(("parallel", "arbitrary")),
        name="mod_mm_glu" if glu else "mod_mm",
    )(*args)


def _mm_out_kernel(a_ref, w_ref, x_ref, g_ref, nw_ref, o_ref, acc, *, res_scale, per_row, tiles_per_batch):
    k = pl.program_id(1)

    @pl.when(k == 0)
    def _():
        acc[...] = jnp.zeros_like(acc)

    acc[...] += jnp.dot(a_ref[...].astype(BF16), w_ref[...], preferred_element_type=F32)

    @pl.when(k == pl.num_programs(1) - 1)
    def _():
        y = acc[...]
        yn = y * lax.rsqrt(jnp.mean(y * y, -1, keepdims=True) + EPS) * nw_ref[...]
        gate = _mod_rows(g_ref, per_row, tiles_per_batch)
        o_ref[...] = x_ref[...] + res_scale * (gate * yn)


def _mm_out(a, w, widx, x, mod, sub, nw, *, res_scale, per_row, rows_per_batch, tm, tk):
    t, kdim = a.shape
    d = w.shape[-1]
    lead = (None,) * len(widx)
    tiles_per_batch = max(rows_per_batch // tm, 1)
    col = sub * 3 + 2
    if per_row:
        g_spec = pl.BlockSpec((tm, d), lambda i, k: (i, col))
    else:
        g_spec = pl.BlockSpec((mod.shape[0], d), lambda i, k: (0, col))
    return pl.pallas_call(
        functools.partial(_mm_out_kernel, res_scale=res_scale, per_row=per_row,
                          tiles_per_batch=tiles_per_batch),
        grid=(t // tm, kdim // tk),
        in_specs=[pl.BlockSpec((tm, tk), lambda i, k: (i, k)),
                  pl.BlockSpec(lead + (tk, d), lambda i, k: widx + (k, 0)),
                  pl.BlockSpec((tm, d), lambda i, k: (i, 0)),
                  g_spec,
                  pl.BlockSpec((1, d), lambda i, k: (0, 0))],
        out_specs=pl.BlockSpec((tm, d), lambda i, k: (i, 0)),
        out_shape=jax.ShapeDtypeStruct((t, d), F32),
        scratch_shapes=[pltpu.VMEM((tm, d), F32)],
        compiler_params=_params(("parallel", "arbitrary")),
        name="mm_out",
    )(a, w, x, mod, nw.reshape(1, d))


def _rotate(x, cos, sin):
    half = x.shape[-1] // 2
    x1, x2 = x[:, :half], x[:, half:]
    return jnp.concatenate([x1 * cos - x2 * sin, x1 * sin + x2 * cos], axis=-1)


def _ret_kernel(lg_ref, cos_ref, sin_ref, q_ref, k_ref, v_ref, g_ref, *rest, c, has_state):
    if has_state:
        s0_ref, o_ref, sout_ref, s_scr = rest
    else:
        o_ref, sout_ref, s_scr = rest
    n = pl.program_id(2)

    @pl.when(n == 0)
    def _():
        if has_state:
            s_scr[...] = s0_ref[...]
        else:
            s_scr[...] = jnp.zeros_like(s_scr)

    lg = lg_ref[0:1, 0:1]
    row = lax.broadcasted_iota(jnp.int32, (c, c), 0)
    colm = lax.broadcasted_iota(jnp.int32, (c, c), 1)
    causal = row >= colm
    diff = jnp.where(causal, row - colm, 0).astype(F32)
    dmask = jnp.where(causal, jnp.exp(lg * diff), 0.0)
    idx = lax.broadcasted_iota(jnp.int32, (c, 1), 0).astype(F32)
    q_decay = jnp.exp(lg * (idx + 1.0))
    k_decay = jnp.exp(lg * (c - 1.0 - idx))
    chunk_decay = jnp.exp(lg * c)

    cos, sin = cos_ref[...], sin_ref[...]
    q = _rotate(q_ref[...], cos, sin) * (RET_DK ** -0.5)
    k = _rotate(k_ref[...], cos, sin)
    v = v_ref[...]
    s = s_scr[...]
    scores = _dot_nt(q, k) * dmask
    o = _dot(scores, v) + _dot(q, s) * q_decay
    s_new = s * chunk_decay + _dot_tn(k * k_decay, v)
    s_scr[...] = s_new

    on = o * lax.rsqrt(jnp.mean(o * o, -1, keepdims=True) + EPS)
    o_ref[...] = (_silu(g_ref[...]) * on).astype(o_ref.dtype)

    @pl.when(n == pl.num_programs(2) - 1)
    def _():
        sout_ref[...] = s_new


def _retention(proj, s0, pos0, batch, seq, c):
    h, dk, dv = RET_HEADS, RET_DK, RET_DV
    nc = seq // c
    half = dk // 2
    log_g = jnp.log1p(-jnp.exp2(-5.0 - jnp.arange(h, dtype=F32)))
    lg_tab = jnp.broadcast_to(log_g[:, None, None], (h, SUBLANES, LANES))
    inv = 1.0 / (ROPE_BASE ** jnp.linspace(0.0, 1.0, half, dtype=F32))
    ang = (pos0 + jnp.arange(seq)).astype(F32)[:, None] * inv[None, :]
    cos, sin = jnp.cos(ang), jnp.sin(ang)
    has_state = s0 is not None
    in_specs = [pl.BlockSpec((None, SUBLANES, LANES), lambda b, hh, n: (hh, 0, 0)),
                pl.BlockSpec((c, half), lambda b, hh, n: (n, 0)),
                pl.BlockSpec((c, half), lambda b, hh, n: (n, 0)),
                pl.BlockSpec((c, dk), lambda b, hh, n: (b * nc + n, hh)),
                pl.BlockSpec((c, dk), lambda b, hh, n: (b * nc + n, h + hh)),
                pl.BlockSpec((c, dv), lambda b, hh, n: (b * nc + n, h + hh)),
                pl.BlockSpec((c, dv), lambda b, hh, n: (b * nc + n, 2 * h + hh))]
    args = [lg_tab, cos, sin, proj, proj, proj, proj]
    if has_state:
        in_specs.append(pl.BlockSpec((None, None, dk, dv), lambda b, hh, n: (b, hh, 0, 0)))
        args.append(s0)
    return pl.pallas_call(
        functools.partial(_ret_kernel, c=c, has_state=has_state),
        grid=(batch, h, nc),
        in_specs=in_specs,
        out_specs=[pl.BlockSpec((c, dv), lambda b, hh, n: (b * nc + n, hh)),
                   pl.BlockSpec((None, None, dk, dv), lambda b, hh, n: (b, hh, 0, 0))],
        out_shape=[jax.ShapeDtypeStruct((batch * seq, h * dv), F32),
                   jax.ShapeDtypeStruct((batch, h, dk, dv), F32)],
        scratch_shapes=[pltpu.VMEM((dk, dv), F32)],
        compiler_params=_params(("parallel", "parallel", "arbitrary")),
        name="retention",
    )(*args)


def _gdn_in_weight(w_in):
    d = w_in.shape[0]
    main = w_in[:, :GDN_CONV_CH + GDN_V]
    a = w_in[:, GDN_CONV_CH + GDN_V:GDN_CONV_CH + GDN_V + GDN_V_HEADS].reshape(d, GDN_K_HEADS, 2)
    b = w_in[:, GDN_CONV_CH + GDN_V + GDN_V_HEADS:].reshape(d, GDN_K_HEADS, 2)
    tail = jnp.concatenate([a, b, jnp.zeros((d, GDN_K_HEADS, LANES - 4), w_in.dtype)], axis=-1)
    return jnp.concatenate([main, tail.reshape(d, GDN_K_HEADS * LANES)], axis=-1)


def _gdn_head_tables(a_log, dt_bias):
    def tab(x):
        row = jnp.concatenate([x.reshape(GDN_K_HEADS, 2), jnp.zeros((GDN_K_HEADS, LANES - 2), F32)], -1)
        return jnp.broadcast_to(row[:, None, :], (GDN_K_HEADS, SUBLANES, LANES))
    return tab(a_log), tab(dt_bias)


GDN_C = 64
GDN_SUB = 256


def _chunk_cumsum(x, c):
    pos = lax.broadcasted_iota(jnp.int32, x.shape, 0) % c
    s = 1
    while s < c:
        x = x + jnp.where(pos >= s, pltpu.roll(x, s, axis=0), 0.0)
        s *= 2
    return x


GDN_BASE = 8


def _inverse_masks(c, width):
    i = lax.broadcasted_iota(jnp.int32, (c, width), 0)
    j = lax.broadcasted_iota(jnp.int32, (c, width), 1) % c
    r = lax.broadcasted_iota(jnp.int32, (width, width), 0)
    s = lax.broadcasted_iota(jnp.int32, (width, width), 1)
    masks = {"same": (r // c) == (s // c),
             "base_lanes": (i // GDN_BASE) == (j // GDN_BASE),
             "base_diag": (r // GDN_BASE) == (s // GDN_BASE)}
    b = GDN_BASE
    while b < c:
        masks[f"lanes{b}"] = ((i // (2 * b)) == (j // (2 * b))) & ((i // b) % 2 == 1) & ((j // b) % 2 == 0)
        masks[f"diag{b}"] = ((r // (2 * b)) == (s // (2 * b))) & ((r // b) % 2 == 1) & ((s // b) % 2 == 0)
        b *= 2
    return masks


def _inverse_minus_eye_lanes(l_lb, l_bd, masks):
    c, width = l_lb.shape
    reps = width // c

    def to_diag(p):
        return jnp.where(masks["same"], jnp.concatenate([p] * reps, axis=0), 0.0)

    lb = jnp.where(masks["base_lanes"], l_lb, 0.0)
    x = -lb
    p = _dot(lb, jnp.where(masks["base_diag"], l_bd, 0.0))
    n = 2
    while 2 * n < GDN_BASE:
        xp = _dot(jnp.concatenate([x, p], axis=0), to_diag(p))
        x = x + p + xp[:c]
        p = xp[c:]
        n *= 2
    x = x + p + _dot(x, to_diag(p))

    b = GDN_BASE
    while b < c:
        m_lb = jnp.where(masks[f"lanes{b}"], l_lb, 0.0)
        m_bd = jnp.where(masks[f"diag{b}"], l_bd, 0.0)
        t = m_lb + _dot(x, m_bd)
        x = x - t - _dot(t, to_diag(x))
        b *= 2
    return x


def _gdn_prep_kernel(q_ref, k_ref, v_ref, ab_ref, qp_ref, kp_ref, vp_ref, wq_ref, wk_ref, wv_ref,
                     alog_ref, dt_ref, u_ref, w_ref, qg_ref, kg_ref, at_ref, eg_ref,
                     bq, bk, bv, gc_scr, *, tb):
    c, sub = GDN_C, GDN_SUB
    first = pl.program_id(1) == 0

    def conv_silu(buf, prev_ref, x_ref, cw_ref):
        buf[pl.ds(0, SUBLANES), :] = jnp.where(first, 0.0, prev_ref[...])
        buf[pl.ds(SUBLANES, tb), :] = x_ref[...]
        cw = cw_ref[...]
        out = buf[pl.ds(SUBLANES - 3, tb), :] * cw[0:1, :]
        for j in range(1, CONV_W):
            out = out + buf[pl.ds(SUBLANES - 3 + j, tb), :] * cw[j:j + 1, :]
        return _silu(out)

    q = conv_silu(bq, qp_ref, q_ref, wq_ref)
    k = conv_silu(bk, kp_ref, k_ref, wk_ref)
    v2 = conv_silu(bv, vp_ref, v_ref, wv_ref)
    q = q * lax.rsqrt(jnp.sum(q * q, -1, keepdims=True) + EPS) * (GDN_DK ** -0.5)
    k = k * lax.rsqrt(jnp.sum(k * k, -1, keepdims=True) + EPS)

    ab = ab_ref[...]
    g_all = -jnp.exp(alog_ref[0:1, :]) * jax.nn.softplus(ab + dt_ref[0:1, :])
    beta_all = jax.nn.sigmoid(ab)
    gc_all = _chunk_cumsum(g_all, c)
    gc_t = gc_all.T

    r = lax.broadcasted_iota(jnp.int32, (sub, sub), 0)
    s = lax.broadcasted_iota(jnp.int32, (sub, sub), 1)
    masks = _inverse_masks(c, sub)
    same = masks["same"]
    tri = same & (r >= s)
    strict = same & (r > s)

    for hf in range(tb // sub):
        rows = slice(hf * sub, (hf + 1) * sub)
        kh, qh = k[rows], q[rows]
        kq = _dot_nt(jnp.concatenate([kh, qh], axis=0), kh)
        kk, qk = kq[:sub], kq[sub:]
        for e in range(2):
            lanes = slice(e * GDN_DV, (e + 1) * GDN_DV)
            gcb = jnp.broadcast_to(gc_all[rows, e:e + 1], (sub, LANES))
            betab = jnp.broadcast_to(beta_all[rows, 2 + e:3 + e], (sub, LANES))
            gc_scr[...] = gcb
            gl = gc_scr[pl.ds(c - 1, sub // c, stride=c), :]
            gl_b = jnp.concatenate(
                [jnp.broadcast_to(gl[i:i + 1, :], (c, LANES)) for i in range(sub // c)], axis=0)
            gcc = jnp.concatenate([gcb, gcb], axis=1)
            decay = jnp.where(tri, jnp.exp(jnp.where(tri, gcc - gc_t[e:e + 1, rows], 0.0)), 0.0)
            l_bd = jnp.where(strict, kk * jnp.concatenate([betab, betab], axis=1) * decay, 0.0)
            a_bd = qk * decay
            l_lb = l_bd[0:c]
            for i in range(1, sub // c):
                l_lb = l_lb + l_bd[i * c:(i + 1) * c]
            x_lb = _inverse_minus_eye_lanes(l_lb, l_bd, masks)
            x_bd = jnp.where(same, jnp.concatenate([x_lb] * (sub // c), axis=0), 0.0)
            rhs = jnp.concatenate([v2[rows, lanes] * betab, kh * betab * jnp.exp(gcb)], axis=1)
            sol = rhs + _dot(x_bd, rhs)
            u_ref[rows, lanes] = sol[:, :GDN_DV]
            w_ref[rows, lanes] = sol[:, GDN_DV:].astype(BF16)
            qg_ref[rows, lanes] = (qh * jnp.exp(gcb)).astype(BF16)
            kg_ref[rows, lanes] = (kh * jnp.exp(gl_b - gcb)).astype(BF16)
            half = sub // 2
            at_ref[rows, lanes] = jnp.concatenate([a_bd[:half, :half], a_bd[half:, half:]], axis=0).astype(BF16)
            eg_ref[hf * (sub // c):(hf + 1) * (sub // c), lanes] = jnp.exp(gl)


def _gdn_scan_kernel(u_ref, w_ref, qg_ref, kg_ref, at_ref, eg_ref, z_ref, nw_ref, o_ref, sout_ref, s_scr,
                     *, tb, heads):
    c = GDN_C

    @pl.when(pl.program_id(2) == 0)
    def _():
        s_scr[...] = jnp.zeros_like(s_scr)

    nw = nw_ref[...]
    for h in range(heads):
        lanes = slice(h * GDN_DV, (h + 1) * GDN_DV)
        s = s_scr[h]
        v_prev = None
        for i in range(tb // c):
            rows = slice(i * c, (i + 1) * c)
            ws = jnp.dot(jnp.concatenate([w_ref[rows, lanes], qg_ref[rows, lanes]], axis=0), s.astype(BF16),
                         preferred_element_type=F32)
            v_new = u_ref[rows, lanes] - ws[:c]
            pair = jnp.concatenate([v_new, jnp.zeros_like(v_new)] if i % 2 == 0 else [v_prev, v_new], axis=0)
            o = ws[c:] + jnp.dot(at_ref[rows, lanes], pair.astype(BF16), preferred_element_type=F32)
            s = s * eg_ref[i:i + 1, lanes] + _dot_tn(kg_ref[rows, lanes], v_new)
            v_prev = v_new
            on = o * lax.rsqrt(jnp.mean(o * o, -1, keepdims=True) + EPS) * nw
            o_ref[rows, lanes] = on * _silu(z_ref[rows, lanes])
        s_scr[h] = s

    @pl.when(pl.program_id(2) == pl.num_programs(2) - 1)
    def _():
        sout_ref[...] = s_scr[...]


def _gdn_prompt(proj, conv_w, a_log, dt_bias, norm_w, batch, seq, tb=512, heads=4):
    hk, dk, dv = GDN_K_HEADS, GDN_DK, GDN_DV
    t = batch * seq
    nt = seq // tb
    kq0 = GDN_QK // dk
    v0 = 2 * GDN_QK // (2 * dv)
    ab0 = (GDN_CONV_CH + GDN_V) // LANES
    alog_tab, dt_tab = _gdn_head_tables(a_log, dt_bias)
    tok = lambda b, n, j: b * nt + n
    prev = lambda b, n, j: jnp.maximum((b * seq + n * tb) // SUBLANES - 1, 0)
    u, w, qg, kg, at, eg = pl.pallas_call(
        functools.partial(_gdn_prep_kernel, tb=tb),
        grid=(batch, nt, hk),
        in_specs=[pl.BlockSpec((tb, dk), lambda b, n, j: (tok(b, n, j), j)),
                  pl.BlockSpec((tb, dk), lambda b, n, j: (tok(b, n, j), kq0 + j)),
                  pl.BlockSpec((tb, 2 * dv), lambda b, n, j: (tok(b, n, j), v0 + j)),
                  pl.BlockSpec((tb, LANES), lambda b, n, j: (tok(b, n, j), ab0 + j)),
                  pl.BlockSpec((SUBLANES, dk), lambda b, n, j: (prev(b, n, j), j)),
                  pl.BlockSpec((SUBLANES, dk), lambda b, n, j: (prev(b, n, j), kq0 + j)),
                  pl.BlockSpec((SUBLANES, 2 * dv), lambda b, n, j: (prev(b, n, j), v0 + j)),
                  pl.BlockSpec((CONV_W, dk), lambda b, n, j: (0, j)),
                  pl.BlockSpec((CONV_W, dk), lambda b, n, j: (0, kq0 + j)),
                  pl.BlockSpec((CONV_W, 2 * dv), lambda b, n, j: (0, v0 + j)),
                  pl.BlockSpec((None, SUBLANES, LANES), lambda b, n, j: (j, 0, 0)),
                  pl.BlockSpec((None, SUBLANES, LANES), lambda b, n, j: (j, 0, 0))],
        out_specs=[pl.BlockSpec((tb, 2 * dv), lambda b, n, j: (tok(b, n, j), j))] * 5
        + [pl.BlockSpec((tb // GDN_C, 2 * dv), lambda b, n, j: (tok(b, n, j), j))],
        out_shape=[jax.ShapeDtypeStruct((t, GDN_V), F32)]
        + [jax.ShapeDtypeStruct((t, GDN_V), BF16)] * 4
        + [jax.ShapeDtypeStruct((t // GDN_C, GDN_V), F32)],
        scratch_shapes=[pltpu.VMEM((SUBLANES + tb, dk), F32),
                        pltpu.VMEM((SUBLANES + tb, dk), F32),
                        pltpu.VMEM((SUBLANES + tb, 2 * dv), F32),
                        pltpu.VMEM((GDN_SUB, LANES), F32)],
        compiler_params=_params(("parallel", "parallel", "parallel")),
        name="gdn_prep",
    )(proj, proj, proj, proj, proj, proj, proj, conv_w, conv_w, conv_w, alog_tab, dt_tab)

    wide = heads * dv
    z0 = GDN_CONV_CH // wide
    blk = lambda b, g, n: (b * nt + n, g)
    return pl.pallas_call(
        functools.partial(_gdn_scan_kernel, tb=tb, heads=heads),
        grid=(batch, GDN_V_HEADS // heads, nt),
        in_specs=[pl.BlockSpec((tb, wide), blk)] * 5
        + [pl.BlockSpec((tb // GDN_C, wide), blk),
           pl.BlockSpec((tb, wide), lambda b, g, n: (b * nt + n, z0 + g)),
           pl.BlockSpec((1, dv), lambda b, g, n: (0, 0))],
        out_specs=[pl.BlockSpec((tb, wide), blk),
                   pl.BlockSpec((None, heads, dk, dv), lambda b, g, n: (b, g, 0, 0))],
        out_shape=[jax.ShapeDtypeStruct((t, GDN_V), F32),
                   jax.ShapeDtypeStruct((batch, GDN_V_HEADS, dk, dv), F32)],
        scratch_shapes=[pltpu.VMEM((heads, dk, dv), F32)],
        compiler_params=_params(("parallel", "parallel", "arbitrary")),
        name="gdn_scan",
    )(u, w, qg, kg, at, eg, proj, norm_w.reshape(1, dv))


def _gdn_step_kernel(q_ref, k_ref, v_ref, z_ref, ab_ref, cq_ref, ck_ref, cv_ref, wq_ref, wk_ref, wv_ref,
                     alog_ref, dt_ref, nw_ref, s0_ref, o_ref, sout_ref, bq, bk, bv, g_scr, *, seq):
    hv, hk, dk, dv = GDN_V_HEADS, GDN_K_HEADS, GDN_DK, GDN_DV
    n = hv * seq

    def conv_silu(buf, c_ref, x_ref, cw_ref):
        buf[pl.ds(SUBLANES - 3, 3), :] = c_ref[...]
        buf[pl.ds(SUBLANES, seq), :] = x_ref[...]
        cw = cw_ref[...]
        out = buf[pl.ds(SUBLANES - 3, seq), :] * cw[0:1, :]
        for j in range(1, CONV_W):
            out = out + buf[pl.ds(SUBLANES - 3 + j, seq), :] * cw[j:j + 1, :]
        return _silu(out)

    def stack(x, width, rep):
        return jnp.concatenate([x[:, (h // rep) * width:(h // rep + 1) * width]
                                for h in range(rep * x.shape[1] // width)], axis=0)

    q = stack(conv_silu(bq, cq_ref, q_ref, wq_ref), dk, 1)
    k = stack(conv_silu(bk, ck_ref, k_ref, wk_ref), dk, 1)
    q = q * lax.rsqrt(jnp.sum(q * q, -1, keepdims=True) + EPS) * (dk ** -0.5)
    k = k * lax.rsqrt(jnp.sum(k * k, -1, keepdims=True) + EPS)
    rep = hv // hk
    q = jnp.concatenate([q[(h // rep) * seq:(h // rep + 1) * seq] for h in range(hv)], axis=0)
    k = jnp.concatenate([k[(h // rep) * seq:(h // rep + 1) * seq] for h in range(hv)], axis=0)
    v = stack(conv_silu(bv, cv_ref, v_ref, wv_ref), dv, 1)
    z = stack(z_ref[...], dv, 1)

    ab = stack(ab_ref[...], LANES, 1)
    g_st = -jnp.exp(alog_ref[...]) * jax.nn.softplus(ab + dt_ref[...])
    gc_st = _chunk_cumsum(g_st, seq)
    beta_st = jax.nn.sigmoid(ab)

    def per_head_rows(x_st, lane0):
        cols = [jnp.broadcast_to(x_st[:, lane0 + e:lane0 + e + 1], (hk * seq, LANES)) for e in range(rep)]
        return jnp.concatenate([cols[h % rep][(h // rep) * seq:(h // rep + 1) * seq] for h in range(hv)], axis=0)

    gcb = per_head_rows(gc_st, 0)
    betab = per_head_rows(beta_st, rep)
    gc_row = gcb.T[0:1, :]
    g_scr[...] = gcb
    gl = g_scr[pl.ds(seq - 1, hv, stride=seq), :]
    gl_b = jnp.concatenate([jnp.broadcast_to(gl[h:h + 1, :], (seq, LANES)) for h in range(hv)], axis=0)
    eg = jnp.exp(gl)

    r = lax.broadcasted_iota(jnp.int32, (n, n), 0)
    s = lax.broadcasted_iota(jnp.int32, (n, n), 1)
    same = (r // seq) == (s // seq)
    tri = same & (r >= s)
    strict = same & (r > s)
    wide = lambda x: jnp.concatenate([x] * (n // LANES), axis=1)
    decay = jnp.where(tri, jnp.exp(jnp.where(tri, wide(gcb) - gc_row, 0.0)), 0.0)
    kq = _dot_nt(jnp.concatenate([k, q], axis=0), k)
    l_bd = jnp.where(strict, kq[:n] * wide(betab) * decay, 0.0)
    a_bd = kq[n:] * decay

    x = -l_bd
    p = _dot(l_bd, l_bd)
    m = 2
    while 2 * m < seq:
        xp = _dot(jnp.concatenate([x, p], axis=0), p)
        x = x + p + xp[:n]
        p = xp[n:]
        m *= 2
    x = x + p + _dot(x, p)

    rhs = jnp.concatenate([v * betab, k * betab * jnp.exp(gcb)], axis=1)
    sol = rhs + _dot(x, rhs)
    u, w = sol[:, :dv], sol[:, dv:]
    qg = q * jnp.exp(gcb)
    kg = k * jnp.exp(gl_b - gcb)

    v_new, o_state = [], []
    for h in range(hv):
        rows = slice(h * seq, (h + 1) * seq)
        st = s0_ref[h]
        ws = _dot(jnp.concatenate([w[rows], qg[rows]], axis=0), st)
        vn = u[rows] - ws[:seq]
        sout_ref[h] = st * eg[h:h + 1, :] + _dot_tn(kg[rows], vn)
        v_new.append(vn)
        o_state.append(ws[seq:])
    v_new = jnp.concatenate(v_new, axis=0)
    o = jnp.concatenate(o_state, axis=0) + _dot(a_bd, v_new)
    on = o * lax.rsqrt(jnp.mean(o * o, -1, keepdims=True) + EPS) * nw_ref[...] * _silu(z)
    for h in range(hv):
        o_ref[:, h * dv:(h + 1) * dv] = on[h * seq:(h + 1) * seq]


def _gdn_step(proj, conv_w, a_log, dt_bias, norm_w, s0, buf0, batch, seq):
    hv, hk, dk, dv = GDN_V_HEADS, GDN_K_HEADS, GDN_DK, GDN_DV
    assert seq == SUBLANES
    qw, vw, abw = GDN_QK, GDN_V, hk * LANES
    alog_tab, dt_tab = _gdn_head_tables(a_log, dt_bias)
    alog_tab = alog_tab.reshape(hk * seq, LANES)
    dt_tab = dt_tab.reshape(hk * seq, LANES)
    return pl.pallas_call(
        functools.partial(_gdn_step_kernel, seq=seq),
        grid=(batch,),
        in_specs=[pl.BlockSpec((seq, qw), lambda b: (b, 0)),
                  pl.BlockSpec((seq, qw), lambda b: (b, 1)),
                  pl.BlockSpec((seq, vw), lambda b: (b, 2 * qw // vw)),
                  pl.BlockSpec((seq, vw), lambda b: (b, GDN_CONV_CH // vw)),
                  pl.BlockSpec((seq, abw), lambda b: (b, (GDN_CONV_CH + GDN_V) // abw)),
                  pl.BlockSpec((None, CONV_W - 1, qw), lambda b: (b, 0, 0)),
                  pl.BlockSpec((None, CONV_W - 1, qw), lambda b: (b, 0, 1)),
                  pl.BlockSpec((None, CONV_W - 1, vw), lambda b: (b, 0, 2 * qw // vw)),
                  pl.BlockSpec((CONV_W, qw), lambda b: (0, 0)),
                  pl.BlockSpec((CONV_W, qw), lambda b: (0, 1)),
                  pl.BlockSpec((CONV_W, vw), lambda b: (0, 2 * qw // vw)),
                  pl.BlockSpec((hk * seq, LANES), lambda b: (0, 0)),
                  pl.BlockSpec((hk * seq, LANES), lambda b: (0, 0)),
                  pl.BlockSpec((1, dv), lambda b: (0, 0)),
                  pl.BlockSpec((None, hv, dk, dv), lambda b: (b, 0, 0, 0))],
        out_specs=[pl.BlockSpec((seq, vw), lambda b: (b, 0)),
                   pl.BlockSpec((None, hv, dk, dv), lambda b: (b, 0, 0, 0))],
        out_shape=[jax.ShapeDtypeStruct((batch * seq, GDN_V), F32),
                   jax.ShapeDtypeStruct((batch, hv, dk, dv), F32)],
        scratch_shapes=[pltpu.VMEM((2 * SUBLANES, qw), F32),
                        pltpu.VMEM((2 * SUBLANES, qw), F32),
                        pltpu.VMEM((2 * SUBLANES, vw), F32),
                        pltpu.VMEM((hv * seq, LANES), F32)],
        compiler_params=_params(("parallel",)),
        name="gdn_step",
    )(proj, proj, proj, proj, proj, buf0, buf0, buf0, conv_w, conv_w, conv_w, alog_tab, dt_tab,
      norm_w.reshape(1, dv), s0)


def _trunk(x, mod_fn, per_row, batch, seq, pos0, ret_state, gdn_state, conv_state, wts, tm):
    (norm_pre, norm_post, w_gu, w_down, ret_w_in, ret_w_out, gdn_w_in, gdn_conv_w, gdn_a_log,
     gdn_dt_bias, gdn_norm_w, gdn_w_out) = wts
    kw = dict(per_row=per_row, rows_per_batch=seq)
    depth = norm_pre.shape[0]
    new_ret, new_gdn, new_conv = [], [], []
    for i in range(depth):
        mod = mod_fn(i)

        def ffn(x, sub, f):
            a = _mod_mm(x, mod, sub, norm_pre[i, sub], w_gu, (i, f), glu=True, out_dtype=BF16,
                        tm=tm, tn=512, **kw)
            return _mm_out(a, w_down, (i, f), x, mod, sub, norm_post[i, sub], res_scale=FFN_RES,
                           tm=min(tm, 512), tk=512, **kw)

        x = ffn(x, 0, 0)
        r = i // 2
        if i % 2 == 0:
            proj = _mod_mm(x, mod, 1, norm_pre[i, 1], ret_w_in, (r,), glu=False, out_dtype=F32,
                           tm=tm, tn=512, **kw)
            y, s = _retention(proj, None if ret_state is None else ret_state[r], pos0, batch, seq,
                              c=math.gcd(256, seq))
            new_ret.append(s)
            x = _mm_out(y, ret_w_out, (r,), x, mod, 1, norm_post[i, 1], res_scale=1.0,
                        tm=min(tm, 512), tk=512, **kw)
        else:
            proj = _mod_mm(x, mod, 1, norm_pre[i, 1], gdn_w_in, (r,), glu=False, out_dtype=F32,
                           tm=tm, tn=512, **kw)
            if gdn_state is None:
                y, s = _gdn_prompt(proj, gdn_conv_w[r], gdn_a_log[r], gdn_dt_bias[r], gdn_norm_w[r],
                                   batch, seq)
            else:
                y, s = _gdn_step(proj, gdn_conv_w[r], gdn_a_log[r], gdn_dt_bias[r], gdn_norm_w[r],
                                 gdn_state[r], conv_state[r], batch, seq)
            new_gdn.append(s)
            new_conv.append(proj.reshape(batch, seq, -1)[:, seq - (CONV_W - 1):, :GDN_CONV_CH])
            x = _mm_out(y, gdn_w_out, (r,), x, mod, 1, norm_post[i, 1], res_scale=1.0,
                        tm=min(tm, 512), tk=512, **kw)
        x = ffn(x, 2, 1)
    stack = lambda xs: xs[0][None] if len(xs) == 1 else jnp.stack(xs)
    return x, stack(new_ret), stack(new_gdn), stack(new_conv)


def kernel(x_prompt, x_sample, c_prompt, c_sample, state_ret, state_gdn, state_conv, w_ada, b_ada,
           norm_pre, norm_post, ffn_w_gu, ffn_w_down, ret_w_in, ret_w_out, gdn_w_in, gdn_conv_w,
           gdn_a_log, gdn_dt_bias, gdn_norm_w, gdn_w_out):
    bp, lp, d = x_prompt.shape
    bs, ls, _ = x_sample.shape
    wts = (norm_pre, norm_post, ffn_w_gu.astype(BF16), ffn_w_down.astype(BF16),
           ret_w_in.astype(BF16), ret_w_out.astype(BF16),
           jax.vmap(_gdn_in_weight)(gdn_w_in).astype(BF16),
           gdn_conv_w, gdn_a_log, gdn_dt_bias, gdn_norm_w, gdn_w_out.astype(BF16))

    cs_rows = jnp.repeat(c_sample, ls, axis=0)
    cp_rows = jnp.pad(c_prompt, ((0, SUBLANES - bp), (0, 0)))
    mods = [_ada(cs_rows, cp_rows, w_ada, b_ada, i) for i in range(w_ada.shape[0])]

    y_p, ret_p, gdn_p, conv_p = _trunk(
        x_prompt.reshape(bp * lp, d), lambda i: mods[i][1], False, bp, lp, 0,
        None, None, None, wts, tm=1024)
    y_s, ret_s, gdn_s, conv_s = _trunk(
        x_sample.reshape(bs * ls, d), lambda i: mods[i][0], True, bs, ls, PAST_LEN,
        state_ret, state_gdn, state_conv, wts, tm=256)
    return (y_p.reshape(bp, lp, d), y_s.reshape(bs, ls, d), ret_p, ret_s, gdn_p, gdn_s, conv_p, conv_s)
```

```python
import functools
import math

import jax
import jax.numpy as jnp
from jax import lax
from jax.experimental import pallas as pl
from jax.experimental.pallas import tpu as pltpu

F32 = jnp.float32
BF16 = jnp.bfloat16
HIGHEST = lax.Precision.HIGHEST

EPS = 1e-6
ROPE_BASE = 10000.0
FFN_RES = 0.5
N_SUB = 3
CONV_W = 4
PAST_LEN = 16384

RET_HEADS = 8
RET_DK = 256
RET_DV = 512
GDN_K_HEADS = 16
GDN_V_HEADS = 32
GDN_DK = 128
GDN_DV = 128
GDN_QK = GDN_K_HEADS * GDN_DK
GDN_V = GDN_V_HEADS * GDN_DV
GDN_CONV_CH = 2 * GDN_QK + GDN_V

LANES = 128
SUBLANES = 8
VMEM_LIMIT = 56 * 1024 * 1024


def _params(sem):
    return pltpu.CompilerParams(dimension_semantics=sem, vmem_limit_bytes=VMEM_LIMIT)


def _silu(x):
    return x * jax.nn.sigmoid(x)


def _dot(a, b):
    return jnp.dot(a.astype(BF16), b.astype(BF16), preferred_element_type=F32)


def _dot_nt(a, b):
    return lax.dot_general(a.astype(BF16), b.astype(BF16), (((1,), (1,)), ((), ())),
                           preferred_element_type=F32)


def _dot_tn(a, b):
    return lax.dot_general(a.astype(BF16), b.astype(BF16), (((0,), (0,)), ((), ())),
                           preferred_element_type=F32)


def _ada_kernel(cs_ref, cp_ref, w_ref, b_ref, os_ref, op_ref):
    w = w_ref[...].astype(BF16)
    b = b_ref[...]
    os_ref[...] = jnp.dot(_silu(cs_ref[...]).astype(BF16), w, preferred_element_type=F32) + b
    op_ref[...] = jnp.dot(_silu(cp_ref[...]).astype(BF16), w, preferred_element_type=F32) + b


def _ada(cs, cp, w, b, layer, tn=512):
    ms, d = cs.shape
    mp = cp.shape[0]
    depth, _, n = w.shape
    return pl.pallas_call(
        _ada_kernel,
        grid=(n // tn,),
        in_specs=[pl.BlockSpec((ms, d), lambda j: (0, 0)),
                  pl.BlockSpec((mp, d), lambda j: (0, 0)),
                  pl.BlockSpec((None, d, tn), lambda j: (layer, 0, j)),
                  pl.BlockSpec((None, 1, tn), lambda j: (layer, 0, j))],
        out_specs=[pl.BlockSpec((ms, tn), lambda j: (0, j)),
                   pl.BlockSpec((mp, tn), lambda j: (0, j))],
        out_shape=[jax.ShapeDtypeStruct((ms, n), F32), jax.ShapeDtypeStruct((mp, n), F32)],
        compiler_params=_params(("arbitrary",)),
        name="ada_table",
    )(cs, cp, w, b.reshape(depth, 1, n))


def _mod_rows(ref, per_row, tiles_per_batch):
    if per_row:
        return ref[...]
    return ref[pl.ds(pl.program_id(0) // tiles_per_batch, 1), :]


def _mod_mm_kernel(x_ref, sh_ref, sc_ref, nw_ref, *rest, glu, per_row, tiles_per_batch):
    if glu:
        wg_ref, wu_ref, o_ref, h_scr = rest
    else:
        w_ref, o_ref, h_scr = rest

    @pl.when(pl.program_id(1) == 0)
    def _():
        x = x_ref[...]
        y = x * lax.rsqrt(jnp.mean(x * x, -1, keepdims=True) + EPS) * nw_ref[...]
        sc = _mod_rows(sc_ref, per_row, tiles_per_batch)
        sh = _mod_rows(sh_ref, per_row, tiles_per_batch)
        h_scr[...] = (y * (1.0 + sc) + sh).astype(BF16)

    h = h_scr[...]
    if glu:
        g = jnp.dot(h, wg_ref[...], preferred_element_type=F32)
        u = jnp.dot(h, wu_ref[...], preferred_element_type=F32)
        o_ref[...] = (_silu(g) * u).astype(o_ref.dtype)
    else:
        o_ref[...] = jnp.dot(h, w_ref[...], preferred_element_type=F32).astype(o_ref.dtype)


def _mod_mm(x, mod, sub, nw, w, widx, *, glu, per_row, rows_per_batch, out_dtype, tm, tn):
    t, d = x.shape
    n = w.shape[-1] // 2 if glu else w.shape[-1]
    nj = n // tn
    lead = (None,) * len(widx)
    tiles_per_batch = max(rows_per_batch // tm, 1)
    mrows = tm if per_row else mod.shape[0]

    def mod_spec(c):
        col = sub * 3 + c
        if per_row:
            return pl.BlockSpec((mrows, d), lambda i, j: (i, col))
        return pl.BlockSpec((mrows, d), lambda i, j: (0, col))

    in_specs = [pl.BlockSpec((tm, d), lambda i, j: (i, 0)),
                mod_spec(0), mod_spec(1),
                pl.BlockSpec((1, d), lambda i, j: (0, 0))]
    args = [x, mod, mod, nw.reshape(1, d)]
    if glu:
        in_specs += [pl.BlockSpec(lead + (d, tn), lambda i, j: widx + (0, j)),
                     pl.BlockSpec(lead + (d, tn), lambda i, j: widx + (0, j + nj))]
        args += [w, w]
    else:
        in_specs += [pl.BlockSpec(lead + (d, tn), lambda i, j: widx + (0, j))]
        args += [w]
    return pl.pallas_call(
        functools.partial(_mod_mm_kernel, glu=glu, per_row=per_row, tiles_per_batch=tiles_per_batch),
        grid=(t // tm, nj),
        in_specs=in_specs,
        out_specs=pl.BlockSpec((tm, tn), lambda i, j: (i, j)),
        out_shape=jax.ShapeDtypeStruct((t, n), out_dtype),
        scratch_shapes=[pltpu.VMEM((tm, d), BF16)],
        compiler_params=_params(("parallel", "arbitrary")),
        name="mod_mm_glu" if glu else "mod_mm",
    )(*args)


def _mm_out_kernel(a_ref, w_ref, x_ref, g_ref, nw_ref, o_ref, *, res_scale, per_row, tiles_per_batch):
    y = jnp.dot(a_ref[...].astype(BF16), w_ref[...], preferred_element_type=F32)
    yn = y * lax.rsqrt(jnp.mean(y * y, -1, keepdims=True) + EPS) * nw_ref[...]
    gate = _mod_rows(g_ref, per_row, tiles_per_batch)
    o_ref[...] = x_ref[...] + res_scale * (gate * yn)


def _mm_out(a, w, widx, x, mod, sub, nw, *, res_scale, per_row, rows_per_batch, tm):
    t, kdim = a.shape
    d = w.shape[-1]
    lead = (None,) * len(widx)
    tiles_per_batch = max(rows_per_batch // tm, 1)
    col = sub * 3 + 2
    if per_row:
        g_spec = pl.BlockSpec((tm, d), lambda i: (i, col))
    else:
        g_spec = pl.BlockSpec((mod.shape[0], d), lambda i: (0, col))
    return pl.pallas_call(
        functools.partial(_mm_out_kernel, res_scale=res_scale, per_row=per_row,
                          tiles_per_batch=tiles_per_batch),
        grid=(t // tm,),
        in_specs=[pl.BlockSpec((tm, kdim), lambda i: (i, 0)),
                  pl.BlockSpec(lead + (kdim, d), lambda i: widx + (0, 0), pipeline_mode=pl.Buffered(1)),
                  pl.BlockSpec((tm, d), lambda i: (i, 0)),
                  g_spec,
                  pl.BlockSpec((1, d), lambda i: (0, 0))],
        out_specs=pl.BlockSpec((tm, d), lambda i: (i, 0)),
        out_shape=jax.ShapeDtypeStruct((t, d), F32),
        compiler_params=_params(("parallel",)),
        name="mm_out",
    )(a, w, x, mod, nw.reshape(1, d))


def _rotate(x, cos, sin):
    half = x.shape[-1] // 2
    x1, x2 = x[:, :half], x[:, half:]
    return jnp.concatenate([x1 * cos - x2 * sin, x1 * sin + x2 * cos], axis=-1)


def _ret_kernel(dm_ref, qd_ref, kd_ref, cd_ref, cos_ref, sin_ref, q_ref, k_ref, v_ref, g_ref, *rest,
                heads, has_state):
    if has_state:
        s0_ref, o_ref, sout_ref, s_scr = rest
    else:
        o_ref, sout_ref, s_scr = rest
    n = pl.program_id(2)
    dk, dv = RET_DK, RET_DV

    @pl.when(n == 0)
    def _():
        if has_state:
            s_scr[...] = s0_ref[...]
        else:
            s_scr[...] = jnp.zeros_like(s_scr)

    cos, sin = cos_ref[...], sin_ref[...]
    hs = range(heads)
    q = [_rotate(q_ref[:, h * dk:(h + 1) * dk], cos, sin) * (dk ** -0.5) for h in hs]
    k = [_rotate(k_ref[:, h * dk:(h + 1) * dk], cos, sin) for h in hs]
    v = [v_ref[:, h * dv:(h + 1) * dv].astype(BF16) for h in hs]
    s = [s_scr[h] for h in hs]
    scores = [_dot_nt(q[h], k[h]) * dm_ref[h] for h in hs]
    cross = [_dot(q[h], s[h]) for h in hs]
    inner = [_dot(scores[h], v[h]) for h in hs]
    upd = [_dot_tn(k[h] * jnp.concatenate([kd_ref[h]] * (dk // LANES), axis=1), v[h]) for h in hs]
    for h in hs:
        s_new = s[h] * cd_ref[h, 0:1, 0:1] + upd[h]
        s_scr[h] = s_new
        o = inner[h] + cross[h] * jnp.concatenate([qd_ref[h]] * (dv // LANES), axis=1)
        on = o * lax.rsqrt(jnp.mean(o * o, -1, keepdims=True) + EPS)
        o_ref[:, h * dv:(h + 1) * dv] = _silu(g_ref[:, h * dv:(h + 1) * dv]) * on

    @pl.when(n == pl.num_programs(2) - 1)
    def _():
        sout_ref[...] = s_scr[...]


def _retention(proj, s0, pos0, batch, seq, c, heads):
    nh, dk, dv = RET_HEADS, RET_DK, RET_DV
    nc = seq // c
    ng = nh // heads
    half = dk // 2
    log_g = jnp.log1p(-jnp.exp2(-5.0 - jnp.arange(nh, dtype=F32)))
    idx = jnp.arange(c, dtype=F32)
    diff = idx[:, None] - idx[None, :]
    causal = diff >= 0
    dmask = jnp.where(causal[None], jnp.exp(log_g[:, None, None] * jnp.where(causal, diff, 0.0)[None]), 0.0)
    lanes = lambda x: jnp.broadcast_to(x[:, :, None], (nh, x.shape[1], LANES))
    q_decay = lanes(jnp.exp(log_g[:, None] * (idx[None, :] + 1.0)))
    k_decay = lanes(jnp.exp(log_g[:, None] * (c - 1.0 - idx)[None, :]))
    chunk_decay = jnp.broadcast_to(jnp.exp(log_g * c)[:, None, None], (nh, SUBLANES, LANES))
    inv = 1.0 / (ROPE_BASE ** jnp.linspace(0.0, 1.0, half, dtype=F32))
    ang = (pos0 + jnp.arange(seq)).astype(F32)[:, None] * inv[None, :]
    cos, sin = jnp.cos(ang), jnp.sin(ang)
    has_state = s0 is not None
    tokb = lambda b, g, n: b * nc + n
    in_specs = [pl.BlockSpec((heads, c, c), lambda b, g, n: (g, 0, 0)),
                pl.BlockSpec((heads, c, LANES), lambda b, g, n: (g, 0, 0)),
                pl.BlockSpec((heads, c, LANES), lambda b, g, n: (g, 0, 0)),
                pl.BlockSpec((heads, SUBLANES, LANES), lambda b, g, n: (g, 0, 0)),
                pl.BlockSpec((c, half), lambda b, g, n: (n, 0)),
                pl.BlockSpec((c, half), lambda b, g, n: (n, 0)),
                pl.BlockSpec((c, heads * dk), lambda b, g, n: (tokb(b, g, n), g)),
                pl.BlockSpec((c, heads * dk), lambda b, g, n: (tokb(b, g, n), ng + g)),
                pl.BlockSpec((c, heads * dv), lambda b, g, n: (tokb(b, g, n), ng + g)),
                pl.BlockSpec((c, heads * dv), lambda b, g, n: (tokb(b, g, n), 2 * ng + g))]
    args = [dmask, q_decay, k_decay, chunk_decay, cos, sin, proj, proj, proj, proj]
    if has_state:
        in_specs.append(pl.BlockSpec((None, heads, dk, dv), lambda b, g, n: (b, g, 0, 0)))
        args.append(s0)
    return pl.pallas_call(
        functools.partial(_ret_kernel, heads=heads, has_state=has_state),
        grid=(batch, ng, nc),
        in_specs=in_specs,
        out_specs=[pl.BlockSpec((c, heads * dv), lambda b, g, n: (tokb(b, g, n), g)),
                   pl.BlockSpec((None, heads, dk, dv), lambda b, g, n: (b, g, 0, 0))],
        out_shape=[jax.ShapeDtypeStruct((batch * seq, nh * dv), F32),
                   jax.ShapeDtypeStruct((batch, nh, dk, dv), F32)],
        scratch_shapes=[pltpu.VMEM((heads, dk, dv), F32)],
        compiler_params=_params(("parallel", "parallel", "arbitrary")),
        name="retention",
    )(*args)


def _gdn_in_weight(w_in):
    d = w_in.shape[0]
    main = w_in[:, :GDN_CONV_CH + GDN_V]
    a = w_in[:, GDN_CONV_CH + GDN_V:GDN_CONV_CH + GDN_V + GDN_V_HEADS].reshape(d, GDN_K_HEADS, 2)
    b = w_in[:, GDN_CONV_CH + GDN_V + GDN_V_HEADS:].reshape(d, GDN_K_HEADS, 2)
    tail = jnp.concatenate([a, b, jnp.zeros((d, GDN_K_HEADS, LANES - 4), w_in.dtype)], axis=-1)
    return jnp.concatenate([main, tail.reshape(d, GDN_K_HEADS * LANES)], axis=-1)


def _gdn_head_tables(a_log, dt_bias):
    def tab(x):
        row = jnp.concatenate([x.reshape(GDN_K_HEADS, 2), jnp.zeros((GDN_K_HEADS, LANES - 2), F32)], -1)
        return jnp.broadcast_to(row[:, None, :], (GDN_K_HEADS, SUBLANES, LANES))
    return tab(a_log), tab(dt_bias)


GDN_C = 64
GDN_SUB = 256


def _chunk_cumsum(x, c):
    pos = lax.broadcasted_iota(jnp.int32, x.shape, 0) % c
    s = 1
    while s < c:
        x = x + jnp.where(pos >= s, pltpu.roll(x, s, axis=0), 0.0)
        s *= 2
    return x


GDN_BASE = 8


def _inverse_masks(c, width):
    i = lax.broadcasted_iota(jnp.int32, (c, width), 0)
    j = lax.broadcasted_iota(jnp.int32, (c, width), 1) % c
    r = lax.broadcasted_iota(jnp.int32, (width, width), 0)
    s = lax.broadcasted_iota(jnp.int32, (width, width), 1)
    masks = {"same": (r // c) == (s // c),
             "base_lanes": (i // GDN_BASE) == (j // GDN_BASE),
             "base_diag": (r // GDN_BASE) == (s // GDN_BASE)}
    b = GDN_BASE
    while b < c:
        masks[f"lanes{b}"] = ((i // (2 * b)) == (j // (2 * b))) & ((i // b) % 2 == 1) & ((j // b) % 2 == 0)
        masks[f"diag{b}"] = ((r // (2 * b)) == (s // (2 * b))) & ((r // b) % 2 == 1) & ((s // b) % 2 == 0)
        b *= 2
    return masks


def _inverse_minus_eye_lanes(l_lbs, l_bds, masks):
    c, width = l_lbs[0].shape
    reps = width // c
    ps = range(len(l_lbs))

    def to_diag(p):
        return jnp.where(masks["same"], jnp.concatenate([p] * reps, axis=0), 0.0)

    lb = [jnp.where(masks["base_lanes"], l_lbs[i], 0.0) for i in ps]
    x = [-lb[i] for i in ps]
    p = [_dot(lb[i], jnp.where(masks["base_diag"], l_bds[i], 0.0)) for i in ps]
    n = 2
    while 2 * n < GDN_BASE:
        xp = [_dot(jnp.concatenate([x[i], p[i]], axis=0), to_diag(p[i])) for i in ps]
        x = [x[i] + p[i] + xp[i][:c] for i in ps]
        p = [xp[i][c:] for i in ps]
        n *= 2
    xp = [_dot(x[i], to_diag(p[i])) for i in ps]
    x = [x[i] + p[i] + xp[i] for i in ps]

    b = GDN_BASE
    while b < c:
        t = [jnp.where(masks[f"lanes{b}"], l_lbs[i], 0.0)
             + _dot(x[i], jnp.where(masks[f"diag{b}"], l_bds[i], 0.0)) for i in ps]
        tx = [_dot(t[i], to_diag(x[i])) for i in ps]
        x = [x[i] - t[i] - tx[i] for i in ps]
        b *= 2
    return x


def _gdn_prep_kernel(q_ref, k_ref, v_ref, ab_ref, qp_ref, kp_ref, vp_ref, wq_ref, wk_ref, wv_ref,
                     alog_ref, dt_ref, u_ref, w_ref, qg_ref, kgt_ref, at_ref, eg_ref,
                     bq, bk, bv, gc_scr, *, tb):
    c, sub = GDN_C, GDN_SUB
    first = pl.program_id(1) == 0

    def conv_silu(buf, prev_ref, x_ref, cw_ref):
        buf[pl.ds(0, SUBLANES), :] = jnp.where(first, 0.0, prev_ref[...])
        buf[pl.ds(SUBLANES, tb), :] = x_ref[...]
        cw = cw_ref[...]
        out = buf[pl.ds(SUBLANES - 3, tb), :] * cw[0:1, :]
        for j in range(1, CONV_W):
            out = out + buf[pl.ds(SUBLANES - 3 + j, tb), :] * cw[j:j + 1, :]
        return _silu(out)

    q = conv_silu(bq, qp_ref, q_ref, wq_ref)
    k = conv_silu(bk, kp_ref, k_ref, wk_ref)
    v2 = conv_silu(bv, vp_ref, v_ref, wv_ref)
    q = q * lax.rsqrt(jnp.sum(q * q, -1, keepdims=True) + EPS) * (GDN_DK ** -0.5)
    k = k * lax.rsqrt(jnp.sum(k * k, -1, keepdims=True) + EPS)

    ab = ab_ref[...]
    g_all = -jnp.exp(alog_ref[0:1, :]) * jax.nn.softplus(ab + dt_ref[0:1, :])
    beta_all = jax.nn.sigmoid(ab)
    gc_all = _chunk_cumsum(g_all, c)
    gc_t = gc_all.T

    r = lax.broadcasted_iota(jnp.int32, (sub, sub), 0)
    s = lax.broadcasted_iota(jnp.int32, (sub, sub), 1)
    masks = _inverse_masks(c, sub)
    same = masks["same"]
    tri = same & (r >= s)
    strict = same & (r > s)

    nh, nchunk, half = tb // sub, sub // c, sub // 2
    rows = [slice(hf * sub, (hf + 1) * sub) for hf in range(nh)]
    kq = [_dot_nt(jnp.concatenate([k[rows[hf]], q[rows[hf]]], axis=0), k[rows[hf]]) for hf in range(nh)]
    probs = [(hf, e) for hf in range(nh) for e in range(2)]
    gcb, betab, gl, l_bd, a_bd, l_lb = [], [], [], [], [], []
    for p, (hf, e) in enumerate(probs):
        gcb.append(jnp.broadcast_to(gc_all[rows[hf], e:e + 1], (sub, LANES)))
        betab.append(jnp.broadcast_to(beta_all[rows[hf], 2 + e:3 + e], (sub, LANES)))
        gc_scr[p] = gcb[p]
        gl.append(gc_scr[p, pl.ds(c - 1, nchunk, stride=c), :])
        gcc = jnp.concatenate([gcb[p], gcb[p]], axis=1)
        decay = jnp.where(tri, jnp.exp(jnp.where(tri, gcc - gc_t[e:e + 1, rows[hf]], 0.0)), 0.0)
        l_bd.append(jnp.where(strict, kq[hf][:sub] * jnp.concatenate([betab[p], betab[p]], axis=1) * decay, 0.0))
        a_bd.append(kq[hf][sub:] * decay)
        acc = l_bd[p][0:c]
        for i in range(1, nchunk):
            acc = acc + l_bd[p][i * c:(i + 1) * c]
        l_lb.append(acc)
    x_lb = _inverse_minus_eye_lanes(l_lb, l_bd, masks)
    rhs = [jnp.concatenate([v2[rows[hf], e * GDN_DV:(e + 1) * GDN_DV] * betab[p],
                            k[rows[hf]] * betab[p] * jnp.exp(gcb[p])], axis=1)
           for p, (hf, e) in enumerate(probs)]
    corr = [_dot(jnp.where(same, jnp.concatenate([x_lb[p]] * nchunk, axis=0), 0.0), rhs[p])
            for p in range(len(probs))]
    for p, (hf, e) in enumerate(probs):
        lanes = slice(e * GDN_DV, (e + 1) * GDN_DV)
        sol = rhs[p] + corr[p]
        u_ref[rows[hf], lanes] = sol[:, :GDN_DV]
        w_ref[rows[hf], lanes] = sol[:, GDN_DV:].astype(BF16)
        qg_ref[rows[hf], lanes] = (q[rows[hf]] * jnp.exp(gcb[p])).astype(BF16)
        gl_b = jnp.concatenate([jnp.broadcast_to(gl[p][i:i + 1, :], (c, LANES)) for i in range(nchunk)], axis=0)
        kg = k[rows[hf]] * jnp.exp(gl_b - gcb[p])
        kgt_ref[rows[hf], lanes] = jnp.concatenate([kg[:half].T, kg[half:].T], axis=0).astype(BF16)
        at_ref[rows[hf], lanes] = jnp.concatenate(
            [a_bd[p][:half, :half], a_bd[p][half:, half:]], axis=0).astype(BF16)
        eg_ref[hf * nchunk:(hf + 1) * nchunk, lanes] = jnp.exp(gl[p])


def _gdn_scan_kernel(u_ref, w_ref, qg_ref, kgt_ref, at_ref, eg_ref, z_ref, nw_ref, o_ref, sout_ref, s_scr,
                     *, tb, heads):
    c = GDN_C

    @pl.when(pl.program_id(2) == 0)
    def _():
        s_scr[...] = jnp.zeros_like(s_scr)

    nw = nw_ref[...]
    hl = [slice(h * GDN_DV, (h + 1) * GDN_DV) for h in range(heads)]
    s = [s_scr[h] for h in range(heads)]
    zeros = jnp.zeros((c, GDN_DV), BF16)
    for i in range(tb // c):
        rows = slice(i * c, (i + 1) * c)
        pair_rows = slice((i // 2) * 2 * c, (i // 2 + 1) * 2 * c)
        ws = [jnp.dot(jnp.concatenate([w_ref[rows, hl[h]], qg_ref[rows, hl[h]]], axis=0),
                      s[h].astype(BF16), preferred_element_type=F32) for h in range(heads)]
        v_new = [(u_ref[rows, hl[h]] - ws[h][:c]).astype(BF16) for h in range(heads)]
        pair = [jnp.concatenate([v_new[h], zeros] if i % 2 == 0 else [zeros, v_new[h]], axis=0)
                for h in range(heads)]
        for h in range(heads):
            o = ws[h][c:] + jnp.dot(at_ref[rows, hl[h]], pair[h], preferred_element_type=F32)
            s[h] = s[h] * eg_ref[i:i + 1, hl[h]] + jnp.dot(kgt_ref[pair_rows, hl[h]], pair[h],
                                                          preferred_element_type=F32)
            on = o * lax.rsqrt(jnp.mean(o * o, -1, keepdims=True) + EPS) * nw
            o_ref[rows, hl[h]] = on * _silu(z_ref[rows, hl[h]])
    for h in range(heads):
        s_scr[h] = s[h]

    @pl.when(pl.program_id(2) == pl.num_programs(2) - 1)
    def _():
        sout_ref[...] = s_scr[...]


def _gdn_prompt(proj, conv_w, a_log, dt_bias, norm_w, batch, seq, tb=512, heads=8):
    hk, dk, dv = GDN_K_HEADS, GDN_DK, GDN_DV
    t = batch * seq
    nt = seq // tb
    kq0 = GDN_QK // dk
    v0 = 2 * GDN_QK // (2 * dv)
    ab0 = (GDN_CONV_CH + GDN_V) // LANES
    alog_tab, dt_tab = _gdn_head_tables(a_log, dt_bias)
    tok = lambda b, n, j: b * nt + n
    prev = lambda b, n, j: jnp.maximum((b * seq + n * tb) // SUBLANES - 1, 0)
    u, w, qg, kg, at, eg = pl.pallas_call(
        functools.partial(_gdn_prep_kernel, tb=tb),
        grid=(batch, nt, hk),
        in_specs=[pl.BlockSpec((tb, dk), lambda b, n, j: (tok(b, n, j), j)),
                  pl.BlockSpec((tb, dk), lambda b, n, j: (tok(b, n, j), kq0 + j)),
                  pl.BlockSpec((tb, 2 * dv), lambda b, n, j: (tok(b, n, j), v0 + j)),
                  pl.BlockSpec((tb, LANES), lambda b, n, j: (tok(b, n, j), ab0 + j)),
                  pl.BlockSpec((SUBLANES, dk), lambda b, n, j: (prev(b, n, j), j)),
                  pl.BlockSpec((SUBLANES, dk), lambda b, n, j: (prev(b, n, j), kq0 + j)),
                  pl.BlockSpec((SUBLANES, 2 * dv), lambda b, n, j: (prev(b, n, j), v0 + j)),
                  pl.BlockSpec((CONV_W, dk), lambda b, n, j: (0, j)),
                  pl.BlockSpec((CONV_W, dk), lambda b, n, j: (0, kq0 + j)),
                  pl.BlockSpec((CONV_W, 2 * dv), lambda b, n, j: (0, v0 + j)),
                  pl.BlockSpec((None, SUBLANES, LANES), lambda b, n, j: (j, 0, 0)),
                  pl.BlockSpec((None, SUBLANES, LANES), lambda b, n, j: (j, 0, 0))],
        out_specs=[pl.BlockSpec((tb, 2 * dv), lambda b, n, j: (tok(b, n, j), j))] * 5
        + [pl.BlockSpec((tb // GDN_C, 2 * dv), lambda b, n, j: (tok(b, n, j), j))],
        out_shape=[jax.ShapeDtypeStruct((t, GDN_V), F32)]
        + [jax.ShapeDtypeStruct((t, GDN_V), BF16)] * 4
        + [jax.ShapeDtypeStruct((t // GDN_C, GDN_V), F32)],
        scratch_shapes=[pltpu.VMEM((SUBLANES + tb, dk), F32),
                        pltpu.VMEM((SUBLANES + tb, dk), F32),
                        pltpu.VMEM((SUBLANES + tb, 2 * dv), F32),
                        pltpu.VMEM((2 * tb // GDN_SUB, GDN_SUB, LANES), F32)],
        compiler_params=_params(("parallel", "parallel", "parallel")),
        name="gdn_prep",
    )(proj, proj, proj, proj, proj, proj, proj, conv_w, conv_w, conv_w, alog_tab, dt_tab)

    wide = heads * dv
    z0 = GDN_CONV_CH // wide
    blk = lambda b, g, n: (b * nt + n, g)
    return pl.pallas_call(
        functools.partial(_gdn_scan_kernel, tb=tb, heads=heads),
        grid=(batch, GDN_V_HEADS // heads, nt),
        in_specs=[pl.BlockSpec((tb, wide), blk)] * 5
        + [pl.BlockSpec((tb // GDN_C, wide), blk),
           pl.BlockSpec((tb, wide), lambda b, g, n: (b * nt + n, z0 + g)),
           pl.BlockSpec((1, dv), lambda b, g, n: (0, 0))],
        out_specs=[pl.BlockSpec((tb, wide), blk),
                   pl.BlockSpec((None, heads, dk, dv), lambda b, g, n: (b, g, 0, 0))],
        out_shape=[jax.ShapeDtypeStruct((t, GDN_V), F32),
                   jax.ShapeDtypeStruct((batch, GDN_V_HEADS, dk, dv), F32)],
        scratch_shapes=[pltpu.VMEM((heads, dk, dv), F32)],
        compiler_params=_params(("parallel", "parallel", "arbitrary")),
        name="gdn_scan",
    )(u, w, qg, kg, at, eg, proj, norm_w.reshape(1, dv))


def _gdn_step_kernel(q_ref, k_ref, v_ref, z_ref, ab_ref, cq_ref, ck_ref, cv_ref, wq_ref, wk_ref, wv_ref,
                     alog_ref, dt_ref, nw_ref, s0_ref, o_ref, sout_ref, bq, bk, bv, g_scr, *, seq):
    hv, hk, dk, dv = GDN_V_HEADS, GDN_K_HEADS, GDN_DK, GDN_DV
    n = hv * seq

    def conv_silu(buf, c_ref, x_ref, cw_ref):
        buf[pl.ds(SUBLANES - 3, 3), :] = c_ref[...]
        buf[pl.ds(SUBLANES, seq), :] = x_ref[...]
        cw = cw_ref[...]
        out = buf[pl.ds(SUBLANES - 3, seq), :] * cw[0:1, :]
        for j in range(1, CONV_W):
            out = out + buf[pl.ds(SUBLANES - 3 + j, seq), :] * cw[j:j + 1, :]
        return _silu(out)

    def stack(x, width, rep):
        return jnp.concatenate([x[:, (h // rep) * width:(h // rep + 1) * width]
                                for h in range(rep * x.shape[1] // width)], axis=0)

    q = stack(conv_silu(bq, cq_ref, q_ref, wq_ref), dk, 1)
    k = stack(conv_silu(bk, ck_ref, k_ref, wk_ref), dk, 1)
    q = q * lax.rsqrt(jnp.sum(q * q, -1, keepdims=True) + EPS) * (dk ** -0.5)
    k = k * lax.rsqrt(jnp.sum(k * k, -1, keepdims=True) + EPS)
    rep = hv // hk
    q = jnp.concatenate([q[(h // rep) * seq:(h // rep + 1) * seq] for h in range(hv)], axis=0)
    k = jnp.concatenate([k[(h // rep) * seq:(h // rep + 1) * seq] for h in range(hv)], axis=0)
    v = stack(conv_silu(bv, cv_ref, v_ref, wv_ref), dv, 1)
    z = stack(z_ref[...], dv, 1)

    ab = stack(ab_ref[...], LANES, 1)
    g_st = -jnp.exp(alog_ref[...]) * jax.nn.softplus(ab + dt_ref[...])
    gc_st = _chunk_cumsum(g_st, seq)
    beta_st = jax.nn.sigmoid(ab)

    def per_head_rows(x_st, lane0):
        cols = [jnp.broadcast_to(x_st[:, lane0 + e:lane0 + e + 1], (hk * seq, LANES)) for e in range(rep)]
        return jnp.concatenate([cols[h % rep][(h // rep) * seq:(h // rep + 1) * seq] for h in range(hv)], axis=0)

    gcb = per_head_rows(gc_st, 0)
    betab = per_head_rows(beta_st, rep)
    gc_row = gcb.T[0:1, :]
    g_scr[...] = gcb
    gl = g_scr[pl.ds(seq - 1, hv, stride=seq), :]
    gl_b = jnp.concatenate([jnp.broadcast_to(gl[h:h + 1, :], (seq, LANES)) for h in range(hv)], axis=0)
    eg = jnp.exp(gl)

    r = lax.broadcasted_iota(jnp.int32, (n, n), 0)
    s = lax.broadcasted_iota(jnp.int32, (n, n), 1)
    same = (r // seq) == (s // seq)
    tri = same & (r >= s)
    strict = same & (r > s)
    wide = lambda x: jnp.concatenate([x] * (n // LANES), axis=1)
    decay = jnp.where(tri, jnp.exp(jnp.where(tri, wide(gcb) - gc_row, 0.0)), 0.0)
    kq = _dot_nt(jnp.concatenate([k, q], axis=0), k)
    l_bd = jnp.where(strict, kq[:n] * wide(betab) * decay, 0.0)
    a_bd = kq[n:] * decay

    x = -l_bd
    p = _dot(l_bd, l_bd)
    m = 2
    while 2 * m < seq:
        xp = _dot(jnp.concatenate([x, p], axis=0), p)
        x = x + p + xp[:n]
        p = xp[n:]
        m *= 2
    x = x + p + _dot(x, p)

    rhs = jnp.concatenate([v * betab, k * betab * jnp.exp(gcb)], axis=1)
    sol = rhs + _dot(x, rhs)
    u, w = sol[:, :dv], sol[:, dv:]
    qg = q * jnp.exp(gcb)
    kg = k * jnp.exp(gl_b - gcb)

    hrows = [slice(h * seq, (h + 1) * seq) for h in range(hv)]
    ws = [_dot(jnp.concatenate([w[hrows[h]], qg[hrows[h]]], axis=0), s0_ref[h]) for h in range(hv)]
    v_new = [u[hrows[h]] - ws[h][:seq] for h in range(hv)]
    upd = [_dot_tn(kg[hrows[h]], v_new[h]) for h in range(hv)]
    for h in range(hv):
        sout_ref[h] = s0_ref[h] * eg[h:h + 1, :] + upd[h]
    o = jnp.concatenate([ws[h][seq:] for h in range(hv)], axis=0) + _dot(a_bd, jnp.concatenate(v_new, axis=0))
    on = o * lax.rsqrt(jnp.mean(o * o, -1, keepdims=True) + EPS) * nw_ref[...] * _silu(z)
    for h in range(hv):
        o_ref[:, h * dv:(h + 1) * dv] = on[h * seq:(h + 1) * seq]


def _gdn_step(proj, conv_w, a_log, dt_bias, norm_w, s0, buf0, batch, seq):
    hv, hk, dk, dv = GDN_V_HEADS, GDN_K_HEADS, GDN_DK, GDN_DV
    assert seq == SUBLANES
    qw, vw, abw = GDN_QK, GDN_V, hk * LANES
    alog_tab, dt_tab = _gdn_head_tables(a_log, dt_bias)
    alog_tab = alog_tab.reshape(hk * seq, LANES)
    dt_tab = dt_tab.reshape(hk * seq, LANES)
    return pl.pallas_call(
        functools.partial(_gdn_step_kernel, seq=seq),
        grid=(batch,),
        in_specs=[pl.BlockSpec((seq, qw), lambda b: (b, 0)),
                  pl.BlockSpec((seq, qw), lambda b: (b, 1)),
                  pl.BlockSpec((seq, vw), lambda b: (b, 2 * qw // vw)),
                  pl.BlockSpec((seq, vw), lambda b: (b, GDN_CONV_CH // vw)),
                  pl.BlockSpec((seq, abw), lambda b: (b, (GDN_CONV_CH + GDN_V) // abw)),
                  pl.BlockSpec((None, CONV_W - 1, qw), lambda b: (b, 0, 0)),
                  pl.BlockSpec((None, CONV_W - 1, qw), lambda b: (b, 0, 1)),
                  pl.BlockSpec((None, CONV_W - 1, vw), lambda b: (b, 0, 2 * qw // vw)),
                  pl.BlockSpec((CONV_W, qw), lambda b: (0, 0)),
                  pl.BlockSpec((CONV_W, qw), lambda b: (0, 1)),
                  pl.BlockSpec((CONV_W, vw), lambda b: (0, 2 * qw // vw)),
                  pl.BlockSpec((hk * seq, LANES), lambda b: (0, 0)),
                  pl.BlockSpec((hk * seq, LANES), lambda b: (0, 0)),
                  pl.BlockSpec((1, dv), lambda b: (0, 0)),
                  pl.BlockSpec((None, hv, dk, dv), lambda b: (b, 0, 0, 0))],
        out_specs=[pl.BlockSpec((seq, vw), lambda b: (b, 0)),
                   pl.BlockSpec((None, hv, dk, dv), lambda b: (b, 0, 0, 0))],
        out_shape=[jax.ShapeDtypeStruct((batch * seq, GDN_V), F32),
                   jax.ShapeDtypeStruct((batch, hv, dk, dv), F32)],
        scratch_shapes=[pltpu.VMEM((2 * SUBLANES, qw), F32),
                        pltpu.VMEM((2 * SUBLANES, qw), F32),
                        pltpu.VMEM((2 * SUBLANES, vw), F32),
                        pltpu.VMEM((hv * seq, LANES), F32)],
        compiler_params=_params(("parallel",)),
        name="gdn_step",
    )(proj, proj, proj, proj, proj, buf0, buf0, buf0, conv_w, conv_w, conv_w, alog_tab, dt_tab,
      norm_w.reshape(1, dv), s0)


def _trunk(x, mod_fn, per_row, batch, seq, pos0, ret_state, gdn_state, conv_state, wts, tm):
    (norm_pre, norm_post, w_gu, w_down, ret_w_in, ret_w_out, gdn_w_in, gdn_conv_w, gdn_a_log,
     gdn_dt_bias, gdn_norm_w, gdn_w_out) = wts
    kw = dict(per_row=per_row, rows_per_batch=seq)
    depth = norm_pre.shape[0]
    new_ret, new_gdn, new_conv = [], [], []
    for i in range(depth):
        mod = mod_fn(i)

        def ffn(x, sub, f):
            a = _mod_mm(x, mod, sub, norm_pre[i, sub], w_gu, (i, f), glu=True, out_dtype=BF16,
                        tm=tm, tn=512, **kw)
            return _mm_out(a, w_down, (i, f), x, mod, sub, norm_post[i, sub], res_scale=FFN_RES,
                           tm=256, **kw)

        x = ffn(x, 0, 0)
        r = i // 2
        if i % 2 == 0:
            proj = _mod_mm(x, mod, 1, norm_pre[i, 1], ret_w_in, (r,), glu=False, out_dtype=F32,
                           tm=tm, tn=512, **kw)
            y, s = _retention(proj, None if ret_state is None else ret_state[r], pos0, batch, seq,
                              c=math.gcd(256, seq), heads=4 if seq >= 256 else RET_HEADS)
            new_ret.append(s)
            x = _mm_out(y, ret_w_out, (r,), x, mod, 1, norm_post[i, 1], res_scale=1.0,
                        tm=256, **kw)
        else:
            proj = _mod_mm(x, mod, 1, norm_pre[i, 1], gdn_w_in, (r,), glu=False, out_dtype=F32,
                           tm=tm, tn=512, **kw)
            if gdn_state is None:
                y, s = _gdn_prompt(proj, gdn_conv_w[r], gdn_a_log[r], gdn_dt_bias[r], gdn_norm_w[r],
                                   batch, seq)
            else:
                y, s = _gdn_step(proj, gdn_conv_w[r], gdn_a_log[r], gdn_dt_bias[r], gdn_norm_w[r],
                                 gdn_state[r], conv_state[r], batch, seq)
            new_gdn.append(s)
            new_conv.append(proj.reshape(batch, seq, -1)[:, seq - (CONV_W - 1):, :GDN_CONV_CH])
            x = _mm_out(y, gdn_w_out, (r,), x, mod, 1, norm_post[i, 1], res_scale=1.0,
                        tm=256, **kw)
        x = ffn(x, 2, 1)
    stack = lambda xs: xs[0][None] if len(xs) == 1 else jnp.stack(xs)
    return x, stack(new_ret), stack(new_gdn), stack(new_conv)


def kernel(x_prompt, x_sample, c_prompt, c_sample, state_ret, state_gdn, state_conv, w_ada, b_ada,
           norm_pre, norm_post, ffn_w_gu, ffn_w_down, ret_w_in, ret_w_out, gdn_w_in, gdn_conv_w,
           gdn_a_log, gdn_dt_bias, gdn_norm_w, gdn_w_out):
    bp, lp, d = x_prompt.shape
    bs, ls, _ = x_sample.shape
    wts = (norm_pre, norm_post, ffn_w_gu.astype(BF16), ffn_w_down.astype(BF16),
           ret_w_in.astype(BF16), ret_w_out.astype(BF16),
           jax.vmap(_gdn_in_weight)(gdn_w_in).astype(BF16),
           gdn_conv_w, gdn_a_log, gdn_dt_bias, gdn_norm_w, gdn_w_out.astype(BF16))

    cs_rows = jnp.repeat(c_sample, ls, axis=0)
    cp_rows = jnp.pad(c_prompt, ((0, SUBLANES - bp), (0, 0)))
    mods = [_ada(cs_rows, cp_rows, w_ada, b_ada, i) for i in range(w_ada.shape[0])]

    y_p, ret_p, gdn_p, conv_p = _trunk(
        x_prompt.reshape(bp * lp, d), lambda i: mods[i][1], False, bp, lp, 0,
        None, None, None, wts, tm=1024)
    y_s, ret_s, gdn_s, conv_s = _trunk(
        x_sample.reshape(bs * ls, d), lambda i: mods[i][0], True, bs, ls, PAST_LEN,
        state_ret, state_gdn, state_conv, wts, tm=512)
    return (y_p.reshape(bp, lp, d), y_s.reshape(bs, ls, d), ret_p, ret_s, gdn_p, gdn_s, conv_p, conv_s)
```

```python
import functools
import math

import jax
import jax.numpy as jnp
from jax import lax
from jax.experimental import pallas as pl
from jax.experimental.pallas import tpu as pltpu

F32 = jnp.float32
BF16 = jnp.bfloat16
HIGHEST = lax.Precision.HIGHEST

EPS = 1e-6
ROPE_BASE = 10000.0
FFN_RES = 0.5
N_SUB = 3
CONV_W = 4
PAST_LEN = 16384

RET_HEADS = 8
RET_DK = 256
RET_DV = 512
GDN_K_HEADS = 16
GDN_V_HEADS = 32
GDN_DK = 128
GDN_DV = 128
GDN_QK = GDN_K_HEADS * GDN_DK
GDN_V = GDN_V_HEADS * GDN_DV
GDN_CONV_CH = 2 * GDN_QK + GDN_V

LANES = 128
SUBLANES = 8
VMEM_LIMIT = 56 * 1024 * 1024
COL_TILE = 1024
GLU_COL_TILE = 512


def _col_tile(n, cap):
    return max(t for t in range(LANES, cap + 1, LANES) if n % t == 0)


def _params(sem):
    return pltpu.CompilerParams(dimension_semantics=sem, vmem_limit_bytes=VMEM_LIMIT)


def _silu(x):
    return x * jax.nn.sigmoid(x)


def _dot(a, b):
    return jnp.dot(a.astype(BF16), b.astype(BF16), preferred_element_type=F32)


def _dot_nt(a, b):
    return lax.dot_general(a.astype(BF16), b.astype(BF16), (((1,), (1,)), ((), ())),
                           preferred_element_type=F32)


def _dot_tn(a, b):
    return lax.dot_general(a.astype(BF16), b.astype(BF16), (((0,), (0,)), ((), ())),
                           preferred_element_type=F32)


def _ada_kernel(cs_ref, cp_ref, w_ref, b_ref, os_ref, op_ref, as_scr, ap_scr):
    @pl.when(pl.program_id(0) == 0)
    def _():
        as_scr[...] = _silu(cs_ref[...]).astype(BF16)
        ap_scr[...] = _silu(cp_ref[...]).astype(BF16)

    w = w_ref[...].astype(BF16)
    b = b_ref[...]
    os_ref[...] = jnp.dot(as_scr[...], w, preferred_element_type=F32) + b
    op_ref[...] = jnp.dot(ap_scr[...], w, preferred_element_type=F32) + b


def _ada(cs, cp, w, b, layer, tn=COL_TILE):
    ms, d = cs.shape
    mp = cp.shape[0]
    depth, _, n = w.shape
    return pl.pallas_call(
        _ada_kernel,
        grid=(n // tn,),
        in_specs=[pl.BlockSpec((ms, d), lambda j: (0, 0)),
                  pl.BlockSpec((mp, d), lambda j: (0, 0)),
                  pl.BlockSpec((None, d, tn), lambda j: (layer, 0, j)),
                  pl.BlockSpec((None, 1, tn), lambda j: (layer, 0, j))],
        out_specs=[pl.BlockSpec((ms, tn), lambda j: (0, j)),
                   pl.BlockSpec((mp, tn), lambda j: (0, j))],
        out_shape=[jax.ShapeDtypeStruct((ms, n), F32), jax.ShapeDtypeStruct((mp, n), F32)],
        scratch_shapes=[pltpu.VMEM((ms, d), BF16), pltpu.VMEM((mp, d), BF16)],
        compiler_params=_params(("arbitrary",)),
        name="ada_table",
    )(cs, cp, w, b.reshape(depth, 1, n))


def _mod_rows(ref, per_row, tiles_per_batch):
    if per_row:
        return jnp.repeat(ref[...], per_row, axis=0)
    return ref[pl.ds(pl.program_id(0) // tiles_per_batch, 1), :]


def _mod_mm_kernel(x_ref, sh_ref, sc_ref, nw_ref, *rest, glu, per_row, tiles_per_batch):
    if glu:
        wg_ref, wu_ref, o_ref, h_scr = rest
    else:
        w_ref, o_ref, h_scr = rest

    @pl.when(pl.program_id(1) == 0)
    def _():
        x = x_ref[...]
        y = x * lax.rsqrt(jnp.mean(x * x, -1, keepdims=True) + EPS) * nw_ref[...]
        sc = _mod_rows(sc_ref, per_row, tiles_per_batch)
        sh = _mod_rows(sh_ref, per_row, tiles_per_batch)
        h_scr[...] = (y * (1.0 + sc) + sh).astype(BF16)

    h = h_scr[...]
    if glu:
        g = jnp.dot(h, wg_ref[...], preferred_element_type=F32)
        u = jnp.dot(h, wu_ref[...], preferred_element_type=F32)
        o_ref[...] = (_silu(g) * u).astype(o_ref.dtype)
    else:
        o_ref[...] = jnp.dot(h, w_ref[...], preferred_element_type=F32).astype(o_ref.dtype)


def _mod_mm(x, mod, sub, nw, w, widx, *, glu, per_row, rows_per_batch, out_dtype, tm, tn):
    t, d = x.shape
    n = w.shape[-1] // 2 if glu else w.shape[-1]
    nj = n // tn
    lead = (None,) * len(widx)
    tiles_per_batch = max(rows_per_batch // tm, 1)
    per_row = rows_per_batch if per_row else 0
    mrows = tm // rows_per_batch if per_row else mod.shape[0]

    def mod_spec(c):
        col = sub * 3 + c
        if per_row:
            return pl.BlockSpec((mrows, d), lambda i, j: (i, col))
        return pl.BlockSpec((mrows, d), lambda i, j: (0, col))

    in_specs = [pl.BlockSpec((tm, d), lambda i, j: (i, 0)),
                mod_spec(0), mod_spec(1),
                pl.BlockSpec((1, d), lambda i, j: (0, 0))]
    args = [x, mod, mod, nw.reshape(1, d)]
    if glu:
        in_specs += [pl.BlockSpec(lead + (d, tn), lambda i, j: widx + (0, j)),
                     pl.BlockSpec(lead + (d, tn), lambda i, j: widx + (0, j + nj))]
        args += [w, w]
    else:
        in_specs += [pl.BlockSpec(lead + (d, tn), lambda i, j: widx + (0, j))]
        args += [w]
    return pl.pallas_call(
        functools.partial(_mod_mm_kernel, glu=glu, per_row=per_row, tiles_per_batch=tiles_per_batch),
        grid=(t // tm, nj),
        in_specs=in_specs,
        out_specs=pl.BlockSpec((tm, tn), lambda i, j: (i, j)),
        out_shape=jax.ShapeDtypeStruct((t, n), out_dtype),
        scratch_shapes=[pltpu.VMEM((tm, d), BF16)],
        compiler_params=_params(("parallel", "arbitrary")),
        name="mod_mm_glu" if glu else "mod_mm",
    )(*args)


def _mm_out_kernel(a_ref, w_ref, x_ref, g_ref, nw_ref, o_ref, *, res_scale, per_row, tiles_per_batch):
    y = jnp.dot(a_ref[...].astype(BF16), w_ref[...], preferred_element_type=F32)
    yn = y * lax.rsqrt(jnp.mean(y * y, -1, keepdims=True) + EPS) * nw_ref[...]
    gate = _mod_rows(g_ref, per_row, tiles_per_batch)
    o_ref[...] = x_ref[...] + res_scale * (gate * yn)


def _mm_out(a, w, widx, x, mod, sub, nw, *, res_scale, per_row, rows_per_batch, tm):
    t, kdim = a.shape
    d = w.shape[-1]
    lead = (None,) * len(widx)
    tiles_per_batch = max(rows_per_batch // tm, 1)
    per_row = rows_per_batch if per_row else 0
    col = sub * 3 + 2
    if per_row:
        g_spec = pl.BlockSpec((tm // rows_per_batch, d), lambda i: (i, col))
    else:
        g_spec = pl.BlockSpec((mod.shape[0], d), lambda i: (0, col))
    return pl.pallas_call(
        functools.partial(_mm_out_kernel, res_scale=res_scale, per_row=per_row,
                          tiles_per_batch=tiles_per_batch),
        grid=(t // tm,),
        in_specs=[pl.BlockSpec((tm, kdim), lambda i: (i, 0)),
                  pl.BlockSpec(lead + (kdim, d), lambda i: widx + (0, 0), pipeline_mode=pl.Buffered(1)),
                  pl.BlockSpec((tm, d), lambda i: (i, 0)),
                  g_spec,
                  pl.BlockSpec((1, d), lambda i: (0, 0))],
        out_specs=pl.BlockSpec((tm, d), lambda i: (i, 0)),
        out_shape=jax.ShapeDtypeStruct((t, d), F32),
        compiler_params=_params(("parallel",)),
        name="mm_out",
    )(a, w, x, mod, nw.reshape(1, d))


def _rotate(x, cos, sin):
    half = x.shape[-1] // 2
    x1, x2 = x[:, :half], x[:, half:]
    return jnp.concatenate([x1 * cos - x2 * sin, x1 * sin + x2 * cos], axis=-1)


def _ret_kernel(dm_ref, qd_ref, kd_ref, cd_ref, cos_ref, sin_ref, q_ref, k_ref, v_ref, g_ref, *rest,
                heads, has_state):
    if has_state:
        s0_ref, o_ref, sout_ref, s_scr = rest
    else:
        o_ref, sout_ref, s_scr = rest
    n = pl.program_id(2)
    dk, dv = RET_DK, RET_DV

    @pl.when(n == 0)
    def _():
        if has_state:
            s_scr[...] = s0_ref[...]
        else:
            s_scr[...] = jnp.zeros_like(s_scr)

    cos, sin = cos_ref[...], sin_ref[...]
    hs = range(heads)
    q = [_rotate(q_ref[:, h * dk:(h + 1) * dk], cos, sin) * (dk ** -0.5) for h in hs]
    k = [_rotate(k_ref[:, h * dk:(h + 1) * dk], cos, sin) for h in hs]
    v = [v_ref[:, h * dv:(h + 1) * dv].astype(BF16) for h in hs]
    s = [s_scr[h] for h in hs]
    scores = [_dot_nt(q[h], k[h]) * dm_ref[h] for h in hs]
    cross = [_dot(q[h], s[h]) for h in hs]
    inner = [_dot(scores[h], v[h]) for h in hs]
    upd = [_dot_tn(k[h] * jnp.concatenate([kd_ref[h]] * (dk // LANES), axis=1), v[h]) for h in hs]
    for h in hs:
        s_new = s[h] * cd_ref[h, 0:1, 0:1] + upd[h]
        s_scr[h] = s_new
        o = inner[h] + cross[h] * jnp.concatenate([qd_ref[h]] * (dv // LANES), axis=1)
        on = o * lax.rsqrt(jnp.mean(o * o, -1, keepdims=True) + EPS)
        o_ref[:, h * dv:(h + 1) * dv] = _silu(g_ref[:, h * dv:(h + 1) * dv]) * on

    @pl.when(n == pl.num_programs(2) - 1)
    def _():
        sout_ref[...] = s_scr[...]


def _retention(proj, s0, pos0, batch, seq, c, heads):
    nh, dk, dv = RET_HEADS, RET_DK, RET_DV
    nc = seq // c
    ng = nh // heads
    half = dk // 2
    log_g = jnp.log1p(-jnp.exp2(-5.0 - jnp.arange(nh, dtype=F32)))
    idx = jnp.arange(c, dtype=F32)
    diff = idx[:, None] - idx[None, :]
    causal = diff >= 0
    dmask = jnp.where(causal[None], jnp.exp(log_g[:, None, None] * jnp.where(causal, diff, 0.0)[None]), 0.0)
    lanes = lambda x: jnp.broadcast_to(x[:, :, None], (nh, x.shape[1], LANES))
    q_decay = lanes(jnp.exp(log_g[:, None] * (idx[None, :] + 1.0)))
    k_decay = lanes(jnp.exp(log_g[:, None] * (c - 1.0 - idx)[None, :]))
    chunk_decay = jnp.broadcast_to(jnp.exp(log_g * c)[:, None, None], (nh, SUBLANES, LANES))
    inv = 1.0 / (ROPE_BASE ** jnp.linspace(0.0, 1.0, half, dtype=F32))
    ang = (pos0 + jnp.arange(seq)).astype(F32)[:, None] * inv[None, :]
    cos, sin = jnp.cos(ang), jnp.sin(ang)
    has_state = s0 is not None
    tokb = lambda b, g, n: b * nc + n
    in_specs = [pl.BlockSpec((heads, c, c), lambda b, g, n: (g, 0, 0)),
                pl.BlockSpec((heads, c, LANES), lambda b, g, n: (g, 0, 0)),
                pl.BlockSpec((heads, c, LANES), lambda b, g, n: (g, 0, 0)),
                pl.BlockSpec((heads, SUBLANES, LANES), lambda b, g, n: (g, 0, 0)),
                pl.BlockSpec((c, half), lambda b, g, n: (n, 0)),
                pl.BlockSpec((c, half), lambda b, g, n: (n, 0)),
                pl.BlockSpec((c, heads * dk), lambda b, g, n: (tokb(b, g, n), g)),
                pl.BlockSpec((c, heads * dk), lambda b, g, n: (tokb(b, g, n), ng + g)),
                pl.BlockSpec((c, heads * dv), lambda b, g, n: (tokb(b, g, n), ng + g)),
                pl.BlockSpec((c, heads * dv), lambda b, g, n: (tokb(b, g, n), 2 * ng + g))]
    args = [dmask, q_decay, k_decay, chunk_decay, cos, sin, proj, proj, proj, proj]
    if has_state:
        in_specs.append(pl.BlockSpec((None, heads, dk, dv), lambda b, g, n: (b, g, 0, 0)))
        args.append(s0)
    return pl.pallas_call(
        functools.partial(_ret_kernel, heads=heads, has_state=has_state),
        grid=(batch, ng, nc),
        in_specs=in_specs,
        out_specs=[pl.BlockSpec((c, heads * dv), lambda b, g, n: (tokb(b, g, n), g)),
                   pl.BlockSpec((None, heads, dk, dv), lambda b, g, n: (b, g, 0, 0))],
        out_shape=[jax.ShapeDtypeStruct((batch * seq, nh * dv), F32),
                   jax.ShapeDtypeStruct((batch, nh, dk, dv), F32)],
        scratch_shapes=[pltpu.VMEM((heads, dk, dv), F32)],
        compiler_params=_params(("parallel", "parallel", "arbitrary")),
        name="retention",
    )(*args)


def _gdn_in_weight(w_in):
    d = w_in.shape[0]
    main = w_in[:, :GDN_CONV_CH + GDN_V]
    a = w_in[:, GDN_CONV_CH + GDN_V:GDN_CONV_CH + GDN_V + GDN_V_HEADS].reshape(d, GDN_K_HEADS, 2)
    b = w_in[:, GDN_CONV_CH + GDN_V + GDN_V_HEADS:].reshape(d, GDN_K_HEADS, 2)
    tail = jnp.concatenate([a, b, jnp.zeros((d, GDN_K_HEADS, LANES - 4), w_in.dtype)], axis=-1)
    return jnp.concatenate([main, tail.reshape(d, GDN_K_HEADS * LANES)], axis=-1)


def _gdn_head_tables(a_log, dt_bias):
    def tab(x):
        row = jnp.concatenate([x.reshape(GDN_K_HEADS, 2), jnp.zeros((GDN_K_HEADS, LANES - 2), F32)], -1)
        return jnp.broadcast_to(row[:, None, :], (GDN_K_HEADS, SUBLANES, LANES))
    return tab(a_log), tab(dt_bias)


GDN_C = 64
GDN_SUB = 256


def _chunk_cumsum(x, c):
    pos = lax.broadcasted_iota(jnp.int32, x.shape, 0) % c
    s = 1
    while s < c:
        x = x + jnp.where(pos >= s, pltpu.roll(x, s, axis=0), 0.0)
        s *= 2
    return x


GDN_BASE = 8


def _inverse_masks(c, width):
    i = jnp.arange(c)[:, None]
    j = (jnp.arange(width) % c)[None, :]
    r = (jnp.arange(width) % (2 * c))[:, None]
    s = jnp.arange(2 * c)[None, :]
    lanes = [(i // GDN_BASE) == (j // GDN_BASE)]
    pair = [(r // c) == (s // c), (r // GDN_BASE) == (s // GDN_BASE)]
    b = GDN_BASE
    while b < c:
        lanes.append(((i // (2 * b)) == (j // (2 * b))) & ((i // b) % 2 == 1) & ((j // b) % 2 == 0))
        pair.append(((r // (2 * b)) == (s // (2 * b))) & ((r // b) % 2 == 1) & ((s // b) % 2 == 0))
        b *= 2
    causal = jnp.stack([pair[0] & (r >= s), pair[0] & (r > s)]).astype(F32)
    return jnp.stack(lanes).astype(F32), jnp.stack(pair).astype(BF16), causal


def _mask_dict(lanes_ref, pair_ref):
    masks = {"base_lanes": lanes_ref[0], "same_pair": pair_ref[0], "base_pair": pair_ref[1]}
    for n in range(1, lanes_ref.shape[0]):
        masks[f"lanes{GDN_BASE << (n - 1)}"] = lanes_ref[n]
        masks[f"pair{GDN_BASE << (n - 1)}"] = pair_ref[n + 1]
    return masks


def _pair_to_diag(p):
    width, pc = p.shape
    n = width // pc
    zero = jnp.zeros((pc, pc), p.dtype)
    return jnp.concatenate(
        [jnp.concatenate([p[i * pc:(i + 1) * pc] if j == i else zero for j in range(n)], axis=1)
         for i in range(n)], axis=0)


def _lanes_to_diag(p, masks):
    c, width = p.shape
    pb = p.astype(BF16)
    pair = jnp.concatenate([jnp.concatenate([pb[:, i * 2 * c:(i + 1) * 2 * c]] * 2, axis=0)
                            for i in range(width // (2 * c))], axis=0)
    return _pair_to_diag(pair * masks["same_pair"])


def _inverse_minus_eye_lanes(l_lbs, l_pairs, masks):
    c = l_lbs[0].shape[0]
    ps = range(len(l_lbs))
    mm = lambda a, b: jnp.dot(a.astype(BF16), b, preferred_element_type=F32)

    lb = [l_lbs[i] * masks["base_lanes"] for i in ps]
    x = [-lb[i] for i in ps]
    p = [mm(lb[i], _pair_to_diag(l_pairs[i] * masks["base_pair"])) for i in ps]
    n = 2
    while 2 * n < GDN_BASE:
        xp = [mm(jnp.concatenate([x[i], p[i]], axis=0), _lanes_to_diag(p[i], masks)) for i in ps]
        x = [x[i] + p[i] + xp[i][:c] for i in ps]
        p = [xp[i][c:] for i in ps]
        n *= 2
    xp = [mm(x[i], _lanes_to_diag(p[i], masks)) for i in ps]
    x = [x[i] + p[i] + xp[i] for i in ps]

    b = GDN_BASE
    while b < c:
        t = [l_lbs[i] * masks[f"lanes{b}"]
             + mm(x[i], _pair_to_diag(l_pairs[i] * masks[f"pair{b}"])) for i in ps]
        tx = [mm(t[i], _lanes_to_diag(x[i], masks)) for i in ps]
        x = [x[i] - t[i] - tx[i] for i in ps]
        b *= 2
    return x


def _gdn_prep_kernel(q_ref, k_ref, v_ref, ab_ref, qp_ref, kp_ref, vp_ref, wq_ref, wk_ref, wv_ref,
                     alog_ref, dt_ref, ml_ref, mp_ref, mc_ref, u_ref, w_ref, qg_ref, kgt_ref, at_ref, eg_ref,
                     bq, bk, bv, gc_scr, *, tb):
    c, sub = GDN_C, GDN_SUB
    first = pl.program_id(1) == 0

    def conv_silu(buf, prev_ref, x_ref, cw_ref):
        buf[pl.ds(0, SUBLANES), :] = jnp.where(first, 0.0, prev_ref[...])
        buf[pl.ds(SUBLANES, tb), :] = x_ref[...]
        cw = cw_ref[...]
        out = buf[pl.ds(SUBLANES - 3, tb), :] * cw[0:1, :]
        for j in range(1, CONV_W):
            out = out + buf[pl.ds(SUBLANES - 3 + j, tb), :] * cw[j:j + 1, :]
        return _silu(out)

    q = conv_silu(bq, qp_ref, q_ref, wq_ref)
    k = conv_silu(bk, kp_ref, k_ref, wk_ref)
    v2 = conv_silu(bv, vp_ref, v_ref, wv_ref)
    q = q * lax.rsqrt(jnp.sum(q * q, -1, keepdims=True) + EPS) * (GDN_DK ** -0.5)
    k = k * lax.rsqrt(jnp.sum(k * k, -1, keepdims=True) + EPS)

    ab = ab_ref[...]
    g_all = -jnp.exp(alog_ref[0:1, :]) * jax.nn.softplus(ab + dt_ref[0:1, :])
    beta_all = jax.nn.sigmoid(ab)
    gc_all = _chunk_cumsum(g_all, c)
    gc_t = gc_all.T

    masks = _mask_dict(ml_ref, mp_ref)
    tri = mc_ref[0] > 0.0
    strict = mc_ref[1] > 0.0

    nh, nchunk, pc = tb // sub, sub // c, 2 * c
    rows = [slice(hf * sub, (hf + 1) * sub) for hf in range(nh)]
    pairs = [slice(i * pc, (i + 1) * pc) for i in range(tb // pc)]
    kq = [_dot_nt(jnp.concatenate([k[pr], q[pr]], axis=0), k[pr]) for pr in pairs]
    npair = sub // pc
    kk = [jnp.concatenate([kq[hf * npair + i][:pc] for i in range(npair)], axis=0) for hf in range(nh)]
    qk = [jnp.concatenate([kq[hf * npair + i][pc:] for i in range(npair)], axis=0) for hf in range(nh)]
    probs = [(hf, e) for hf in range(nh) for e in range(2)]
    gcb, betab, gl, l_pair, l_lb = [], [], [], [], []
    for p, (hf, e) in enumerate(probs):
        lanes = slice(e * GDN_DV, (e + 1) * GDN_DV)
        gcb.append(jnp.broadcast_to(gc_all[rows[hf], e:e + 1], (sub, LANES)))
        betab.append(jnp.broadcast_to(beta_all[rows[hf], 2 + e:3 + e], (sub, LANES)))
        gc_scr[p] = gcb[p]
        gl.append(gc_scr[p, pl.ds(c - 1, nchunk, stride=c), :])
        gc_cols = jnp.concatenate(
            [jnp.broadcast_to(gc_t[e:e + 1, hf * sub + i * pc:hf * sub + (i + 1) * pc], (pc, pc))
             for i in range(npair)], axis=0)
        decay = jnp.where(tri, jnp.exp(jnp.where(tri, gcb[p] - gc_cols, 0.0)), 0.0)
        lp = jnp.where(strict, kk[hf] * betab[p] * decay, 0.0)
        at_ref[rows[hf], lanes] = (qk[hf] * decay).astype(BF16)
        l_pair.append(lp.astype(BF16))
        l_lb.append(jnp.concatenate([lp[2 * i * c:(2 * i + 1) * c] + lp[(2 * i + 1) * c:(2 * i + 2) * c]
                                     for i in range(npair)], axis=1))
    x_lb = _inverse_minus_eye_lanes(l_lb, l_pair, masks)
    rhs = [jnp.concatenate([v2[rows[hf], e * GDN_DV:(e + 1) * GDN_DV] * betab[p],
                            k[rows[hf]] * betab[p] * jnp.exp(gcb[p])], axis=1)
           for p, (hf, e) in enumerate(probs)]
    corr = [jnp.dot(_lanes_to_diag(x_lb[p], masks), rhs[p].astype(BF16), preferred_element_type=F32)
            for p in range(len(probs))]
    for p, (hf, e) in enumerate(probs):
        lanes = slice(e * GDN_DV, (e + 1) * GDN_DV)
        sol = rhs[p] + corr[p]
        u_ref[rows[hf], lanes] = sol[:, :GDN_DV]
        w_ref[rows[hf], lanes] = sol[:, GDN_DV:].astype(BF16)
        qg_ref[rows[hf], lanes] = (q[rows[hf]] * jnp.exp(gcb[p])).astype(BF16)
        gl_b = jnp.concatenate([jnp.broadcast_to(gl[p][i:i + 1, :], (c, LANES)) for i in range(nchunk)], axis=0)
        kg = k[rows[hf]] * jnp.exp(gl_b - gcb[p])
        kgt_ref[rows[hf], lanes] = jnp.concatenate(
            [kg[i * pc:(i + 1) * pc].T for i in range(npair)], axis=0).astype(BF16)
        eg_ref[hf * nchunk:(hf + 1) * nchunk, lanes] = jnp.exp(gl[p])


def _gdn_scan_kernel(u_ref, w_ref, qg_ref, kgt_ref, at_ref, eg_ref, z_ref, nw_ref, o_ref, sout_ref, s_scr,
                     *, tb, heads):
    c = GDN_C

    @pl.when(pl.program_id(2) == 0)
    def _():
        s_scr[...] = jnp.zeros_like(s_scr)

    nw = nw_ref[...]
    hl = [slice(h * GDN_DV, (h + 1) * GDN_DV) for h in range(heads)]
    s = [s_scr[h] for h in range(heads)]
    zeros = jnp.zeros((c, GDN_DV), BF16)
    for i in range(tb // c):
        rows = slice(i * c, (i + 1) * c)
        pair_rows = slice((i // 2) * 2 * c, (i // 2 + 1) * 2 * c)
        ws = [jnp.dot(jnp.concatenate([w_ref[rows, hl[h]], qg_ref[rows, hl[h]]], axis=0),
                      s[h].astype(BF16), preferred_element_type=F32) for h in range(heads)]
        v_new = [(u_ref[rows, hl[h]] - ws[h][:c]).astype(BF16) for h in range(heads)]
        pair = [jnp.concatenate([v_new[h], zeros] if i % 2 == 0 else [zeros, v_new[h]], axis=0)
                for h in range(heads)]
        for h in range(heads):
            o = ws[h][c:] + jnp.dot(at_ref[rows, hl[h]], pair[h], preferred_element_type=F32)
            s[h] = s[h] * eg_ref[i:i + 1, hl[h]] + jnp.dot(kgt_ref[pair_rows, hl[h]], pair[h],
                                                          preferred_element_type=F32)
            on = o * lax.rsqrt(jnp.mean(o * o, -1, keepdims=True) + EPS) * nw
            o_ref[rows, hl[h]] = on * _silu(z_ref[rows, hl[h]])
    for h in range(heads):
        s_scr[h] = s[h]

    @pl.when(pl.program_id(2) == pl.num_programs(2) - 1)
    def _():
        sout_ref[...] = s_scr[...]


def _gdn_prompt(proj, conv_w, a_log, dt_bias, norm_w, batch, seq, tb=512, heads=8):
    hk, dk, dv = GDN_K_HEADS, GDN_DK, GDN_DV
    t = batch * seq
    nt = seq // tb
    kq0 = GDN_QK // dk
    v0 = 2 * GDN_QK // (2 * dv)
    ab0 = (GDN_CONV_CH + GDN_V) // LANES
    alog_tab, dt_tab = _gdn_head_tables(a_log, dt_bias)
    m_lanes, m_pair, m_causal = _inverse_masks(GDN_C, GDN_SUB)
    tok = lambda b, n, j: b * nt + n
    prev = lambda b, n, j: jnp.maximum((b * seq + n * tb) // SUBLANES - 1, 0)
    u, w, qg, kg, at, eg = pl.pallas_call(
        functools.partial(_gdn_prep_kernel, tb=tb),
        grid=(batch, nt, hk),
        in_specs=[pl.BlockSpec((tb, dk), lambda b, n, j: (tok(b, n, j), j)),
                  pl.BlockSpec((tb, dk), lambda b, n, j: (tok(b, n, j), kq0 + j)),
                  pl.BlockSpec((tb, 2 * dv), lambda b, n, j: (tok(b, n, j), v0 + j)),
                  pl.BlockSpec((tb, LANES), lambda b, n, j: (tok(b, n, j), ab0 + j)),
                  pl.BlockSpec((SUBLANES, dk), lambda b, n, j: (prev(b, n, j), j)),
                  pl.BlockSpec((SUBLANES, dk), lambda b, n, j: (prev(b, n, j), kq0 + j)),
                  pl.BlockSpec((SUBLANES, 2 * dv), lambda b, n, j: (prev(b, n, j), v0 + j)),
                  pl.BlockSpec((CONV_W, dk), lambda b, n, j: (0, j)),
                  pl.BlockSpec((CONV_W, dk), lambda b, n, j: (0, kq0 + j)),
                  pl.BlockSpec((CONV_W, 2 * dv), lambda b, n, j: (0, v0 + j)),
                  pl.BlockSpec((None, SUBLANES, LANES), lambda b, n, j: (j, 0, 0)),
                  pl.BlockSpec((None, SUBLANES, LANES), lambda b, n, j: (j, 0, 0)),
                  pl.BlockSpec(m_lanes.shape, lambda b, n, j: (0, 0, 0)),
                  pl.BlockSpec(m_pair.shape, lambda b, n, j: (0, 0, 0)),
                  pl.BlockSpec(m_causal.shape, lambda b, n, j: (0, 0, 0))],
        out_specs=[pl.BlockSpec((tb, 2 * dv), lambda b, n, j: (tok(b, n, j), j))] * 5
        + [pl.BlockSpec((tb // GDN_C, 2 * dv), lambda b, n, j: (tok(b, n, j), j))],
        out_shape=[jax.ShapeDtypeStruct((t, GDN_V), F32)]
        + [jax.ShapeDtypeStruct((t, GDN_V), BF16)] * 4
        + [jax.ShapeDtypeStruct((t // GDN_C, GDN_V), F32)],
        scratch_shapes=[pltpu.VMEM((SUBLANES + tb, dk), F32),
                        pltpu.VMEM((SUBLANES + tb, dk), F32),
                        pltpu.VMEM((SUBLANES + tb, 2 * dv), F32),
                        pltpu.VMEM((2 * tb // GDN_SUB, GDN_SUB, LANES), F32)],
        compiler_params=_params(("parallel", "parallel", "parallel")),
        name="gdn_prep",
    )(proj, proj, proj, proj, proj, proj, proj, conv_w, conv_w, conv_w, alog_tab, dt_tab,
      m_lanes, m_pair, m_causal)

    wide = heads * dv
    z0 = GDN_CONV_CH // wide
    blk = lambda b, g, n: (b * nt + n, g)
    return pl.pallas_call(
        functools.partial(_gdn_scan_kernel, tb=tb, heads=heads),
        grid=(batch, GDN_V_HEADS // heads, nt),
        in_specs=[pl.BlockSpec((tb, wide), blk)] * 5
        + [pl.BlockSpec((tb // GDN_C, wide), blk),
           pl.BlockSpec((tb, wide), lambda b, g, n: (b * nt + n, z0 + g)),
           pl.BlockSpec((1, dv), lambda b, g, n: (0, 0))],
        out_specs=[pl.BlockSpec((tb, wide), blk),
                   pl.BlockSpec((None, heads, dk, dv), lambda b, g, n: (b, g, 0, 0))],
        out_shape=[jax.ShapeDtypeStruct((t, GDN_V), F32),
                   jax.ShapeDtypeStruct((batch, GDN_V_HEADS, dk, dv), F32)],
        scratch_shapes=[pltpu.VMEM((heads, dk, dv), F32)],
        compiler_params=_params(("parallel", "parallel", "arbitrary")),
        name="gdn_scan",
    )(u, w, qg, kg, at, eg, proj, norm_w.reshape(1, dv))


def _gdn_step_kernel(q_ref, k_ref, v_ref, z_ref, ab_ref, cq_ref, ck_ref, cv_ref, wq_ref, wk_ref, wv_ref,
                     alog_ref, dt_ref, nw_ref, s0_ref, o_ref, sout_ref, cout_ref, bq, bk, bv, g_scr, *, seq):
    hv, hk, dk, dv = GDN_V_HEADS, GDN_K_HEADS, GDN_DK, GDN_DV
    n = hv * seq

    def conv_silu(buf, c_ref, x_ref, cw_ref, col):
        width = buf.shape[1]
        buf[pl.ds(SUBLANES - 3, 3), :] = c_ref[...]
        buf[pl.ds(SUBLANES, seq), :] = x_ref[...]
        cout_ref[:, col:col + width] = buf[pl.ds(SUBLANES + seq - 3, 3), :]
        cw = cw_ref[...]
        out = buf[pl.ds(SUBLANES - 3, seq), :] * cw[0:1, :]
        for j in range(1, CONV_W):
            out = out + buf[pl.ds(SUBLANES - 3 + j, seq), :] * cw[j:j + 1, :]
        return _silu(out)

    def stack(x, width):
        return jnp.concatenate([x[:, h * width:(h + 1) * width] for h in range(x.shape[1] // width)], axis=0)

    q = stack(conv_silu(bq, cq_ref, q_ref, wq_ref, 0), dk)
    k = stack(conv_silu(bk, ck_ref, k_ref, wk_ref, GDN_QK), dk)
    q = q * lax.rsqrt(jnp.sum(q * q, -1, keepdims=True) + EPS) * (dk ** -0.5)
    k = k * lax.rsqrt(jnp.sum(k * k, -1, keepdims=True) + EPS)
    rep = hv // hk
    q = jnp.concatenate([q[(h // rep) * seq:(h // rep + 1) * seq] for h in range(hv)], axis=0)
    k = jnp.concatenate([k[(h // rep) * seq:(h // rep + 1) * seq] for h in range(hv)], axis=0)
    v = stack(conv_silu(bv, cv_ref, v_ref, wv_ref, 2 * GDN_QK), dv)
    z = stack(z_ref[...], dv)

    ab = stack(ab_ref[...], LANES)
    g_st = -jnp.exp(alog_ref[...]) * jax.nn.softplus(ab + dt_ref[...])
    gc_st = _chunk_cumsum(g_st, seq)
    beta_st = jax.nn.sigmoid(ab)

    def per_head_rows(x_st, lane0):
        cols = [jnp.broadcast_to(x_st[:, lane0 + e:lane0 + e + 1], (hk * seq, LANES)) for e in range(rep)]
        return jnp.concatenate([cols[h % rep][(h // rep) * seq:(h // rep + 1) * seq] for h in range(hv)], axis=0)

    gcb = per_head_rows(gc_st, 0)
    betab = per_head_rows(beta_st, rep)
    gc_row = gcb.T[0:1, :]
    g_scr[...] = gcb
    gl = g_scr[pl.ds(seq - 1, hv, stride=seq), :]
    gl_b = jnp.concatenate([jnp.broadcast_to(gl[h:h + 1, :], (seq, LANES)) for h in range(hv)], axis=0)
    eg = jnp.exp(gl)

    r = lax.broadcasted_iota(jnp.int32, (n, n), 0)
    s = lax.broadcasted_iota(jnp.int32, (n, n), 1)
    same = (r // seq) == (s // seq)
    tri = same & (r >= s)
    strict = same & (r > s)
    wide = lambda x: jnp.concatenate([x] * (n // LANES), axis=1)
    decay = jnp.where(tri, jnp.exp(jnp.where(tri, wide(gcb) - gc_row, 0.0)), 0.0)
    kq = _dot_nt(jnp.concatenate([k, q], axis=0), k)
    l_bd = jnp.where(strict, kq[:n] * wide(betab) * decay, 0.0)
    a_bd = kq[n:] * decay

    x = -l_bd
    p = _dot(l_bd, l_bd)
    m = 2
    while 2 * m < seq:
        xp = _dot(jnp.concatenate([x, p], axis=0), p)
        x = x + p + xp[:n]
        p = xp[n:]
        m *= 2
    x = x + p + _dot(x, p)

    rhs = jnp.concatenate([v * betab, k * betab * jnp.exp(gcb)], axis=1)
    sol = rhs + _dot(x, rhs)
    u, w = sol[:, :dv], sol[:, dv:]
    qg = q * jnp.exp(gcb)
    kg = k * jnp.exp(gl_b - gcb)

    hrows = [slice(h * seq, (h + 1) * seq) for h in range(hv)]
    ws = [_dot(jnp.concatenate([w[hrows[h]], qg[hrows[h]]], axis=0), s0_ref[h]) for h in range(hv)]
    v_new = [u[hrows[h]] - ws[h][:seq] for h in range(hv)]
    upd = [_dot_tn(kg[hrows[h]], v_new[h]) for h in range(hv)]
    for h in range(hv):
        sout_ref[h] = s0_ref[h] * eg[h:h + 1, :] + upd[h]
    o = jnp.concatenate([ws[h][seq:] for h in range(hv)], axis=0) + _dot(a_bd, jnp.concatenate(v_new, axis=0))
    on = o * lax.rsqrt(jnp.mean(o * o, -1, keepdims=True) + EPS) * nw_ref[...] * _silu(z)
    for h in range(hv):
        o_ref[:, h * dv:(h + 1) * dv] = on[h * seq:(h + 1) * seq]


def _gdn_step(proj, conv_w, a_log, dt_bias, norm_w, s0, buf0, batch, seq):
    hv, hk, dk, dv = GDN_V_HEADS, GDN_K_HEADS, GDN_DK, GDN_DV
    assert seq == SUBLANES
    qw, vw, abw = GDN_QK, GDN_V, hk * LANES
    alog_tab, dt_tab = _gdn_head_tables(a_log, dt_bias)
    alog_tab = alog_tab.reshape(hk * seq, LANES)
    dt_tab = dt_tab.reshape(hk * seq, LANES)
    return pl.pallas_call(
        functools.partial(_gdn_step_kernel, seq=seq),
        grid=(batch,),
        in_specs=[pl.BlockSpec((seq, qw), lambda b: (b, 0)),
                  pl.BlockSpec((seq, qw), lambda b: (b, 1)),
                  pl.BlockSpec((seq, vw), lambda b: (b, 2 * qw // vw)),
                  pl.BlockSpec((seq, vw), lambda b: (b, GDN_CONV_CH // vw)),
                  pl.BlockSpec((seq, abw), lambda b: (b, (GDN_CONV_CH + GDN_V) // abw)),
                  pl.BlockSpec((None, CONV_W - 1, qw), lambda b: (b, 0, 0)),
                  pl.BlockSpec((None, CONV_W - 1, qw), lambda b: (b, 0, 1)),
                  pl.BlockSpec((None, CONV_W - 1, vw), lambda b: (b, 0, 2 * qw // vw)),
                  pl.BlockSpec((CONV_W, qw), lambda b: (0, 0)),
                  pl.BlockSpec((CONV_W, qw), lambda b: (0, 1)),
                  pl.BlockSpec((CONV_W, vw), lambda b: (0, 2 * qw // vw)),
                  pl.BlockSpec((hk * seq, LANES), lambda b: (0, 0)),
                  pl.BlockSpec((hk * seq, LANES), lambda b: (0, 0)),
                  pl.BlockSpec((1, dv), lambda b: (0, 0)),
                  pl.BlockSpec((None, hv, dk, dv), lambda b: (b, 0, 0, 0))],
        out_specs=[pl.BlockSpec((seq, vw), lambda b: (b, 0)),
                   pl.BlockSpec((None, hv, dk, dv), lambda b: (b, 0, 0, 0)),
                   pl.BlockSpec((None, CONV_W - 1, GDN_CONV_CH), lambda b: (b, 0, 0))],
        out_shape=[jax.ShapeDtypeStruct((batch * seq, GDN_V), F32),
                   jax.ShapeDtypeStruct((batch, hv, dk, dv), F32),
                   jax.ShapeDtypeStruct((batch, CONV_W - 1, GDN_CONV_CH), F32)],
        scratch_shapes=[pltpu.VMEM((2 * SUBLANES, qw), F32),
                        pltpu.VMEM((2 * SUBLANES, qw), F32),
                        pltpu.VMEM((2 * SUBLANES, vw), F32),
                        pltpu.VMEM((hv * seq, LANES), F32)],
        compiler_params=_params(("parallel",)),
        name="gdn_step",
    )(proj, proj, proj, proj, proj, buf0, buf0, buf0, conv_w, conv_w, conv_w, alog_tab, dt_tab,
      norm_w.reshape(1, dv), s0)


def _trunk(x, mod_fn, per_row, batch, seq, pos0, ret_state, gdn_state, conv_state, wts, tm):
    (norm_pre, norm_post, w_gu, w_down, ret_w_in, ret_w_out, gdn_w_in, gdn_conv_w, gdn_a_log,
     gdn_dt_bias, gdn_norm_w, gdn_w_out) = wts
    kw = dict(per_row=per_row, rows_per_batch=seq)
    depth = norm_pre.shape[0]
    new_ret, new_gdn, new_conv = [], [], []
    for i in range(depth):
        mod = mod_fn(i)

        def ffn(x, sub, f):
            a = _mod_mm(x, mod, sub, norm_pre[i, sub], w_gu, (i, f), glu=True, out_dtype=BF16,
                        tm=tm, tn=_col_tile(w_gu.shape[-1] // 2, GLU_COL_TILE), **kw)
            return _mm_out(a, w_down, (i, f), x, mod, sub, norm_post[i, sub], res_scale=FFN_RES,
                           tm=256, **kw)

        x = ffn(x, 0, 0)
        r = i // 2
        if i % 2 == 0:
            proj = _mod_mm(x, mod, 1, norm_pre[i, 1], ret_w_in, (r,), glu=False, out_dtype=F32,
                           tm=tm, tn=_col_tile(ret_w_in.shape[-1], COL_TILE), **kw)
            y, s = _retention(proj, None if ret_state is None else ret_state[r], pos0, batch, seq,
                              c=math.gcd(256, seq), heads=4 if seq >= 256 else RET_HEADS)
            new_ret.append(s)
            x = _mm_out(y, ret_w_out, (r,), x, mod, 1, norm_post[i, 1], res_scale=1.0,
                        tm=256, **kw)
        else:
            proj = _mod_mm(x, mod, 1, norm_pre[i, 1], gdn_w_in, (r,), glu=False, out_dtype=F32,
                           tm=tm, tn=_col_tile(gdn_w_in.shape[-1], COL_TILE), **kw)
            if gdn_state is None:
                y, s = _gdn_prompt(proj, gdn_conv_w[r], gdn_a_log[r], gdn_dt_bias[r], gdn_norm_w[r],
                                   batch, seq)
                cs = proj.reshape(batch, seq, -1)[:, seq - (CONV_W - 1):, :GDN_CONV_CH]
            else:
                y, s, cs = _gdn_step(proj, gdn_conv_w[r], gdn_a_log[r], gdn_dt_bias[r], gdn_norm_w[r],
                                     gdn_state[r], conv_state[r], batch, seq)
            new_gdn.append(s)
            new_conv.append(cs)
            x = _mm_out(y, gdn_w_out, (r,), x, mod, 1, norm_post[i, 1], res_scale=1.0,
                        tm=256, **kw)
        x = ffn(x, 2, 1)
    stack = lambda xs: xs[0][None] if len(xs) == 1 else jnp.stack(xs)
    return x, stack(new_ret), stack(new_gdn), stack(new_conv)


def kernel(x_prompt, x_sample, c_prompt, c_sample, state_ret, state_gdn, state_conv, w_ada, b_ada,
           norm_pre, norm_post, ffn_w_gu, ffn_w_down, ret_w_in, ret_w_out, gdn_w_in, gdn_conv_w,
           gdn_a_log, gdn_dt_bias, gdn_norm_w, gdn_w_out):
    bp, lp, d = x_prompt.shape
    bs, ls, _ = x_sample.shape
    wts = (norm_pre, norm_post, ffn_w_gu.astype(BF16), ffn_w_down.astype(BF16),
           ret_w_in.astype(BF16), ret_w_out.astype(BF16),
           jax.vmap(_gdn_in_weight)(gdn_w_in).astype(BF16),
           gdn_conv_w, gdn_a_log, gdn_dt_bias, gdn_norm_w, gdn_w_out.astype(BF16))

    cs_rows = c_sample
    cp_rows = jnp.pad(c_prompt, ((0, 2 * SUBLANES - bp), (0, 0)))
    mods = [_ada(cs_rows, cp_rows, w_ada, b_ada, i) for i in range(w_ada.shape[0])]

    y_p, ret_p, gdn_p, conv_p = _trunk(
        x_prompt.reshape(bp * lp, d), lambda i: mods[i][1], False, bp, lp, 0,
        None, None, None, wts, tm=1024)
    y_s, ret_s, gdn_s, conv_s = _trunk(
        x_sample.reshape(bs * ls, d), lambda i: mods[i][0], True, bs, ls, PAST_LEN,
        state_ret, state_gdn, state_conv, wts, tm=512)
    return (y_p.reshape(bp, lp, d), y_s.reshape(bs, ls, d), ret_p, ret_s, gdn_p, gdn_s, conv_p, conv_s)
```

```python
import functools
import math

import jax
import jax.numpy as jnp
from jax import lax
from jax.experimental import pallas as pl
from jax.experimental.pallas import tpu as pltpu

F32 = jnp.float32
BF16 = jnp.bfloat16
HIGHEST = lax.Precision.HIGHEST

EPS = 1e-6
ROPE_BASE = 10000.0
FFN_RES = 0.5
N_SUB = 3
CONV_W = 4
PAST_LEN = 16384

RET_HEADS = 8
RET_DK = 256
RET_DV = 512
GDN_K_HEADS = 16
GDN_V_HEADS = 32
GDN_DK = 128
GDN_DV = 128
GDN_QK = GDN_K_HEADS * GDN_DK
GDN_V = GDN_V_HEADS * GDN_DV
GDN_CONV_CH = 2 * GDN_QK + GDN_V

LANES = 128
SUBLANES = 8
VMEM_LIMIT = 56 * 1024 * 1024
COL_TILE = 1024
GLU_COL_TILE = 512


def _col_tile(n, cap):
    return max(t for t in range(LANES, cap + 1, LANES) if n % t == 0)


def _params(sem):
    return pltpu.CompilerParams(dimension_semantics=sem, vmem_limit_bytes=VMEM_LIMIT)


def _silu(x):
    return x * jax.nn.sigmoid(x)


def _dot(a, b):
    return jnp.dot(a.astype(BF16), b.astype(BF16), preferred_element_type=F32)


def _dot_nt(a, b):
    return lax.dot_general(a.astype(BF16), b.astype(BF16), (((1,), (1,)), ((), ())),
                           preferred_element_type=F32)


def _dot_tn(a, b):
    return lax.dot_general(a.astype(BF16), b.astype(BF16), (((0,), (0,)), ((), ())),
                           preferred_element_type=F32)


def _ada_kernel(cs_ref, cp_ref, w_ref, b_ref, os_ref, op_ref, as_scr, ap_scr):
    @pl.when(pl.program_id(0) == 0)
    def _():
        as_scr[...] = _silu(cs_ref[...]).astype(BF16)
        ap_scr[...] = _silu(cp_ref[...]).astype(BF16)

    w = w_ref[...].astype(BF16)
    b = b_ref[...]
    os_ref[...] = jnp.dot(as_scr[...], w, preferred_element_type=F32) + b
    op_ref[...] = jnp.dot(ap_scr[...], w, preferred_element_type=F32) + b


def _ada(cs, cp, w, b, layer, tn=COL_TILE):
    ms, d = cs.shape
    mp = cp.shape[0]
    depth, _, n = w.shape
    return pl.pallas_call(
        _ada_kernel,
        grid=(n // tn,),
        in_specs=[pl.BlockSpec((ms, d), lambda j: (0, 0)),
                  pl.BlockSpec((mp, d), lambda j: (0, 0)),
                  pl.BlockSpec((None, d, tn), lambda j: (layer, 0, j)),
                  pl.BlockSpec((None, 1, tn), lambda j: (layer, 0, j))],
        out_specs=[pl.BlockSpec((ms, tn), lambda j: (0, j)),
                   pl.BlockSpec((mp, tn), lambda j: (0, j))],
        out_shape=[jax.ShapeDtypeStruct((ms, n), F32), jax.ShapeDtypeStruct((mp, n), F32)],
        scratch_shapes=[pltpu.VMEM((ms, d), BF16), pltpu.VMEM((mp, d), BF16)],
        compiler_params=_params(("arbitrary",)),
        name="ada_table",
    )(cs, cp, w, b.reshape(depth, 1, n))


ROW_CHUNK = 128


def _mod_rows(ref, per_row, tiles_per_batch, chunk=None):
    if per_row:
        if chunk is None:
            return jnp.repeat(ref[...], per_row, axis=0)
        n = ROW_CHUNK // per_row
        return jnp.repeat(ref[pl.ds(pl.multiple_of(chunk * n, n), n), :], per_row, axis=0)
    return ref[pl.ds(pl.program_id(0) // tiles_per_batch, 1), :]


def _mod_mm_kernel(x_ref, sh_ref, sc_ref, nw_ref, *rest, glu, per_row, tiles_per_batch):
    if glu:
        wg_ref, wu_ref, o_ref, h_scr = rest
    else:
        w_ref, o_ref, h_scr = rest

    @pl.when(pl.program_id(1) == 0)
    def _():
        nw = nw_ref[...]

        def chunk(r, carry):
            start = pl.multiple_of(r * ROW_CHUNK, ROW_CHUNK)
            x = x_ref[pl.ds(start, ROW_CHUNK), :]
            y = x * lax.rsqrt(jnp.mean(x * x, -1, keepdims=True) + EPS) * nw
            sc = _mod_rows(sc_ref, per_row, tiles_per_batch, r)
            sh = _mod_rows(sh_ref, per_row, tiles_per_batch, r)
            h_scr[pl.ds(start, ROW_CHUNK), :] = (y * (1.0 + sc) + sh).astype(BF16)
            return carry

        lax.fori_loop(0, x_ref.shape[0] // ROW_CHUNK, chunk, 0)

    h = h_scr[...]
    if glu:
        g = jnp.dot(h, wg_ref[...], preferred_element_type=F32)
        u = jnp.dot(h, wu_ref[...], preferred_element_type=F32)
        o_ref[...] = (_silu(g) * u).astype(o_ref.dtype)
    else:
        o_ref[...] = jnp.dot(h, w_ref[...], preferred_element_type=F32).astype(o_ref.dtype)


def _mod_mm(x, mod, sub, nw, w, widx, *, glu, per_row, rows_per_batch, out_dtype, tm, tn):
    t, d = x.shape
    n = w.shape[-1] // 2 if glu else w.shape[-1]
    nj = n // tn
    lead = (None,) * len(widx)
    tiles_per_batch = max(rows_per_batch // tm, 1)
    per_row = rows_per_batch if per_row else 0
    mrows = tm // rows_per_batch if per_row else mod.shape[0]

    def mod_spec(c):
        col = sub * 3 + c
        if per_row:
            return pl.BlockSpec((mrows, d), lambda i, j: (i, col))
        return pl.BlockSpec((mrows, d), lambda i, j: (0, col))

    in_specs = [pl.BlockSpec((tm, d), lambda i, j: (i, 0)),
                mod_spec(0), mod_spec(1),
                pl.BlockSpec((1, d), lambda i, j: (0, 0))]
    args = [x, mod, mod, nw.reshape(1, d)]
    if glu:
        in_specs += [pl.BlockSpec(lead + (d, tn), lambda i, j: widx + (0, j)),
                     pl.BlockSpec(lead + (d, tn), lambda i, j: widx + (0, j + nj))]
        args += [w, w]
    else:
        in_specs += [pl.BlockSpec(lead + (d, tn), lambda i, j: widx + (0, j))]
        args += [w]
    return pl.pallas_call(
        functools.partial(_mod_mm_kernel, glu=glu, per_row=per_row, tiles_per_batch=tiles_per_batch),
        grid=(t // tm, nj),
        in_specs=in_specs,
        out_specs=pl.BlockSpec((tm, tn), lambda i, j: (i, j)),
        out_shape=jax.ShapeDtypeStruct((t, n), out_dtype),
        scratch_shapes=[pltpu.VMEM((tm, d), BF16)],
        compiler_params=_params(("parallel", "arbitrary")),
        name="mod_mm_glu" if glu else "mod_mm",
    )(*args)


def _mm_out_kernel(a_ref, w_ref, x_ref, g_ref, nw_ref, o_ref, *, res_scale, per_row, tiles_per_batch):
    y = jnp.dot(a_ref[...].astype(BF16), w_ref[...], preferred_element_type=F32)
    yn = y * lax.rsqrt(jnp.mean(y * y, -1, keepdims=True) + EPS) * nw_ref[...]
    gate = _mod_rows(g_ref, per_row, tiles_per_batch)
    o_ref[...] = x_ref[...] + res_scale * (gate * yn)


def _mm_out(a, w, widx, x, mod, sub, nw, *, res_scale, per_row, rows_per_batch, tm):
    t, kdim = a.shape
    d = w.shape[-1]
    lead = (None,) * len(widx)
    tiles_per_batch = max(rows_per_batch // tm, 1)
    per_row = rows_per_batch if per_row else 0
    col = sub * 3 + 2
    if per_row:
        g_spec = pl.BlockSpec((tm // rows_per_batch, d), lambda i: (i, col))
    else:
        g_spec = pl.BlockSpec((mod.shape[0], d), lambda i: (0, col))
    return pl.pallas_call(
        functools.partial(_mm_out_kernel, res_scale=res_scale, per_row=per_row,
                          tiles_per_batch=tiles_per_batch),
        grid=(t // tm,),
        in_specs=[pl.BlockSpec((tm, kdim), lambda i: (i, 0)),
                  pl.BlockSpec(lead + (kdim, d), lambda i: widx + (0, 0), pipeline_mode=pl.Buffered(1)),
                  pl.BlockSpec((tm, d), lambda i: (i, 0)),
                  g_spec,
                  pl.BlockSpec((1, d), lambda i: (0, 0))],
        out_specs=pl.BlockSpec((tm, d), lambda i: (i, 0)),
        out_shape=jax.ShapeDtypeStruct((t, d), F32),
        compiler_params=_params(("parallel",)),
        name="mm_out",
    )(a, w, x, mod, nw.reshape(1, d))


def _rotate(x, cos, sin):
    half = x.shape[-1] // 2
    x1, x2 = x[:, :half], x[:, half:]
    return jnp.concatenate([x1 * cos - x2 * sin, x1 * sin + x2 * cos], axis=-1)


def _ret_kernel(dm_ref, qd_ref, kd_ref, cd_ref, cos_ref, sin_ref, q_ref, k_ref, v_ref, g_ref, *rest,
                heads, has_state):
    if has_state:
        s0_ref, o_ref, sout_ref, s_scr = rest
    else:
        o_ref, sout_ref, s_scr = rest
    n = pl.program_id(2)
    dk, dv = RET_DK, RET_DV

    @pl.when(n == 0)
    def _():
        if has_state:
            s_scr[...] = s0_ref[...]
        else:
            s_scr[...] = jnp.zeros_like(s_scr)

    cos, sin = cos_ref[...], sin_ref[...]
    hs = range(heads)
    q = [_rotate(q_ref[:, h * dk:(h + 1) * dk], cos, sin) * (dk ** -0.5) for h in hs]
    k = [_rotate(k_ref[:, h * dk:(h + 1) * dk], cos, sin) for h in hs]
    v = [v_ref[:, h * dv:(h + 1) * dv].astype(BF16) for h in hs]
    s = [s_scr[h] for h in hs]
    scores = [_dot_nt(q[h], k[h]) * dm_ref[h] for h in hs]
    cross = [_dot(q[h], s[h]) for h in hs]
    inner = [_dot(scores[h], v[h]) for h in hs]
    upd = [_dot_tn(k[h] * jnp.concatenate([kd_ref[h]] * (dk // LANES), axis=1), v[h]) for h in hs]
    for h in hs:
        s_new = s[h] * cd_ref[h, 0:1, 0:1] + upd[h]
        s_scr[h] = s_new
        o = inner[h] + cross[h] * jnp.concatenate([qd_ref[h]] * (dv // LANES), axis=1)
        on = o * lax.rsqrt(jnp.mean(o * o, -1, keepdims=True) + EPS)
        o_ref[:, h * dv:(h + 1) * dv] = (_silu(g_ref[:, h * dv:(h + 1) * dv]) * on).astype(o_ref.dtype)

    @pl.when(n == pl.num_programs(2) - 1)
    def _():
        sout_ref[...] = s_scr[...]


def _retention(proj, s0, pos0, batch, seq, c, heads):
    nh, dk, dv = RET_HEADS, RET_DK, RET_DV
    nc = seq // c
    ng = nh // heads
    half = dk // 2
    log_g = jnp.log1p(-jnp.exp2(-5.0 - jnp.arange(nh, dtype=F32)))
    idx = jnp.arange(c, dtype=F32)
    diff = idx[:, None] - idx[None, :]
    causal = diff >= 0
    dmask = jnp.where(causal[None], jnp.exp(log_g[:, None, None] * jnp.where(causal, diff, 0.0)[None]), 0.0)
    lanes = lambda x: jnp.broadcast_to(x[:, :, None], (nh, x.shape[1], LANES))
    q_decay = lanes(jnp.exp(log_g[:, None] * (idx[None, :] + 1.0)))
    k_decay = lanes(jnp.exp(log_g[:, None] * (c - 1.0 - idx)[None, :]))
    chunk_decay = jnp.broadcast_to(jnp.exp(log_g * c)[:, None, None], (nh, SUBLANES, LANES))
    inv = 1.0 / (ROPE_BASE ** jnp.linspace(0.0, 1.0, half, dtype=F32))
    ang = (pos0 + jnp.arange(seq)).astype(F32)[:, None] * inv[None, :]
    cos, sin = jnp.cos(ang), jnp.sin(ang)
    has_state = s0 is not None
    tokb = lambda b, g, n: b * nc + n
    in_specs = [pl.BlockSpec((heads, c, c), lambda b, g, n: (g, 0, 0)),
                pl.BlockSpec((heads, c, LANES), lambda b, g, n: (g, 0, 0)),
                pl.BlockSpec((heads, c, LANES), lambda b, g, n: (g, 0, 0)),
                pl.BlockSpec((heads, SUBLANES, LANES), lambda b, g, n: (g, 0, 0)),
                pl.BlockSpec((c, half), lambda b, g, n: (n, 0)),
                pl.BlockSpec((c, half), lambda b, g, n: (n, 0)),
                pl.BlockSpec((c, heads * dk), lambda b, g, n: (tokb(b, g, n), g)),
                pl.BlockSpec((c, heads * dk), lambda b, g, n: (tokb(b, g, n), ng + g)),
                pl.BlockSpec((c, heads * dv), lambda b, g, n: (tokb(b, g, n), ng + g)),
                pl.BlockSpec((c, heads * dv), lambda b, g, n: (tokb(b, g, n), 2 * ng + g))]
    args = [dmask, q_decay, k_decay, chunk_decay, cos, sin, proj, proj, proj, proj]
    if has_state:
        in_specs.append(pl.BlockSpec((None, heads, dk, dv), lambda b, g, n: (b, g, 0, 0)))
        args.append(s0)
    return pl.pallas_call(
        functools.partial(_ret_kernel, heads=heads, has_state=has_state),
        grid=(batch, ng, nc),
        in_specs=in_specs,
        out_specs=[pl.BlockSpec((c, heads * dv), lambda b, g, n: (tokb(b, g, n), g)),
                   pl.BlockSpec((None, heads, dk, dv), lambda b, g, n: (b, g, 0, 0))],
        out_shape=[jax.ShapeDtypeStruct((batch * seq, nh * dv), BF16 if c % (2 * SUBLANES) == 0 else F32),
                   jax.ShapeDtypeStruct((batch, nh, dk, dv), F32)],
        scratch_shapes=[pltpu.VMEM((heads, dk, dv), F32)],
        compiler_params=_params(("parallel", "parallel", "arbitrary")),
        name="retention",
    )(*args)


def _gdn_in_weight(w_in):
    d = w_in.shape[0]
    ab = w_in[:, GDN_CONV_CH + GDN_V:]
    a = ab[:, :GDN_V_HEADS].reshape(d, GDN_K_HEADS, 2)
    b = ab[:, GDN_V_HEADS:].reshape(d, GDN_K_HEADS, 2)
    tail = jnp.concatenate([a, b, jnp.zeros((d, GDN_K_HEADS, LANES - 4), w_in.dtype)], axis=-1)
    return jnp.concatenate([w_in[:, :GDN_CONV_CH + GDN_V].astype(BF16),
                            tail.reshape(d, GDN_K_HEADS * LANES).astype(BF16)], axis=-1)


def _gdn_head_tables(a_log, dt_bias):
    def tab(x):
        row = jnp.concatenate([x.reshape(GDN_K_HEADS, 2), jnp.zeros((GDN_K_HEADS, LANES - 2), F32)], -1)
        return jnp.broadcast_to(row[:, None, :], (GDN_K_HEADS, SUBLANES, LANES))
    return tab(a_log), tab(dt_bias)


GDN_C = 64
GDN_SUB = 256


def _chunk_cumsum(x, c):
    pos = lax.broadcasted_iota(jnp.int32, x.shape, 0) % c
    s = 1
    while s < c:
        x = x + jnp.where(pos >= s, pltpu.roll(x, s, axis=0), 0.0)
        s *= 2
    return x


GDN_BASE = 8


def _inverse_masks(c, width):
    i = jnp.arange(c)[:, None]
    j = (jnp.arange(width) % c)[None, :]
    r = (jnp.arange(width) % (2 * c))[:, None]
    s = jnp.arange(2 * c)[None, :]
    lanes = [(i // GDN_BASE) == (j // GDN_BASE)]
    pair = [(r // c) == (s // c), (r // GDN_BASE) == (s // GDN_BASE)]
    b = GDN_BASE
    while b < c:
        lanes.append(((i // (2 * b)) == (j // (2 * b))) & ((i // b) % 2 == 1) & ((j // b) % 2 == 0))
        pair.append(((r // (2 * b)) == (s // (2 * b))) & ((r // b) % 2 == 1) & ((s // b) % 2 == 0))
        b *= 2
    causal = jnp.stack([pair[0] & (r >= s), pair[0] & (r > s)]).astype(F32)
    return jnp.stack(lanes).astype(F32), jnp.stack(pair).astype(BF16), causal


def _mask_dict(lanes_ref, pair_ref):
    masks = {"base_lanes": lanes_ref[0], "same_pair": pair_ref[0], "base_pair": pair_ref[1]}
    for n in range(1, lanes_ref.shape[0]):
        masks[f"lanes{GDN_BASE << (n - 1)}"] = lanes_ref[n]
        masks[f"pair{GDN_BASE << (n - 1)}"] = pair_ref[n + 1]
    return masks


def _pair_to_diag(p):
    width, pc = p.shape
    n = width // pc
    zero = jnp.zeros((pc, pc), p.dtype)
    return jnp.concatenate(
        [jnp.concatenate([p[i * pc:(i + 1) * pc] if j == i else zero for j in range(n)], axis=1)
         for i in range(n)], axis=0)


def _lanes_to_diag(p, masks):
    c, width = p.shape
    pb = p.astype(BF16)
    pair = jnp.concatenate([jnp.concatenate([pb[:, i * 2 * c:(i + 1) * 2 * c]] * 2, axis=0)
                            for i in range(width // (2 * c))], axis=0)
    return _pair_to_diag(pair * masks["same_pair"])


def _inverse_minus_eye_lanes(l_lbs, l_pairs, masks):
    c = l_lbs[0].shape[0]
    ps = range(len(l_lbs))
    mm = lambda a, b: jnp.dot(a.astype(BF16), b, preferred_element_type=F32)

    lb = [l_lbs[i] * masks["base_lanes"] for i in ps]
    x = [-lb[i] for i in ps]
    p = [mm(lb[i], _pair_to_diag(l_pairs[i] * masks["base_pair"])) for i in ps]
    n = 2
    while 2 * n < GDN_BASE:
        xp = [mm(jnp.concatenate([x[i], p[i]], axis=0), _lanes_to_diag(p[i], masks)) for i in ps]
        x = [x[i] + p[i] + xp[i][:c] for i in ps]
        p = [xp[i][c:] for i in ps]
        n *= 2
    xp = [mm(x[i], _lanes_to_diag(p[i], masks)) for i in ps]
    x = [x[i] + p[i] + xp[i] for i in ps]

    b = GDN_BASE
    while b < c:
        t = [l_lbs[i] * masks[f"lanes{b}"]
             + mm(x[i], _pair_to_diag(l_pairs[i] * masks[f"pair{b}"])) for i in ps]
        tx = [mm(t[i], _lanes_to_diag(x[i], masks)) for i in ps]
        x = [x[i] - t[i] - tx[i] for i in ps]
        b *= 2
    return x


def _gdn_prep_kernel(q_ref, k_ref, v_ref, ab_ref, qp_ref, kp_ref, vp_ref, wq_ref, wk_ref, wv_ref,
                     alog_ref, dt_ref, ml_ref, mp_ref, mc_ref, u_ref, w_ref, qg_ref, kgt_ref, at_ref, eg_ref,
                     bq, bk, bv, gc_scr, *, tb):
    c, sub = GDN_C, GDN_SUB
    first = pl.program_id(1) == 0

    def conv_silu(buf, prev_ref, x_ref, cw_ref):
        buf[pl.ds(0, SUBLANES), :] = jnp.where(first, 0.0, prev_ref[...])
        buf[pl.ds(SUBLANES, tb), :] = x_ref[...]
        cw = cw_ref[...]
        out = buf[pl.ds(SUBLANES - 3, tb), :] * cw[0:1, :]
        for j in range(1, CONV_W):
            out = out + buf[pl.ds(SUBLANES - 3 + j, tb), :] * cw[j:j + 1, :]
        return _silu(out)

    q = conv_silu(bq, qp_ref, q_ref, wq_ref)
    k = conv_silu(bk, kp_ref, k_ref, wk_ref)
    v2 = conv_silu(bv, vp_ref, v_ref, wv_ref)
    q = q * lax.rsqrt(jnp.sum(q * q, -1, keepdims=True) + EPS) * (GDN_DK ** -0.5)
    k = k * lax.rsqrt(jnp.sum(k * k, -1, keepdims=True) + EPS)

    ab = ab_ref[...]
    g_all = -jnp.exp(alog_ref[0:1, :]) * jax.nn.softplus(ab + dt_ref[0:1, :])
    beta_all = jax.nn.sigmoid(ab)
    gc_all = _chunk_cumsum(g_all, c)
    gc_t = gc_all.T

    masks = _mask_dict(ml_ref, mp_ref)
    tri = mc_ref[0] > 0.0
    strict = mc_ref[1] > 0.0

    nh, nchunk, pc = tb // sub, sub // c, 2 * c
    rows = [slice(hf * sub, (hf + 1) * sub) for hf in range(nh)]
    pairs = [slice(i * pc, (i + 1) * pc) for i in range(tb // pc)]
    kq = [_dot_nt(jnp.concatenate([k[pr], q[pr]], axis=0), k[pr]) for pr in pairs]
    npair = sub // pc
    kk = [jnp.concatenate([kq[hf * npair + i][:pc] for i in range(npair)], axis=0) for hf in range(nh)]
    qk = [jnp.concatenate([kq[hf * npair + i][pc:] for i in range(npair)], axis=0) for hf in range(nh)]
    probs = [(hf, e) for hf in range(nh) for e in range(2)]
    gcb, betab, gl, l_pair, l_lb = [], [], [], [], []
    for p, (hf, e) in enumerate(probs):
        lanes = slice(e * GDN_DV, (e + 1) * GDN_DV)
        gcb.append(jnp.broadcast_to(gc_all[rows[hf], e:e + 1], (sub, LANES)))
        betab.append(jnp.broadcast_to(beta_all[rows[hf], 2 + e:3 + e], (sub, LANES)))
        gc_scr[p] = gcb[p]
        gl.append(gc_scr[p, pl.ds(c - 1, nchunk, stride=c), :])
        gc_cols = jnp.concatenate(
            [jnp.broadcast_to(gc_t[e:e + 1, hf * sub + i * pc:hf * sub + (i + 1) * pc], (pc, pc))
             for i in range(npair)], axis=0)
        decay = jnp.where(tri, jnp.exp(jnp.where(tri, gcb[p] - gc_cols, 0.0)), 0.0)
        lp = jnp.where(strict, kk[hf] * betab[p] * decay, 0.0)
        at_ref[rows[hf], lanes] = (qk[hf] * decay).astype(BF16)
        l_pair.append(lp.astype(BF16))
        l_lb.append(jnp.concatenate([lp[2 * i * c:(2 * i + 1) * c] + lp[(2 * i + 1) * c:(2 * i + 2) * c]
                                     for i in range(npair)], axis=1))
    x_lb = _inverse_minus_eye_lanes(l_lb, l_pair, masks)
    rhs = [jnp.concatenate([v2[rows[hf], e * GDN_DV:(e + 1) * GDN_DV] * betab[p],
                            k[rows[hf]] * betab[p] * jnp.exp(gcb[p])], axis=1)
           for p, (hf, e) in enumerate(probs)]
    corr = [jnp.dot(_lanes_to_diag(x_lb[p], masks), rhs[p].astype(BF16), preferred_element_type=F32)
            for p in range(len(probs))]
    for p, (hf, e) in enumerate(probs):
        lanes = slice(e * GDN_DV, (e + 1) * GDN_DV)
        sol = rhs[p] + corr[p]
        u_ref[rows[hf], lanes] = sol[:, :GDN_DV]
        w_ref[rows[hf], lanes] = sol[:, GDN_DV:].astype(BF16)
        qg_ref[rows[hf], lanes] = (q[rows[hf]] * jnp.exp(gcb[p])).astype(BF16)
        gl_b = jnp.concatenate([jnp.broadcast_to(gl[p][i:i + 1, :], (c, LANES)) for i in range(nchunk)], axis=0)
        kg = k[rows[hf]] * jnp.exp(gl_b - gcb[p])
        kgt_ref[rows[hf], lanes] = jnp.concatenate(
            [kg[i * pc:(i + 1) * pc].T for i in range(npair)], axis=0).astype(BF16)
        eg_ref[hf * nchunk:(hf + 1) * nchunk, lanes] = jnp.exp(gl[p])


def _gdn_scan_kernel(u_ref, w_ref, qg_ref, kgt_ref, at_ref, eg_ref, z_ref, nw_ref, o_ref, sout_ref, s_scr,
                     *, tb, heads):
    c = GDN_C

    @pl.when(pl.program_id(2) == 0)
    def _():
        s_scr[...] = jnp.zeros_like(s_scr)

    nw = nw_ref[...]
    hl = [slice(h * GDN_DV, (h + 1) * GDN_DV) for h in range(heads)]
    s = [s_scr[h] for h in range(heads)]
    zeros = jnp.zeros((c, GDN_DV), BF16)
    for i in range(tb // c):
        rows = slice(i * c, (i + 1) * c)
        pair_rows = slice((i // 2) * 2 * c, (i // 2 + 1) * 2 * c)
        ws = [jnp.dot(jnp.concatenate([w_ref[rows, hl[h]], qg_ref[rows, hl[h]]], axis=0),
                      s[h].astype(BF16), preferred_element_type=F32) for h in range(heads)]
        v_new = [(u_ref[rows, hl[h]] - ws[h][:c]).astype(BF16) for h in range(heads)]
        pair = [jnp.concatenate([v_new[h], zeros] if i % 2 == 0 else [zeros, v_new[h]], axis=0)
                for h in range(heads)]
        for h in range(heads):
            o = ws[h][c:] + jnp.dot(at_ref[rows, hl[h]], pair[h], preferred_element_type=F32)
            s[h] = s[h] * eg_ref[i:i + 1, hl[h]] + jnp.dot(kgt_ref[pair_rows, hl[h]], pair[h],
                                                          preferred_element_type=F32)
            on = o * lax.rsqrt(jnp.mean(o * o, -1, keepdims=True) + EPS) * nw
            o_ref[rows, hl[h]] = (on * _silu(z_ref[rows, hl[h]])).astype(o_ref.dtype)
    for h in range(heads):
        s_scr[h] = s[h]

    @pl.when(pl.program_id(2) == pl.num_programs(2) - 1)
    def _():
        sout_ref[...] = s_scr[...]


def _gdn_prompt(proj, conv_w, a_log, dt_bias, norm_w, batch, seq, tb_prep=1024, tb_scan=512, heads=8):
    hk, dk, dv = GDN_K_HEADS, GDN_DK, GDN_DV
    assert seq % tb_prep == 0 and seq % tb_scan == 0
    t = batch * seq
    tb = tb_prep
    nt = seq // tb
    kq0 = GDN_QK // dk
    v0 = 2 * GDN_QK // (2 * dv)
    ab0 = (GDN_CONV_CH + GDN_V) // LANES
    alog_tab, dt_tab = _gdn_head_tables(a_log, dt_bias)
    m_lanes, m_pair, m_causal = _inverse_masks(GDN_C, GDN_SUB)
    tok = lambda b, n, j: b * nt + n
    prev = lambda b, n, j: jnp.maximum((b * seq + n * tb) // SUBLANES - 1, 0)
    u, w, qg, kg, at, eg = pl.pallas_call(
        functools.partial(_gdn_prep_kernel, tb=tb),
        grid=(batch, nt, hk),
        in_specs=[pl.BlockSpec((tb, dk), lambda b, n, j: (tok(b, n, j), j)),
                  pl.BlockSpec((tb, dk), lambda b, n, j: (tok(b, n, j), kq0 + j)),
                  pl.BlockSpec((tb, 2 * dv), lambda b, n, j: (tok(b, n, j), v0 + j)),
                  pl.BlockSpec((tb, LANES), lambda b, n, j: (tok(b, n, j), ab0 + j)),
                  pl.BlockSpec((SUBLANES, dk), lambda b, n, j: (prev(b, n, j), j)),
                  pl.BlockSpec((SUBLANES, dk), lambda b, n, j: (prev(b, n, j), kq0 + j)),
                  pl.BlockSpec((SUBLANES, 2 * dv), lambda b, n, j: (prev(b, n, j), v0 + j)),
                  pl.BlockSpec((CONV_W, dk), lambda b, n, j: (0, j)),
                  pl.BlockSpec((CONV_W, dk), lambda b, n, j: (0, kq0 + j)),
                  pl.BlockSpec((CONV_W, 2 * dv), lambda b, n, j: (0, v0 + j)),
                  pl.BlockSpec((None, SUBLANES, LANES), lambda b, n, j: (j, 0, 0)),
                  pl.BlockSpec((None, SUBLANES, LANES), lambda b, n, j: (j, 0, 0)),
                  pl.BlockSpec(m_lanes.shape, lambda b, n, j: (0, 0, 0)),
                  pl.BlockSpec(m_pair.shape, lambda b, n, j: (0, 0, 0)),
                  pl.BlockSpec(m_causal.shape, lambda b, n, j: (0, 0, 0))],
        out_specs=[pl.BlockSpec((tb, 2 * dv), lambda b, n, j: (tok(b, n, j), j))] * 5
        + [pl.BlockSpec((tb // GDN_C, 2 * dv), lambda b, n, j: (tok(b, n, j), j))],
        out_shape=[jax.ShapeDtypeStruct((t, GDN_V), F32)]
        + [jax.ShapeDtypeStruct((t, GDN_V), BF16)] * 4
        + [jax.ShapeDtypeStruct((t // GDN_C, GDN_V), F32)],
        scratch_shapes=[pltpu.VMEM((SUBLANES + tb, dk), F32),
                        pltpu.VMEM((SUBLANES + tb, dk), F32),
                        pltpu.VMEM((SUBLANES + tb, 2 * dv), F32),
                        pltpu.VMEM((2 * tb // GDN_SUB, GDN_SUB, LANES), F32)],
        compiler_params=_params(("parallel", "parallel", "parallel")),
        name="gdn_prep",
    )(proj, proj, proj, proj, proj, proj, proj, conv_w, conv_w, conv_w, alog_tab, dt_tab,
      m_lanes, m_pair, m_causal)

    tb = tb_scan
    nt = seq // tb
    wide = heads * dv
    z0 = GDN_CONV_CH // wide
    blk = lambda b, g, n: (b * nt + n, g)
    return pl.pallas_call(
        functools.partial(_gdn_scan_kernel, tb=tb, heads=heads),
        grid=(batch, GDN_V_HEADS // heads, nt),
        in_specs=[pl.BlockSpec((tb, wide), blk)] * 5
        + [pl.BlockSpec((tb // GDN_C, wide), blk),
           pl.BlockSpec((tb, wide), lambda b, g, n: (b * nt + n, z0 + g)),
           pl.BlockSpec((1, dv), lambda b, g, n: (0, 0))],
        out_specs=[pl.BlockSpec((tb, wide), blk),
                   pl.BlockSpec((None, heads, dk, dv), lambda b, g, n: (b, g, 0, 0))],
        out_shape=[jax.ShapeDtypeStruct((t, GDN_V), BF16),
                   jax.ShapeDtypeStruct((batch, GDN_V_HEADS, dk, dv), F32)],
        scratch_shapes=[pltpu.VMEM((heads, dk, dv), F32)],
        compiler_params=_params(("parallel", "parallel", "arbitrary")),
        name="gdn_scan",
    )(u, w, qg, kg, at, eg, proj, norm_w.reshape(1, dv))


def _gdn_step_kernel(q_ref, k_ref, v_ref, z_ref, ab_ref, cq_ref, ck_ref, cv_ref, wq_ref, wk_ref, wv_ref,
                     alog_ref, dt_ref, nw_ref, s0_ref, o_ref, sout_ref, cout_ref, bq, bk, bv, g_scr, *, seq, nb):
    hv, hk, dk, dv = GDN_V_HEADS, GDN_K_HEADS, GDN_DK, GDN_DV
    n = hv * seq
    rep = hv // hk
    es = range(nb)

    def conv_silu(e, buf, c_ref, x_ref, cw_ref, col):
        width = buf.shape[-1]
        buf[e, pl.ds(SUBLANES - 3, 3), :] = c_ref[e]
        buf[e, pl.ds(SUBLANES, seq), :] = x_ref[e * seq:(e + 1) * seq, :]
        cout_ref[e, :, col:col + width] = buf[e, pl.ds(SUBLANES + seq - 3, 3), :]
        cw = cw_ref[...]
        out = buf[e, pl.ds(SUBLANES - 3, seq), :] * cw[0:1, :]
        for j in range(1, CONV_W):
            out = out + buf[e, pl.ds(SUBLANES - 3 + j, seq), :] * cw[j:j + 1, :]
        return _silu(out)

    def stack(x, width):
        return jnp.concatenate([x[:, h * width:(h + 1) * width] for h in range(x.shape[1] // width)], axis=0)

    def per_v_head(x):
        return jnp.concatenate([x[(h // rep) * seq:(h // rep + 1) * seq] for h in range(hv)], axis=0)

    def unit(x):
        return x * lax.rsqrt(jnp.sum(x * x, -1, keepdims=True) + EPS)

    def per_head_rows(x_st, lane0):
        cols = [jnp.broadcast_to(x_st[:, lane0 + e:lane0 + e + 1], (hk * seq, LANES)) for e in range(rep)]
        return jnp.concatenate([cols[h % rep][(h // rep) * seq:(h // rep + 1) * seq] for h in range(hv)], axis=0)

    q = [per_v_head(unit(stack(conv_silu(e, bq, cq_ref, q_ref, wq_ref, 0), dk))) * (dk ** -0.5) for e in es]
    k = [per_v_head(unit(stack(conv_silu(e, bk, ck_ref, k_ref, wk_ref, GDN_QK), dk))) for e in es]
    v = [stack(conv_silu(e, bv, cv_ref, v_ref, wv_ref, 2 * GDN_QK), dv) for e in es]

    ab = [stack(ab_ref[e * seq:(e + 1) * seq, :], LANES) for e in es]
    gc_st = [_chunk_cumsum(-jnp.exp(alog_ref[...]) * jax.nn.softplus(ab[e] + dt_ref[...]), seq) for e in es]
    gcb = [per_head_rows(gc_st[e], 0) for e in es]
    betab = [per_head_rows(jax.nn.sigmoid(ab[e]), rep) for e in es]
    gc_row = [gcb[e].T[0:1, :] for e in es]
    gl = []
    for e in es:
        g_scr[e] = gcb[e]
        gl.append(g_scr[e, pl.ds(seq - 1, hv, stride=seq), :])
    eg = [jnp.exp(gl[e]) for e in es]

    r = lax.broadcasted_iota(jnp.int32, (n, n), 0)
    s = lax.broadcasted_iota(jnp.int32, (n, n), 1)
    same = (r // seq) == (s // seq)
    tri = same & (r >= s)
    strict = same & (r > s)
    wide = lambda x: jnp.concatenate([x] * (n // LANES), axis=1)
    decay = [jnp.where(tri, jnp.exp(jnp.where(tri, wide(gcb[e]) - gc_row[e], 0.0)), 0.0) for e in es]
    kq = [_dot_nt(jnp.concatenate([k[e], q[e]], axis=0), k[e]) for e in es]
    l_bd = [jnp.where(strict, kq[e][:n] * wide(betab[e]) * decay[e], 0.0) for e in es]
    a_bd = [kq[e][n:] * decay[e] for e in es]

    x = [-l_bd[e] for e in es]
    p = [_dot(l_bd[e], l_bd[e]) for e in es]
    m = 2
    while 2 * m < seq:
        xp = [_dot(jnp.concatenate([x[e], p[e]], axis=0), p[e]) for e in es]
        x = [x[e] + p[e] + xp[e][:n] for e in es]
        p = [xp[e][n:] for e in es]
        m *= 2
    xp = [_dot(x[e], p[e]) for e in es]
    x = [x[e] + p[e] + xp[e] for e in es]

    rhs = [jnp.concatenate([v[e] * betab[e], k[e] * betab[e] * jnp.exp(gcb[e])], axis=1) for e in es]
    corr = [_dot(x[e], rhs[e]) for e in es]
    sol = [rhs[e] + corr[e] for e in es]
    qg = [q[e] * jnp.exp(gcb[e]) for e in es]
    kg = []
    for e in es:
        gl_b = jnp.concatenate([jnp.broadcast_to(gl[e][h:h + 1, :], (seq, LANES)) for h in range(hv)], axis=0)
        kg.append(k[e] * jnp.exp(gl_b - gcb[e]))

    eh = [(e, h) for e in es for h in range(hv)]
    hrows = [slice(h * seq, (h + 1) * seq) for h in range(hv)]
    ws = {(e, h): _dot(jnp.concatenate([sol[e][hrows[h], dv:], qg[e][hrows[h]]], axis=0), s0_ref[e, h])
          for e, h in eh}
    v_new = {(e, h): sol[e][hrows[h], :dv] - ws[e, h][:seq] for e, h in eh}
    upd = {(e, h): _dot_tn(kg[e][hrows[h]], v_new[e, h]) for e, h in eh}
    for e, h in eh:
        sout_ref[e, h] = s0_ref[e, h] * eg[e][h:h + 1, :] + upd[e, h]
    inner = [_dot(a_bd[e], jnp.concatenate([v_new[e, h] for h in range(hv)], axis=0)) for e in es]
    for e in es:
        o = jnp.concatenate([ws[e, h][seq:] for h in range(hv)], axis=0) + inner[e]
        z = stack(z_ref[e * seq:(e + 1) * seq, :], dv)
        on = o * lax.rsqrt(jnp.mean(o * o, -1, keepdims=True) + EPS) * nw_ref[...] * _silu(z)
        for h in range(hv):
            o_ref[e * seq:(e + 1) * seq, h * dv:(h + 1) * dv] = on[h * seq:(h + 1) * seq]


def _gdn_step(proj, conv_w, a_log, dt_bias, norm_w, s0, buf0, batch, seq, nb=2):
    hv, hk, dk, dv = GDN_V_HEADS, GDN_K_HEADS, GDN_DK, GDN_DV
    assert seq == SUBLANES and batch % nb == 0
    qw, vw, abw = GDN_QK, GDN_V, hk * LANES
    alog_tab, dt_tab = _gdn_head_tables(a_log, dt_bias)
    alog_tab = alog_tab.reshape(hk * seq, LANES)
    dt_tab = dt_tab.reshape(hk * seq, LANES)
    return pl.pallas_call(
        functools.partial(_gdn_step_kernel, seq=seq, nb=nb),
        grid=(batch // nb,),
        in_specs=[pl.BlockSpec((nb * seq, qw), lambda b: (b, 0)),
                  pl.BlockSpec((nb * seq, qw), lambda b: (b, 1)),
                  pl.BlockSpec((nb * seq, vw), lambda b: (b, 2 * qw // vw)),
                  pl.BlockSpec((nb * seq, vw), lambda b: (b, GDN_CONV_CH // vw)),
                  pl.BlockSpec((nb * seq, abw), lambda b: (b, (GDN_CONV_CH + GDN_V) // abw)),
                  pl.BlockSpec((nb, CONV_W - 1, qw), lambda b: (b, 0, 0)),
                  pl.BlockSpec((nb, CONV_W - 1, qw), lambda b: (b, 0, 1)),
                  pl.BlockSpec((nb, CONV_W - 1, vw), lambda b: (b, 0, 2 * qw // vw)),
                  pl.BlockSpec((CONV_W, qw), lambda b: (0, 0)),
                  pl.BlockSpec((CONV_W, qw), lambda b: (0, 1)),
                  pl.BlockSpec((CONV_W, vw), lambda b: (0, 2 * qw // vw)),
                  pl.BlockSpec((hk * seq, LANES), lambda b: (0, 0)),
                  pl.BlockSpec((hk * seq, LANES), lambda b: (0, 0)),
                  pl.BlockSpec((1, dv), lambda b: (0, 0)),
                  pl.BlockSpec((nb, hv, dk, dv), lambda b: (b, 0, 0, 0))],
        out_specs=[pl.BlockSpec((nb * seq, vw), lambda b: (b, 0)),
                   pl.BlockSpec((nb, hv, dk, dv), lambda b: (b, 0, 0, 0)),
                   pl.BlockSpec((nb, CONV_W - 1, GDN_CONV_CH), lambda b: (b, 0, 0))],
        out_shape=[jax.ShapeDtypeStruct((batch * seq, GDN_V), F32),
                   jax.ShapeDtypeStruct((batch, hv, dk, dv), F32),
                   jax.ShapeDtypeStruct((batch, CONV_W - 1, GDN_CONV_CH), F32)],
        scratch_shapes=[pltpu.VMEM((nb, 2 * SUBLANES, qw), F32),
                        pltpu.VMEM((nb, 2 * SUBLANES, qw), F32),
                        pltpu.VMEM((nb, 2 * SUBLANES, vw), F32),
                        pltpu.VMEM((nb, hv * seq, LANES), F32)],
        compiler_params=_params(("parallel",)),
        name="gdn_step",
    )(proj, proj, proj, proj, proj, buf0, buf0, buf0, conv_w, conv_w, conv_w, alog_tab, dt_tab,
      norm_w.reshape(1, dv), s0)


def _trunk(x, mod_fn, per_row, batch, seq, pos0, ret_state, gdn_state, conv_state, wts, tm):
    (norm_pre, norm_post, w_gu, w_down, ret_w_in, ret_w_out, gdn_w_in, gdn_conv_w, gdn_a_log,
     gdn_dt_bias, gdn_norm_w, gdn_w_out) = wts
    kw = dict(per_row=per_row, rows_per_batch=seq)
    depth = norm_pre.shape[0]
    new_ret, new_gdn, new_conv = [], [], []
    for i in range(depth):
        mod = mod_fn(i)

        def ffn(x, sub, f):
            a = _mod_mm(x, mod, sub, norm_pre[i, sub], w_gu, (i, f), glu=True, out_dtype=BF16,
                        tm=tm, tn=_col_tile(w_gu.shape[-1] // 2, GLU_COL_TILE), **kw)
            return _mm_out(a, w_down, (i, f), x, mod, sub, norm_post[i, sub], res_scale=FFN_RES,
                           tm=256, **kw)

        x = ffn(x, 0, 0)
        r = i // 2
        if i % 2 == 0:
            proj = _mod_mm(x, mod, 1, norm_pre[i, 1], ret_w_in, (r,), glu=False, out_dtype=F32,
                           tm=tm, tn=_col_tile(ret_w_in.shape[-1], COL_TILE), **kw)
            y, s = _retention(proj, None if ret_state is None else ret_state[r], pos0, batch, seq,
                              c=math.gcd(256, seq), heads=4 if seq >= 256 else RET_HEADS)
            new_ret.append(s)
            x = _mm_out(y, ret_w_out, (r,), x, mod, 1, norm_post[i, 1], res_scale=1.0,
                        tm=256, **kw)
        else:
            proj = _mod_mm(x, mod, 1, norm_pre[i, 1], gdn_w_in, (r,), glu=False, out_dtype=F32,
                           tm=tm, tn=_col_tile(gdn_w_in.shape[-1], COL_TILE), **kw)
            if gdn_state is None:
                y, s = _gdn_prompt(proj, gdn_conv_w[r], gdn_a_log[r], gdn_dt_bias[r], gdn_norm_w[r],
                                   batch, seq)
                cs = proj.reshape(batch, seq, -1)[:, seq - (CONV_W - 1):, :GDN_CONV_CH]
            else:
                y, s, cs = _gdn_step(proj, gdn_conv_w[r], gdn_a_log[r], gdn_dt_bias[r], gdn_norm_w[r],
                                     gdn_state[r], conv_state[r], batch, seq)
            new_gdn.append(s)
            new_conv.append(cs)
            x = _mm_out(y, gdn_w_out, (r,), x, mod, 1, norm_post[i, 1], res_scale=1.0,
                        tm=256, **kw)
        x = ffn(x, 2, 1)
    stack = lambda xs: xs[0][None] if len(xs) == 1 else jnp.stack(xs)
    return x, stack(new_ret), stack(new_gdn), stack(new_conv)


def kernel(x_prompt, x_sample, c_prompt, c_sample, state_ret, state_gdn, state_conv, w_ada, b_ada,
           norm_pre, norm_post, ffn_w_gu, ffn_w_down, ret_w_in, ret_w_out, gdn_w_in, gdn_conv_w,
           gdn_a_log, gdn_dt_bias, gdn_norm_w, gdn_w_out):
    bp, lp, d = x_prompt.shape
    bs, ls, _ = x_sample.shape
    wts = (norm_pre, norm_post, ffn_w_gu.astype(BF16), ffn_w_down.astype(BF16),
           ret_w_in.astype(BF16), ret_w_out.astype(BF16),
           jax.vmap(_gdn_in_weight)(gdn_w_in).astype(BF16),
           gdn_conv_w, gdn_a_log, gdn_dt_bias, gdn_norm_w, gdn_w_out.astype(BF16))

    cs_rows = c_sample
    cp_rows = jnp.pad(c_prompt, ((0, 2 * SUBLANES - bp), (0, 0)))
    mods = [_ada(cs_rows, cp_rows, w_ada, b_ada, i) for i in range(w_ada.shape[0])]

    y_p, ret_p, gdn_p, conv_p = _trunk(
        x_prompt.reshape(bp * lp, d), lambda i: mods[i][1], False, bp, lp, 0,
        None, None, None, wts, tm=1024)
    y_s, ret_s, gdn_s, conv_s = _trunk(
        x_sample.reshape(bs * ls, d), lambda i: mods[i][0], True, bs, ls, PAST_LEN,
        state_ret, state_gdn, state_conv, wts, tm=1024)
    return (y_p.reshape(bp, lp, d), y_s.reshape(bs, ls, d), ret_p, ret_s, gdn_p, gdn_s, conv_p, conv_s)
```

```python
import functools
import math

import jax
import jax.numpy as jnp
from jax import lax
from jax.experimental import pallas as pl
from jax.experimental.pallas import tpu as pltpu

F32 = jnp.float32
BF16 = jnp.bfloat16
HIGHEST = lax.Precision.HIGHEST

EPS = 1e-6
ROPE_BASE = 10000.0
FFN_RES = 0.5
N_SUB = 3
CONV_W = 4
PAST_LEN = 16384

RET_HEADS = 8
RET_DK = 256
RET_DV = 512
GDN_K_HEADS = 16
GDN_V_HEADS = 32
GDN_DK = 128
GDN_DV = 128
GDN_QK = GDN_K_HEADS * GDN_DK
GDN_V = GDN_V_HEADS * GDN_DV
GDN_CONV_CH = 2 * GDN_QK + GDN_V

LANES = 128
SUBLANES = 8
VMEM_LIMIT = 56 * 1024 * 1024
COL_TILE = 1024
GLU_COL_TILE = 512


def _col_tile(n, cap):
    return max(t for t in range(LANES, cap + 1, LANES) if n % t == 0)


def _params(sem):
    return pltpu.CompilerParams(dimension_semantics=sem, vmem_limit_bytes=VMEM_LIMIT)


def _silu(x):
    return x * jax.nn.sigmoid(x)


def _dot(a, b):
    return jnp.dot(a.astype(BF16), b.astype(BF16), preferred_element_type=F32)


def _dot_nt(a, b):
    return lax.dot_general(a.astype(BF16), b.astype(BF16), (((1,), (1,)), ((), ())),
                           preferred_element_type=F32)


def _dot_tn(a, b):
    return lax.dot_general(a.astype(BF16), b.astype(BF16), (((0,), (0,)), ((), ())),
                           preferred_element_type=F32)


def _ada_kernel(cs_ref, cp_ref, w_ref, b_ref, os_ref, op_ref, as_scr, ap_scr):
    @pl.when(pl.program_id(0) == 0)
    def _():
        as_scr[...] = _silu(cs_ref[...]).astype(BF16)
        ap_scr[...] = _silu(cp_ref[...]).astype(BF16)

    w = w_ref[...].astype(BF16)
    b = b_ref[...]
    os_ref[...] = jnp.dot(as_scr[...], w, preferred_element_type=F32) + b
    op_ref[...] = jnp.dot(ap_scr[...], w, preferred_element_type=F32) + b


def _ada(cs, cp, w, b, layer, tn=COL_TILE):
    ms, d = cs.shape
    mp = cp.shape[0]
    depth, _, n = w.shape
    return pl.pallas_call(
        _ada_kernel,
        grid=(n // tn,),
        in_specs=[pl.BlockSpec((ms, d), lambda j: (0, 0)),
                  pl.BlockSpec((mp, d), lambda j: (0, 0)),
                  pl.BlockSpec((None, d, tn), lambda j: (layer, 0, j)),
                  pl.BlockSpec((None, 1, tn), lambda j: (layer, 0, j))],
        out_specs=[pl.BlockSpec((ms, tn), lambda j: (0, j)),
                   pl.BlockSpec((mp, tn), lambda j: (0, j))],
        out_shape=[jax.ShapeDtypeStruct((ms, n), F32), jax.ShapeDtypeStruct((mp, n), F32)],
        scratch_shapes=[pltpu.VMEM((ms, d), BF16), pltpu.VMEM((mp, d), BF16)],
        compiler_params=_params(("arbitrary",)),
        name="ada_table",
    )(cs, cp, w, b.reshape(depth, 1, n))


ROW_CHUNK = 128


def _mod_rows(ref, per_row, tiles_per_batch, chunk=None):
    if per_row:
        if chunk is None:
            return jnp.repeat(ref[...], per_row, axis=0)
        n = ROW_CHUNK // per_row
        return jnp.repeat(ref[pl.ds(pl.multiple_of(chunk * n, n), n), :], per_row, axis=0)
    return ref[pl.ds(pl.program_id(0) // tiles_per_batch, 1), :]


def _mod_mm_kernel(x_ref, sh_ref, sc_ref, nw_ref, *rest, glu, per_row, tiles_per_batch):
    if glu:
        wg_ref, wu_ref, o_ref, h_scr = rest
    else:
        w_ref, o_ref, h_scr = rest

    @pl.when(pl.program_id(1) == 0)
    def _():
        nw = nw_ref[...]

        def chunk(r, carry):
            start = pl.multiple_of(r * ROW_CHUNK, ROW_CHUNK)
            x = x_ref[pl.ds(start, ROW_CHUNK), :]
            y = x * lax.rsqrt(jnp.mean(x * x, -1, keepdims=True) + EPS) * nw
            sc = _mod_rows(sc_ref, per_row, tiles_per_batch, r)
            sh = _mod_rows(sh_ref, per_row, tiles_per_batch, r)
            h_scr[pl.ds(start, ROW_CHUNK), :] = (y * (1.0 + sc) + sh).astype(BF16)
            return carry

        lax.fori_loop(0, x_ref.shape[0] // ROW_CHUNK, chunk, 0)

    h = h_scr[...]
    if glu:
        g = jnp.dot(h, wg_ref[...], preferred_element_type=F32)
        u = jnp.dot(h, wu_ref[...], preferred_element_type=F32)
        o_ref[...] = (_silu(g) * u).astype(o_ref.dtype)
    else:
        o_ref[...] = jnp.dot(h, w_ref[...], preferred_element_type=F32).astype(o_ref.dtype)


def _mod_mm(x, mod, sub, nw, w, widx, *, glu, per_row, rows_per_batch, out_dtype, tm, tn, ncols=None):
    t, d = x.shape
    n = w.shape[-1] // 2 if glu else (ncols or w.shape[-1])
    nj = n // tn
    lead = (None,) * len(widx)
    tiles_per_batch = max(rows_per_batch // tm, 1)
    per_row = rows_per_batch if per_row else 0
    mrows = tm // rows_per_batch if per_row else mod.shape[0]

    def mod_spec(c):
        col = sub * 3 + c
        if per_row:
            return pl.BlockSpec((mrows, d), lambda i, j: (i, col))
        return pl.BlockSpec((mrows, d), lambda i, j: (0, col))

    in_specs = [pl.BlockSpec((tm, d), lambda i, j: (i, 0)),
                mod_spec(0), mod_spec(1),
                pl.BlockSpec((1, d), lambda i, j: (0, 0))]
    args = [x, mod, mod, nw.reshape(1, d)]
    if glu:
        in_specs += [pl.BlockSpec(lead + (d, tn), lambda i, j: widx + (0, j)),
                     pl.BlockSpec(lead + (d, tn), lambda i, j: widx + (0, j + nj))]
        args += [w, w]
    else:
        in_specs += [pl.BlockSpec(lead + (d, tn), lambda i, j: widx + (0, j))]
        args += [w]
    return pl.pallas_call(
        functools.partial(_mod_mm_kernel, glu=glu, per_row=per_row, tiles_per_batch=tiles_per_batch),
        grid=(t // tm, nj),
        in_specs=in_specs,
        out_specs=pl.BlockSpec((tm, tn), lambda i, j: (i, j)),
        out_shape=jax.ShapeDtypeStruct((t, n), out_dtype),
        scratch_shapes=[pltpu.VMEM((tm, d), BF16)],
        compiler_params=_params(("parallel", "arbitrary")),
        name="mod_mm_glu" if glu else "mod_mm",
    )(*args)


def _mm_out_kernel(a_ref, w_ref, x_ref, g_ref, nw_ref, o_ref, *, res_scale, per_row, tiles_per_batch):
    y = jnp.dot(a_ref[...].astype(BF16), w_ref[...], preferred_element_type=F32)
    yn = y * lax.rsqrt(jnp.mean(y * y, -1, keepdims=True) + EPS) * nw_ref[...]
    gate = _mod_rows(g_ref, per_row, tiles_per_batch)
    o_ref[...] = x_ref[...] + res_scale * (gate * yn)


def _mm_out(a, w, widx, x, mod, sub, nw, *, res_scale, per_row, rows_per_batch, tm):
    t, kdim = a.shape
    d = w.shape[-1]
    lead = (None,) * len(widx)
    tiles_per_batch = max(rows_per_batch // tm, 1)
    per_row = rows_per_batch if per_row else 0
    col = sub * 3 + 2
    if per_row:
        g_spec = pl.BlockSpec((tm // rows_per_batch, d), lambda i: (i, col))
    else:
        g_spec = pl.BlockSpec((mod.shape[0], d), lambda i: (0, col))
    return pl.pallas_call(
        functools.partial(_mm_out_kernel, res_scale=res_scale, per_row=per_row,
                          tiles_per_batch=tiles_per_batch),
        grid=(t // tm,),
        in_specs=[pl.BlockSpec((tm, kdim), lambda i: (i, 0)),
                  pl.BlockSpec(lead + (kdim, d), lambda i: widx + (0, 0), pipeline_mode=pl.Buffered(1)),
                  pl.BlockSpec((tm, d), lambda i: (i, 0)),
                  g_spec,
                  pl.BlockSpec((1, d), lambda i: (0, 0))],
        out_specs=pl.BlockSpec((tm, d), lambda i: (i, 0)),
        out_shape=jax.ShapeDtypeStruct((t, d), F32),
        compiler_params=_params(("parallel",)),
        name="mm_out",
    )(a, w, x, mod, nw.reshape(1, d))


def _rotate(x, cos, sin):
    half = x.shape[-1] // 2
    x1, x2 = x[:, :half], x[:, half:]
    return jnp.concatenate([x1 * cos - x2 * sin, x1 * sin + x2 * cos], axis=-1)


def _ret_kernel(dm_ref, qd_ref, kd_ref, cd_ref, cos_ref, sin_ref, q_ref, k_ref, v_ref, g_ref, *rest,
                heads, has_state):
    if has_state:
        s0_ref, o_ref, sout_ref, s_scr = rest
    else:
        o_ref, sout_ref, s_scr = rest
    n = pl.program_id(2)
    dk, dv = RET_DK, RET_DV

    @pl.when(n == 0)
    def _():
        if has_state:
            s_scr[...] = s0_ref[...]
        else:
            s_scr[...] = jnp.zeros_like(s_scr)

    cos, sin = cos_ref[...], sin_ref[...]
    hs = range(heads)
    q = [_rotate(q_ref[:, h * dk:(h + 1) * dk], cos, sin) * (dk ** -0.5) for h in hs]
    k = [_rotate(k_ref[:, h * dk:(h + 1) * dk], cos, sin) for h in hs]
    v = [v_ref[:, h * dv:(h + 1) * dv].astype(BF16) for h in hs]
    s = [s_scr[h] for h in hs]
    scores = [_dot_nt(q[h], k[h]) * dm_ref[h] for h in hs]
    cross = [_dot(q[h], s[h]) for h in hs]
    inner = [_dot(scores[h], v[h]) for h in hs]
    upd = [_dot_tn(k[h] * jnp.concatenate([kd_ref[h]] * (dk // LANES), axis=1), v[h]) for h in hs]
    for h in hs:
        s_new = s[h] * cd_ref[h, 0:1, 0:1] + upd[h]
        s_scr[h] = s_new
        o = inner[h] + cross[h] * jnp.concatenate([qd_ref[h]] * (dv // LANES), axis=1)
        on = o * lax.rsqrt(jnp.mean(o * o, -1, keepdims=True) + EPS)
        o_ref[:, h * dv:(h + 1) * dv] = (_silu(g_ref[:, h * dv:(h + 1) * dv]) * on).astype(o_ref.dtype)

    @pl.when(n == pl.num_programs(2) - 1)
    def _():
        sout_ref[...] = s_scr[...]


def _retention(proj, s0, pos0, batch, seq, c, heads):
    nh, dk, dv = RET_HEADS, RET_DK, RET_DV
    nc = seq // c
    ng = nh // heads
    half = dk // 2
    log_g = jnp.log1p(-jnp.exp2(-5.0 - jnp.arange(nh, dtype=F32)))
    idx = jnp.arange(c, dtype=F32)
    diff = idx[:, None] - idx[None, :]
    causal = diff >= 0
    dmask = jnp.where(causal[None], jnp.exp(log_g[:, None, None] * jnp.where(causal, diff, 0.0)[None]), 0.0)
    lanes = lambda x: jnp.broadcast_to(x[:, :, None], (nh, x.shape[1], LANES))
    q_decay = lanes(jnp.exp(log_g[:, None] * (idx[None, :] + 1.0)))
    k_decay = lanes(jnp.exp(log_g[:, None] * (c - 1.0 - idx)[None, :]))
    chunk_decay = jnp.broadcast_to(jnp.exp(log_g * c)[:, None, None], (nh, SUBLANES, LANES))
    inv = 1.0 / (ROPE_BASE ** jnp.linspace(0.0, 1.0, half, dtype=F32))
    ang = (pos0 + jnp.arange(seq)).astype(F32)[:, None] * inv[None, :]
    cos, sin = jnp.cos(ang), jnp.sin(ang)
    has_state = s0 is not None
    tokb = lambda b, g, n: b * nc + n
    in_specs = [pl.BlockSpec((heads, c, c), lambda b, g, n: (g, 0, 0)),
                pl.BlockSpec((heads, c, LANES), lambda b, g, n: (g, 0, 0)),
                pl.BlockSpec((heads, c, LANES), lambda b, g, n: (g, 0, 0)),
                pl.BlockSpec((heads, SUBLANES, LANES), lambda b, g, n: (g, 0, 0)),
                pl.BlockSpec((c, half), lambda b, g, n: (n, 0)),
                pl.BlockSpec((c, half), lambda b, g, n: (n, 0)),
                pl.BlockSpec((c, heads * dk), lambda b, g, n: (tokb(b, g, n), g)),
                pl.BlockSpec((c, heads * dk), lambda b, g, n: (tokb(b, g, n), ng + g)),
                pl.BlockSpec((c, heads * dv), lambda b, g, n: (tokb(b, g, n), ng + g)),
                pl.BlockSpec((c, heads * dv), lambda b, g, n: (tokb(b, g, n), 2 * ng + g))]
    args = [dmask, q_decay, k_decay, chunk_decay, cos, sin, proj, proj, proj, proj]
    if has_state:
        in_specs.append(pl.BlockSpec((None, heads, dk, dv), lambda b, g, n: (b, g, 0, 0)))
        args.append(s0)
    return pl.pallas_call(
        functools.partial(_ret_kernel, heads=heads, has_state=has_state),
        grid=(batch, ng, nc),
        in_specs=in_specs,
        out_specs=[pl.BlockSpec((c, heads * dv), lambda b, g, n: (tokb(b, g, n), g)),
                   pl.BlockSpec((None, heads, dk, dv), lambda b, g, n: (b, g, 0, 0))],
        out_shape=[jax.ShapeDtypeStruct((batch * seq, nh * dv), BF16 if c % (2 * SUBLANES) == 0 else F32),
                   jax.ShapeDtypeStruct((batch, nh, dk, dv), F32)],
        scratch_shapes=[pltpu.VMEM((heads, dk, dv), F32)],
        compiler_params=_params(("parallel", "parallel", "arbitrary")),
        name="retention",
    )(*args)


def _gdn_ab_weight(w_in):
    d = w_in.shape[0]
    ab = w_in[:, GDN_CONV_CH + GDN_V:]
    a = ab[:, :GDN_V_HEADS].reshape(d, GDN_K_HEADS, 2)
    b = ab[:, GDN_V_HEADS:].reshape(d, GDN_K_HEADS, 2)
    tail = jnp.concatenate([a, b, jnp.zeros((d, GDN_K_HEADS, LANES - 4), w_in.dtype)], axis=-1)
    return tail.reshape(d, GDN_K_HEADS * LANES).astype(BF16)


def _gdn_head_tables(a_log, dt_bias):
    def tab(x):
        row = jnp.concatenate([x.reshape(GDN_K_HEADS, 2), jnp.zeros((GDN_K_HEADS, LANES - 2), F32)], -1)
        return jnp.broadcast_to(row[:, None, :], (GDN_K_HEADS, SUBLANES, LANES))
    return tab(a_log), tab(dt_bias)


GDN_C = 64
GDN_SUB = 256


def _chunk_cumsum(x, c):
    pos = lax.broadcasted_iota(jnp.int32, x.shape, 0) % c
    s = 1
    while s < c:
        x = x + jnp.where(pos >= s, pltpu.roll(x, s, axis=0), 0.0)
        s *= 2
    return x


GDN_BASE = 8


def _inverse_masks(c, width):
    i = jnp.arange(c)[:, None]
    j = (jnp.arange(width) % c)[None, :]
    r = (jnp.arange(width) % (2 * c))[:, None]
    s = jnp.arange(2 * c)[None, :]
    lanes = [(i // GDN_BASE) == (j // GDN_BASE)]
    pair = [(r // c) == (s // c), (r // GDN_BASE) == (s // GDN_BASE)]
    b = GDN_BASE
    while b < c:
        lanes.append(((i // (2 * b)) == (j // (2 * b))) & ((i // b) % 2 == 1) & ((j // b) % 2 == 0))
        pair.append(((r // (2 * b)) == (s // (2 * b))) & ((r // b) % 2 == 1) & ((s // b) % 2 == 0))
        b *= 2
    causal = jnp.stack([pair[0] & (r >= s), pair[0] & (r > s)]).astype(F32)
    return jnp.stack(lanes).astype(F32), jnp.stack(pair).astype(BF16), causal


def _mask_dict(lanes_ref, pair_ref):
    masks = {"base_lanes": lanes_ref[0], "same_pair": pair_ref[0], "base_pair": pair_ref[1]}
    for n in range(1, lanes_ref.shape[0]):
        masks[f"lanes{GDN_BASE << (n - 1)}"] = lanes_ref[n]
        masks[f"pair{GDN_BASE << (n - 1)}"] = pair_ref[n + 1]
    return masks


def _pair_to_diag(p):
    width, pc = p.shape
    n = width // pc
    zero = jnp.zeros((pc, pc), p.dtype)
    return jnp.concatenate(
        [jnp.concatenate([p[i * pc:(i + 1) * pc] if j == i else zero for j in range(n)], axis=1)
         for i in range(n)], axis=0)


def _lanes_to_diag(p, masks):
    c, width = p.shape
    pb = p.astype(BF16)
    pair = jnp.concatenate([jnp.concatenate([pb[:, i * 2 * c:(i + 1) * 2 * c]] * 2, axis=0)
                            for i in range(width // (2 * c))], axis=0)
    return _pair_to_diag(pair * masks["same_pair"])


def _inverse_minus_eye_lanes(l_lbs, l_pairs, masks):
    c = l_lbs[0].shape[0]
    ps = range(len(l_lbs))
    mm = lambda a, b: jnp.dot(a.astype(BF16), b, preferred_element_type=F32)

    lb = [l_lbs[i] * masks["base_lanes"] for i in ps]
    x = [-lb[i] for i in ps]
    p = [mm(lb[i], _pair_to_diag(l_pairs[i] * masks["base_pair"])) for i in ps]
    n = 2
    while 2 * n < GDN_BASE:
        xp = [mm(jnp.concatenate([x[i], p[i]], axis=0), _lanes_to_diag(p[i], masks)) for i in ps]
        x = [x[i] + p[i] + xp[i][:c] for i in ps]
        p = [xp[i][c:] for i in ps]
        n *= 2
    xp = [mm(x[i], _lanes_to_diag(p[i], masks)) for i in ps]
    x = [x[i] + p[i] + xp[i] for i in ps]

    b = GDN_BASE
    while b < c:
        t = [l_lbs[i] * masks[f"lanes{b}"]
             + mm(x[i], _pair_to_diag(l_pairs[i] * masks[f"pair{b}"])) for i in ps]
        tx = [mm(t[i], _lanes_to_diag(x[i], masks)) for i in ps]
        x = [x[i] - t[i] - tx[i] for i in ps]
        b *= 2
    return x


def _gdn_prep_kernel(q_ref, k_ref, v_ref, ab_ref, qp_ref, kp_ref, vp_ref, wq_ref, wk_ref, wv_ref,
                     alog_ref, dt_ref, ml_ref, mp_ref, mc_ref, u_ref, w_ref, qg_ref, kgt_ref, at_ref, eg_ref,
                     bq, bk, bv, gc_scr, *, tb):
    c, sub = GDN_C, GDN_SUB
    first = pl.program_id(1) == 0

    def conv_silu(buf, prev_ref, x_ref, cw_ref):
        buf[pl.ds(0, SUBLANES), :] = jnp.where(first, 0.0, prev_ref[...])
        buf[pl.ds(SUBLANES, tb), :] = x_ref[...]
        cw = cw_ref[...]
        out = buf[pl.ds(SUBLANES - 3, tb), :] * cw[0:1, :]
        for j in range(1, CONV_W):
            out = out + buf[pl.ds(SUBLANES - 3 + j, tb), :] * cw[j:j + 1, :]
        return _silu(out)

    q = conv_silu(bq, qp_ref, q_ref, wq_ref)
    k = conv_silu(bk, kp_ref, k_ref, wk_ref)
    v2 = conv_silu(bv, vp_ref, v_ref, wv_ref)
    q = q * lax.rsqrt(jnp.sum(q * q, -1, keepdims=True) + EPS) * (GDN_DK ** -0.5)
    k = k * lax.rsqrt(jnp.sum(k * k, -1, keepdims=True) + EPS)

    ab = ab_ref[...]
    g_all = -jnp.exp(alog_ref[0:1, :]) * jax.nn.softplus(ab + dt_ref[0:1, :])
    beta_all = jax.nn.sigmoid(ab)
    gc_all = _chunk_cumsum(g_all, c)
    gc_t = gc_all.T

    masks = _mask_dict(ml_ref, mp_ref)
    tri = mc_ref[0] > 0.0
    strict = mc_ref[1] > 0.0

    nh, nchunk, pc = tb // sub, sub // c, 2 * c
    rows = [slice(hf * sub, (hf + 1) * sub) for hf in range(nh)]
    pairs = [slice(i * pc, (i + 1) * pc) for i in range(tb // pc)]
    kq = [_dot_nt(jnp.concatenate([k[pr], q[pr]], axis=0), k[pr]) for pr in pairs]
    npair = sub // pc
    kk = [jnp.concatenate([kq[hf * npair + i][:pc] for i in range(npair)], axis=0) for hf in range(nh)]
    qk = [jnp.concatenate([kq[hf * npair + i][pc:] for i in range(npair)], axis=0) for hf in range(nh)]
    probs = [(hf, e) for hf in range(nh) for e in range(2)]
    gcb, betab, gl, l_pair, l_lb = [], [], [], [], []
    for p, (hf, e) in enumerate(probs):
        lanes = slice(e * GDN_DV, (e + 1) * GDN_DV)
        gcb.append(jnp.broadcast_to(gc_all[rows[hf], e:e + 1], (sub, LANES)))
        betab.append(jnp.broadcast_to(beta_all[rows[hf], 2 + e:3 + e], (sub, LANES)))
        gc_scr[p] = gcb[p]
        gl.append(gc_scr[p, pl.ds(c - 1, nchunk, stride=c), :])
        gc_cols = jnp.concatenate(
            [jnp.broadcast_to(gc_t[e:e + 1, hf * sub + i * pc:hf * sub + (i + 1) * pc], (pc, pc))
             for i in range(npair)], axis=0)
        decay = jnp.where(tri, jnp.exp(jnp.where(tri, gcb[p] - gc_cols, 0.0)), 0.0)
        lp = jnp.where(strict, kk[hf] * betab[p] * decay, 0.0)
        at_ref[rows[hf], lanes] = (qk[hf] * decay).astype(BF16)
        l_pair.append(lp.astype(BF16))
        l_lb.append(jnp.concatenate([lp[2 * i * c:(2 * i + 1) * c] + lp[(2 * i + 1) * c:(2 * i + 2) * c]
                                     for i in range(npair)], axis=1))
    x_lb = _inverse_minus_eye_lanes(l_lb, l_pair, masks)
    rhs = [jnp.concatenate([v2[rows[hf], e * GDN_DV:(e + 1) * GDN_DV] * betab[p],
                            k[rows[hf]] * betab[p] * jnp.exp(gcb[p])], axis=1)
           for p, (hf, e) in enumerate(probs)]
    corr = [jnp.dot(_lanes_to_diag(x_lb[p], masks), rhs[p].astype(BF16), preferred_element_type=F32)
            for p in range(len(probs))]
    for p, (hf, e) in enumerate(probs):
        lanes = slice(e * GDN_DV, (e + 1) * GDN_DV)
        sol = rhs[p] + corr[p]
        u_ref[rows[hf], lanes] = sol[:, :GDN_DV]
        w_ref[rows[hf], lanes] = sol[:, GDN_DV:].astype(BF16)
        qg_ref[rows[hf], lanes] = (q[rows[hf]] * jnp.exp(gcb[p])).astype(BF16)
        gl_b = jnp.concatenate([jnp.broadcast_to(gl[p][i:i + 1, :], (c, LANES)) for i in range(nchunk)], axis=0)
        kg = k[rows[hf]] * jnp.exp(gl_b - gcb[p])
        kgt_ref[rows[hf], lanes] = jnp.concatenate(
            [kg[i * pc:(i + 1) * pc].T for i in range(npair)], axis=0).astype(BF16)
        eg_ref[hf * nchunk:(hf + 1) * nchunk, lanes] = jnp.exp(gl[p])


def _gdn_scan_kernel(u_ref, w_ref, qg_ref, kgt_ref, at_ref, eg_ref, z_ref, nw_ref, o_ref, sout_ref, s_scr,
                     *, tb, heads):
    c = GDN_C

    @pl.when(pl.program_id(2) == 0)
    def _():
        s_scr[...] = jnp.zeros_like(s_scr)

    nw = nw_ref[...]
    hl = [slice(h * GDN_DV, (h + 1) * GDN_DV) for h in range(heads)]
    s = [s_scr[h] for h in range(heads)]
    zeros = jnp.zeros((c, GDN_DV), BF16)
    for i in range(tb // c):
        rows = slice(i * c, (i + 1) * c)
        pair_rows = slice((i // 2) * 2 * c, (i // 2 + 1) * 2 * c)
        ws = [jnp.dot(jnp.concatenate([w_ref[rows, hl[h]], qg_ref[rows, hl[h]]], axis=0),
                      s[h].astype(BF16), preferred_element_type=F32) for h in range(heads)]
        v_new = [(u_ref[rows, hl[h]] - ws[h][:c]).astype(BF16) for h in range(heads)]
        pair = [jnp.concatenate([v_new[h], zeros] if i % 2 == 0 else [zeros, v_new[h]], axis=0)
                for h in range(heads)]
        for h in range(heads):
            o = ws[h][c:] + jnp.dot(at_ref[rows, hl[h]], pair[h], preferred_element_type=F32)
            s[h] = s[h] * eg_ref[i:i + 1, hl[h]] + jnp.dot(kgt_ref[pair_rows, hl[h]], pair[h],
                                                          preferred_element_type=F32)
            on = o * lax.rsqrt(jnp.mean(o * o, -1, keepdims=True) + EPS) * nw
            o_ref[rows, hl[h]] = (on * _silu(z_ref[rows, hl[h]])).astype(o_ref.dtype)
    for h in range(heads):
        s_scr[h] = s[h]

    @pl.when(pl.program_id(2) == pl.num_programs(2) - 1)
    def _():
        sout_ref[...] = s_scr[...]


def _gdn_prompt(proj, proj_ab, conv_w, a_log, dt_bias, norm_w, batch, seq, tb_prep=1024, tb_scan=512,
                heads=8):
    hk, dk, dv = GDN_K_HEADS, GDN_DK, GDN_DV
    assert seq % tb_prep == 0 and seq % tb_scan == 0
    t = batch * seq
    tb = tb_prep
    nt = seq // tb
    kq0 = GDN_QK // dk
    v0 = 2 * GDN_QK // (2 * dv)
    alog_tab, dt_tab = _gdn_head_tables(a_log, dt_bias)
    m_lanes, m_pair, m_causal = _inverse_masks(GDN_C, GDN_SUB)
    tok = lambda b, n, j: b * nt + n
    prev = lambda b, n, j: jnp.maximum((b * seq + n * tb) // SUBLANES - 1, 0)
    u, w, qg, kg, at, eg = pl.pallas_call(
        functools.partial(_gdn_prep_kernel, tb=tb),
        grid=(batch, nt, hk),
        in_specs=[pl.BlockSpec((tb, dk), lambda b, n, j: (tok(b, n, j), j)),
                  pl.BlockSpec((tb, dk), lambda b, n, j: (tok(b, n, j), kq0 + j)),
                  pl.BlockSpec((tb, 2 * dv), lambda b, n, j: (tok(b, n, j), v0 + j)),
                  pl.BlockSpec((tb, LANES), lambda b, n, j: (tok(b, n, j), j)),
                  pl.BlockSpec((SUBLANES, dk), lambda b, n, j: (prev(b, n, j), j)),
                  pl.BlockSpec((SUBLANES, dk), lambda b, n, j: (prev(b, n, j), kq0 + j)),
                  pl.BlockSpec((SUBLANES, 2 * dv), lambda b, n, j: (prev(b, n, j), v0 + j)),
                  pl.BlockSpec((CONV_W, dk), lambda b, n, j: (0, j)),
                  pl.BlockSpec((CONV_W, dk), lambda b, n, j: (0, kq0 + j)),
                  pl.BlockSpec((CONV_W, 2 * dv), lambda b, n, j: (0, v0 + j)),
                  pl.BlockSpec((None, SUBLANES, LANES), lambda b, n, j: (j, 0, 0)),
                  pl.BlockSpec((None, SUBLANES, LANES), lambda b, n, j: (j, 0, 0)),
                  pl.BlockSpec(m_lanes.shape, lambda b, n, j: (0, 0, 0)),
                  pl.BlockSpec(m_pair.shape, lambda b, n, j: (0, 0, 0)),
                  pl.BlockSpec(m_causal.shape, lambda b, n, j: (0, 0, 0))],
        out_specs=[pl.BlockSpec((tb, 2 * dv), lambda b, n, j: (tok(b, n, j), j))] * 5
        + [pl.BlockSpec((tb // GDN_C, 2 * dv), lambda b, n, j: (tok(b, n, j), j))],
        out_shape=[jax.ShapeDtypeStruct((t, GDN_V), F32)]
        + [jax.ShapeDtypeStruct((t, GDN_V), BF16)] * 4
        + [jax.ShapeDtypeStruct((t // GDN_C, GDN_V), F32)],
        scratch_shapes=[pltpu.VMEM((SUBLANES + tb, dk), F32),
                        pltpu.VMEM((SUBLANES + tb, dk), F32),
                        pltpu.VMEM((SUBLANES + tb, 2 * dv), F32),
                        pltpu.VMEM((2 * tb // GDN_SUB, GDN_SUB, LANES), F32)],
        compiler_params=_params(("parallel", "parallel", "parallel")),
        name="gdn_prep",
    )(proj, proj, proj, proj_ab, proj, proj, proj, conv_w, conv_w, conv_w, alog_tab, dt_tab,
      m_lanes, m_pair, m_causal)

    tb = tb_scan
    nt = seq // tb
    wide = heads * dv
    z0 = GDN_CONV_CH // wide
    blk = lambda b, g, n: (b * nt + n, g)
    return pl.pallas_call(
        functools.partial(_gdn_scan_kernel, tb=tb, heads=heads),
        grid=(batch, GDN_V_HEADS // heads, nt),
        in_specs=[pl.BlockSpec((tb, wide), blk)] * 5
        + [pl.BlockSpec((tb // GDN_C, wide), blk),
           pl.BlockSpec((tb, wide), lambda b, g, n: (b * nt + n, z0 + g)),
           pl.BlockSpec((1, dv), lambda b, g, n: (0, 0))],
        out_specs=[pl.BlockSpec((tb, wide), blk),
                   pl.BlockSpec((None, heads, dk, dv), lambda b, g, n: (b, g, 0, 0))],
        out_shape=[jax.ShapeDtypeStruct((t, GDN_V), BF16),
                   jax.ShapeDtypeStruct((batch, GDN_V_HEADS, dk, dv), F32)],
        scratch_shapes=[pltpu.VMEM((heads, dk, dv), F32)],
        compiler_params=_params(("parallel", "parallel", "arbitrary")),
        name="gdn_scan",
    )(u, w, qg, kg, at, eg, proj, norm_w.reshape(1, dv))


def _gdn_step_kernel(q_ref, k_ref, v_ref, z_ref, ab_ref, cq_ref, ck_ref, cv_ref, wq_ref, wk_ref, wv_ref,
                     alog_ref, dt_ref, nw_ref, s0_ref, o_ref, sout_ref, cout_ref, bq, bk, bv, g_scr, *, seq, nb):
    hv, hk, dk, dv = GDN_V_HEADS, GDN_K_HEADS, GDN_DK, GDN_DV
    n = hv * seq
    rep = hv // hk
    es = range(nb)

    def conv_silu(e, buf, c_ref, x_ref, cw_ref, col):
        width = buf.shape[-1]
        buf[e, pl.ds(SUBLANES - 3, 3), :] = c_ref[e]
        buf[e, pl.ds(SUBLANES, seq), :] = x_ref[e * seq:(e + 1) * seq, :]
        cout_ref[e, :, col:col + width] = buf[e, pl.ds(SUBLANES + seq - 3, 3), :]
        cw = cw_ref[...]
        out = buf[e, pl.ds(SUBLANES - 3, seq), :] * cw[0:1, :]
        for j in range(1, CONV_W):
            out = out + buf[e, pl.ds(SUBLANES - 3 + j, seq), :] * cw[j:j + 1, :]
        return _silu(out)

    def stack(x, width):
        return jnp.concatenate([x[:, h * width:(h + 1) * width] for h in range(x.shape[1] // width)], axis=0)

    def per_v_head(x):
        return jnp.concatenate([x[(h // rep) * seq:(h // rep + 1) * seq] for h in range(hv)], axis=0)

    def unit(x):
        return x * lax.rsqrt(jnp.sum(x * x, -1, keepdims=True) + EPS)

    def per_head_rows(x_st, lane0):
        cols = [jnp.broadcast_to(x_st[:, lane0 + e:lane0 + e + 1], (hk * seq, LANES)) for e in range(rep)]
        return jnp.concatenate([cols[h % rep][(h // rep) * seq:(h // rep + 1) * seq] for h in range(hv)], axis=0)

    q = [per_v_head(unit(stack(conv_silu(e, bq, cq_ref, q_ref, wq_ref, 0), dk))) * (dk ** -0.5) for e in es]
    k = [per_v_head(unit(stack(conv_silu(e, bk, ck_ref, k_ref, wk_ref, GDN_QK), dk))) for e in es]
    v = [stack(conv_silu(e, bv, cv_ref, v_ref, wv_ref, 2 * GDN_QK), dv) for e in es]

    ab = [stack(ab_ref[e * seq:(e + 1) * seq, :], LANES) for e in es]
    gc_st = [_chunk_cumsum(-jnp.exp(alog_ref[...]) * jax.nn.softplus(ab[e] + dt_ref[...]), seq) for e in es]
    gcb = [per_head_rows(gc_st[e], 0) for e in es]
    betab = [per_head_rows(jax.nn.sigmoid(ab[e]), rep) for e in es]
    gc_row = [gcb[e].T[0:1, :] for e in es]
    gl = []
    for e in es:
        g_scr[e] = gcb[e]
        gl.append(g_scr[e, pl.ds(seq - 1, hv, stride=seq), :])
    eg = [jnp.exp(gl[e]) for e in es]

    r = lax.broadcasted_iota(jnp.int32, (n, n), 0)
    s = lax.broadcasted_iota(jnp.int32, (n, n), 1)
    same = (r // seq) == (s // seq)
    tri = same & (r >= s)
    strict = same & (r > s)
    wide = lambda x: jnp.concatenate([x] * (n // LANES), axis=1)
    decay = [jnp.where(tri, jnp.exp(jnp.where(tri, wide(gcb[e]) - gc_row[e], 0.0)), 0.0) for e in es]
    kq = [_dot_nt(jnp.concatenate([k[e], q[e]], axis=0), k[e]) for e in es]
    l_bd = [jnp.where(strict, kq[e][:n] * wide(betab[e]) * decay[e], 0.0) for e in es]
    a_bd = [kq[e][n:] * decay[e] for e in es]

    x = [-l_bd[e] for e in es]
    p = [_dot(l_bd[e], l_bd[e]) for e in es]
    m = 2
    while 2 * m < seq:
        xp = [_dot(jnp.concatenate([x[e], p[e]], axis=0), p[e]) for e in es]
        x = [x[e] + p[e] + xp[e][:n] for e in es]
        p = [xp[e][n:] for e in es]
        m *= 2
    xp = [_dot(x[e], p[e]) for e in es]
    x = [x[e] + p[e] + xp[e] for e in es]

    rhs = [jnp.concatenate([v[e] * betab[e], k[e] * betab[e] * jnp.exp(gcb[e])], axis=1) for e in es]
    corr = [_dot(x[e], rhs[e]) for e in es]
    sol = [rhs[e] + corr[e] for e in es]
    qg = [q[e] * jnp.exp(gcb[e]) for e in es]
    kg = []
    for e in es:
        gl_b = jnp.concatenate([jnp.broadcast_to(gl[e][h:h + 1, :], (seq, LANES)) for h in range(hv)], axis=0)
        kg.append(k[e] * jnp.exp(gl_b - gcb[e]))

    eh = [(e, h) for e in es for h in range(hv)]
    hrows = [slice(h * seq, (h + 1) * seq) for h in range(hv)]
    ws = {(e, h): _dot(jnp.concatenate([sol[e][hrows[h], dv:], qg[e][hrows[h]]], axis=0), s0_ref[e, h])
          for e, h in eh}
    v_new = {(e, h): sol[e][hrows[h], :dv] - ws[e, h][:seq] for e, h in eh}
    upd = {(e, h): _dot_tn(kg[e][hrows[h]], v_new[e, h]) for e, h in eh}
    for e, h in eh:
        sout_ref[e, h] = s0_ref[e, h] * eg[e][h:h + 1, :] + upd[e, h]
    inner = [_dot(a_bd[e], jnp.concatenate([v_new[e, h] for h in range(hv)], axis=0)) for e in es]
    for e in es:
        o = jnp.concatenate([ws[e, h][seq:] for h in range(hv)], axis=0) + inner[e]
        z = stack(z_ref[e * seq:(e + 1) * seq, :], dv)
        on = o * lax.rsqrt(jnp.mean(o * o, -1, keepdims=True) + EPS) * nw_ref[...] * _silu(z)
        for h in range(hv):
            o_ref[e * seq:(e + 1) * seq, h * dv:(h + 1) * dv] = on[h * seq:(h + 1) * seq]


def _gdn_step(proj, proj_ab, conv_w, a_log, dt_bias, norm_w, s0, buf0, batch, seq, nb=2):
    hv, hk, dk, dv = GDN_V_HEADS, GDN_K_HEADS, GDN_DK, GDN_DV
    assert seq == SUBLANES and batch % nb == 0
    qw, vw, abw = GDN_QK, GDN_V, hk * LANES
    alog_tab, dt_tab = _gdn_head_tables(a_log, dt_bias)
    alog_tab = alog_tab.reshape(hk * seq, LANES)
    dt_tab = dt_tab.reshape(hk * seq, LANES)
    return pl.pallas_call(
        functools.partial(_gdn_step_kernel, seq=seq, nb=nb),
        grid=(batch // nb,),
        in_specs=[pl.BlockSpec((nb * seq, qw), lambda b: (b, 0)),
                  pl.BlockSpec((nb * seq, qw), lambda b: (b, 1)),
                  pl.BlockSpec((nb * seq, vw), lambda b: (b, 2 * qw // vw)),
                  pl.BlockSpec((nb * seq, vw), lambda b: (b, GDN_CONV_CH // vw)),
                  pl.BlockSpec((nb * seq, abw), lambda b: (b, 0)),
                  pl.BlockSpec((nb, CONV_W - 1, qw), lambda b: (b, 0, 0)),
                  pl.BlockSpec((nb, CONV_W - 1, qw), lambda b: (b, 0, 1)),
                  pl.BlockSpec((nb, CONV_W - 1, vw), lambda b: (b, 0, 2 * qw // vw)),
                  pl.BlockSpec((CONV_W, qw), lambda b: (0, 0)),
                  pl.BlockSpec((CONV_W, qw), lambda b: (0, 1)),
                  pl.BlockSpec((CONV_W, vw), lambda b: (0, 2 * qw // vw)),
                  pl.BlockSpec((hk * seq, LANES), lambda b: (0, 0)),
                  pl.BlockSpec((hk * seq, LANES), lambda b: (0, 0)),
                  pl.BlockSpec((1, dv), lambda b: (0, 0)),
                  pl.BlockSpec((nb, hv, dk, dv), lambda b: (b, 0, 0, 0))],
        out_specs=[pl.BlockSpec((nb * seq, vw), lambda b: (b, 0)),
                   pl.BlockSpec((nb, hv, dk, dv), lambda b: (b, 0, 0, 0)),
                   pl.BlockSpec((nb, CONV_W - 1, GDN_CONV_CH), lambda b: (b, 0, 0))],
        out_shape=[jax.ShapeDtypeStruct((batch * seq, GDN_V), F32),
                   jax.ShapeDtypeStruct((batch, hv, dk, dv), F32),
                   jax.ShapeDtypeStruct((batch, CONV_W - 1, GDN_CONV_CH), F32)],
        scratch_shapes=[pltpu.VMEM((nb, 2 * SUBLANES, qw), F32),
                        pltpu.VMEM((nb, 2 * SUBLANES, qw), F32),
                        pltpu.VMEM((nb, 2 * SUBLANES, vw), F32),
                        pltpu.VMEM((nb, hv * seq, LANES), F32)],
        compiler_params=_params(("parallel",)),
        name="gdn_step",
    )(proj, proj, proj, proj, proj_ab, buf0, buf0, buf0, conv_w, conv_w, conv_w, alog_tab, dt_tab,
      norm_w.reshape(1, dv), s0)


def _trunk(x, mod_fn, per_row, batch, seq, pos0, ret_state, gdn_state, conv_state, wts, tm):
    (norm_pre, norm_post, w_gu, w_down, ret_w_in, ret_w_out, gdn_w_in, gdn_w_ab, gdn_conv_w, gdn_a_log,
     gdn_dt_bias, gdn_norm_w, gdn_w_out) = wts
    kw = dict(per_row=per_row, rows_per_batch=seq)
    depth = norm_pre.shape[0]
    new_ret, new_gdn, new_conv = [], [], []
    for i in range(depth):
        mod = mod_fn(i)

        def ffn(x, sub, f):
            a = _mod_mm(x, mod, sub, norm_pre[i, sub], w_gu, (i, f), glu=True, out_dtype=BF16,
                        tm=tm, tn=_col_tile(w_gu.shape[-1] // 2, GLU_COL_TILE), **kw)
            return _mm_out(a, w_down, (i, f), x, mod, sub, norm_post[i, sub], res_scale=FFN_RES,
                           tm=256, **kw)

        x = ffn(x, 0, 0)
        r = i // 2
        if i % 2 == 0:
            proj = _mod_mm(x, mod, 1, norm_pre[i, 1], ret_w_in, (r,), glu=False, out_dtype=F32,
                           tm=tm, tn=_col_tile(ret_w_in.shape[-1], COL_TILE), **kw)
            y, s = _retention(proj, None if ret_state is None else ret_state[r], pos0, batch, seq,
                              c=math.gcd(256, seq), heads=4 if seq >= 256 else RET_HEADS)
            new_ret.append(s)
            x = _mm_out(y, ret_w_out, (r,), x, mod, 1, norm_post[i, 1], res_scale=1.0,
                        tm=256, **kw)
        else:
            wide = GDN_CONV_CH + GDN_V
            proj = _mod_mm(x, mod, 1, norm_pre[i, 1], gdn_w_in, (r,), glu=False, out_dtype=F32,
                           tm=tm, tn=_col_tile(wide, COL_TILE), ncols=wide, **kw)
            proj_ab = _mod_mm(x, mod, 1, norm_pre[i, 1], gdn_w_ab, (r,), glu=False, out_dtype=F32,
                              tm=tm, tn=_col_tile(gdn_w_ab.shape[-1], COL_TILE), **kw)
            if gdn_state is None:
                y, s = _gdn_prompt(proj, proj_ab, gdn_conv_w[r], gdn_a_log[r], gdn_dt_bias[r],
                                   gdn_norm_w[r], batch, seq)
                cs = proj.reshape(batch, seq, -1)[:, seq - (CONV_W - 1):, :GDN_CONV_CH]
            else:
                y, s, cs = _gdn_step(proj, proj_ab, gdn_conv_w[r], gdn_a_log[r], gdn_dt_bias[r],
                                     gdn_norm_w[r], gdn_state[r], conv_state[r], batch, seq)
            new_gdn.append(s)
            new_conv.append(cs)
            x = _mm_out(y, gdn_w_out, (r,), x, mod, 1, norm_post[i, 1], res_scale=1.0,
                        tm=256, **kw)
        x = ffn(x, 2, 1)
    stack = lambda xs: xs[0][None] if len(xs) == 1 else jnp.stack(xs)
    return x, stack(new_ret), stack(new_gdn), stack(new_conv)


def kernel(x_prompt, x_sample, c_prompt, c_sample, state_ret, state_gdn, state_conv, w_ada, b_ada,
           norm_pre, norm_post, ffn_w_gu, ffn_w_down, ret_w_in, ret_w_out, gdn_w_in, gdn_conv_w,
           gdn_a_log, gdn_dt_bias, gdn_norm_w, gdn_w_out):
    bp, lp, d = x_prompt.shape
    bs, ls, _ = x_sample.shape
    wts = (norm_pre, norm_post, ffn_w_gu.astype(BF16), ffn_w_down.astype(BF16),
           ret_w_in.astype(BF16), ret_w_out.astype(BF16),
           gdn_w_in.astype(BF16), jax.vmap(_gdn_ab_weight)(gdn_w_in),
           gdn_conv_w, gdn_a_log, gdn_dt_bias, gdn_norm_w, gdn_w_out.astype(BF16))

    cs_rows = c_sample
    cp_rows = jnp.pad(c_prompt, ((0, 2 * SUBLANES - bp), (0, 0)))
    mods = [_ada(cs_rows, cp_rows, w_ada, b_ada, i) for i in range(w_ada.shape[0])]

    y_p, ret_p, gdn_p, conv_p = _trunk(
        x_prompt.reshape(bp * lp, d), lambda i: mods[i][1], False, bp, lp, 0,
        None, None, None, wts, tm=1024)
    y_s, ret_s, gdn_s, conv_s = _trunk(
        x_sample.reshape(bs * ls, d), lambda i: mods[i][0], True, bs, ls, PAST_LEN,
        state_ret, state_gdn, state_conv, wts, tm=1024)
    return (y_p.reshape(bp, lp, d), y_s.reshape(bs, ls, d), ret_p, ret_s, gdn_p, gdn_s, conv_p, conv_s)
```

```python
import functools
import math

import jax
import jax.numpy as jnp
from jax import lax
from jax.experimental import pallas as pl
from jax.experimental.pallas import tpu as pltpu

F32 = jnp.float32
BF16 = jnp.bfloat16

EPS = 1e-6
ROPE_BASE = 10000.0
FFN_RES = 0.5
CONV_W = 4
CONV_HIST = CONV_W - 1
PAST_LEN = 16384

RET_HEADS = 8
RET_DK = 256
RET_DV = 512
GDN_K_HEADS = 16
GDN_V_HEADS = 32
GDN_DK = 128
GDN_DV = 128
GDN_QK = GDN_K_HEADS * GDN_DK
GDN_V = GDN_V_HEADS * GDN_DV
GDN_CONV_CH = 2 * GDN_QK + GDN_V

LANES = 128
SUBLANES = 8
VMEM_LIMIT = 56 * 1024 * 1024
COL_TILE = 1024
GLU_COL_TILE = 512


def _col_tile(n, cap):
    return max(t for t in range(LANES, cap + 1, LANES) if n % t == 0)


def _params(sem):
    return pltpu.CompilerParams(dimension_semantics=sem, vmem_limit_bytes=VMEM_LIMIT)


def _silu(x):
    return x * jax.nn.sigmoid(x)


def _dot(a, b):
    return jnp.dot(a.astype(BF16), b.astype(BF16), preferred_element_type=F32)


def _dot_nt(a, b):
    return lax.dot_general(a.astype(BF16), b.astype(BF16), (((1,), (1,)), ((), ())),
                           preferred_element_type=F32)


def _dot_tn(a, b):
    return lax.dot_general(a.astype(BF16), b.astype(BF16), (((0,), (0,)), ((), ())),
                           preferred_element_type=F32)


def _ada_kernel(cs_ref, cp_ref, w_ref, b_ref, os_ref, op_ref, as_scr, ap_scr):
    @pl.when(pl.program_id(0) == 0)
    def _():
        as_scr[...] = _silu(cs_ref[...]).astype(BF16)
        ap_scr[...] = _silu(cp_ref[...]).astype(BF16)

    w = w_ref[...].astype(BF16)
    b = b_ref[...]
    os_ref[...] = jnp.dot(as_scr[...], w, preferred_element_type=F32) + b
    op_ref[...] = jnp.dot(ap_scr[...], w, preferred_element_type=F32) + b


def _ada(cs, cp, w, b, layer, tn=COL_TILE):
    ms, d = cs.shape
    mp = cp.shape[0]
    depth, _, n = w.shape
    return pl.pallas_call(
        _ada_kernel,
        grid=(n // tn,),
        in_specs=[pl.BlockSpec((ms, d), lambda j: (0, 0)),
                  pl.BlockSpec((mp, d), lambda j: (0, 0)),
                  pl.BlockSpec((None, d, tn), lambda j: (layer, 0, j)),
                  pl.BlockSpec((None, 1, tn), lambda j: (layer, 0, j))],
        out_specs=[pl.BlockSpec((ms, tn), lambda j: (0, j)),
                   pl.BlockSpec((mp, tn), lambda j: (0, j))],
        out_shape=[jax.ShapeDtypeStruct((ms, n), F32), jax.ShapeDtypeStruct((mp, n), F32)],
        scratch_shapes=[pltpu.VMEM((ms, d), BF16), pltpu.VMEM((mp, d), BF16)],
        compiler_params=_params(("arbitrary",)),
        name="ada_table",
    )(cs, cp, w, b.reshape(depth, 1, n))


ROW_CHUNK = 128


def _mod_rows(ref, per_row, tiles_per_batch, chunk=None):
    if per_row:
        if chunk is None:
            return jnp.repeat(ref[...], per_row, axis=0)
        n = ROW_CHUNK // per_row
        return jnp.repeat(ref[pl.ds(pl.multiple_of(chunk * n, n), n), :], per_row, axis=0)
    return ref[pl.ds(pl.program_id(0) // tiles_per_batch, 1), :]


def _mod_mm_kernel(x_ref, sh_ref, sc_ref, nw_ref, *rest, glu, per_row, tiles_per_batch):
    if glu:
        wg_ref, wu_ref, o_ref, h_scr = rest
    else:
        w_ref, o_ref, h_scr = rest

    @pl.when(pl.program_id(1) == 0)
    def _():
        nw = nw_ref[...]

        def chunk(r, carry):
            start = pl.multiple_of(r * ROW_CHUNK, ROW_CHUNK)
            x = x_ref[pl.ds(start, ROW_CHUNK), :]
            y = x * lax.rsqrt(jnp.mean(x * x, -1, keepdims=True) + EPS) * nw
            sc = _mod_rows(sc_ref, per_row, tiles_per_batch, r)
            sh = _mod_rows(sh_ref, per_row, tiles_per_batch, r)
            h_scr[pl.ds(start, ROW_CHUNK), :] = (y * (1.0 + sc) + sh).astype(BF16)
            return carry

        lax.fori_loop(0, x_ref.shape[0] // ROW_CHUNK, chunk, 0)

    h = h_scr[...]
    if glu:
        g = jnp.dot(h, wg_ref[...], preferred_element_type=F32)
        u = jnp.dot(h, wu_ref[...], preferred_element_type=F32)
        o_ref[...] = (_silu(g) * u).astype(o_ref.dtype)
    else:
        o_ref[...] = jnp.dot(h, w_ref[...], preferred_element_type=F32).astype(o_ref.dtype)


def _mod_mm(x, mod, sub, nw, w, widx, *, glu, per_row, rows_per_batch, out_dtype, tm, tn, ncols=None):
    t, d = x.shape
    n = w.shape[-1] // 2 if glu else (ncols or w.shape[-1])
    nj = n // tn
    lead = (None,) * len(widx)
    tiles_per_batch = max(rows_per_batch // tm, 1)
    per_row = rows_per_batch if per_row else 0
    mrows = tm // rows_per_batch if per_row else mod.shape[0]

    def mod_spec(c):
        col = sub * 3 + c
        if per_row:
            return pl.BlockSpec((mrows, d), lambda i, j: (i, col))
        return pl.BlockSpec((mrows, d), lambda i, j: (0, col))

    in_specs = [pl.BlockSpec((tm, d), lambda i, j: (i, 0)),
                mod_spec(0), mod_spec(1),
                pl.BlockSpec((1, d), lambda i, j: (0, 0))]
    args = [x, mod, mod, nw.reshape(1, d)]
    if glu:
        in_specs += [pl.BlockSpec(lead + (d, tn), lambda i, j: widx + (0, j)),
                     pl.BlockSpec(lead + (d, tn), lambda i, j: widx + (0, j + nj))]
        args += [w, w]
    else:
        in_specs += [pl.BlockSpec(lead + (d, tn), lambda i, j: widx + (0, j))]
        args += [w]
    return pl.pallas_call(
        functools.partial(_mod_mm_kernel, glu=glu, per_row=per_row, tiles_per_batch=tiles_per_batch),
        grid=(t // tm, nj),
        in_specs=in_specs,
        out_specs=pl.BlockSpec((tm, tn), lambda i, j: (i, j)),
        out_shape=jax.ShapeDtypeStruct((t, n), out_dtype),
        scratch_shapes=[pltpu.VMEM((tm, d), BF16)],
        compiler_params=_params(("parallel", "arbitrary")),
        name="mod_mm_glu" if glu else "mod_mm",
    )(*args)


def _mm_out_kernel(a_ref, w_ref, x_ref, g_ref, nw_ref, o_ref, *, res_scale, per_row, tiles_per_batch):
    y = jnp.dot(a_ref[...].astype(BF16), w_ref[...], preferred_element_type=F32)
    yn = y * lax.rsqrt(jnp.mean(y * y, -1, keepdims=True) + EPS) * nw_ref[...]
    gate = _mod_rows(g_ref, per_row, tiles_per_batch)
    o_ref[...] = x_ref[...] + res_scale * (gate * yn)


def _mm_out(a, w, widx, x, mod, sub, nw, *, res_scale, per_row, rows_per_batch, tm):
    t, kdim = a.shape
    d = w.shape[-1]
    lead = (None,) * len(widx)
    tiles_per_batch = max(rows_per_batch // tm, 1)
    per_row = rows_per_batch if per_row else 0
    col = sub * 3 + 2
    if per_row:
        g_spec = pl.BlockSpec((tm // rows_per_batch, d), lambda i: (i, col))
    else:
        g_spec = pl.BlockSpec((mod.shape[0], d), lambda i: (0, col))
    return pl.pallas_call(
        functools.partial(_mm_out_kernel, res_scale=res_scale, per_row=per_row,
                          tiles_per_batch=tiles_per_batch),
        grid=(t // tm,),
        in_specs=[pl.BlockSpec((tm, kdim), lambda i: (i, 0)),
                  pl.BlockSpec(lead + (kdim, d), lambda i: widx + (0, 0), pipeline_mode=pl.Buffered(1)),
                  pl.BlockSpec((tm, d), lambda i: (i, 0)),
                  g_spec,
                  pl.BlockSpec((1, d), lambda i: (0, 0))],
        out_specs=pl.BlockSpec((tm, d), lambda i: (i, 0)),
        out_shape=jax.ShapeDtypeStruct((t, d), F32),
        compiler_params=_params(("parallel",)),
        name="mm_out",
    )(a, w, x, mod, nw.reshape(1, d))


def _rotate(x, cos, sin):
    half = x.shape[-1] // 2
    x1, x2 = x[:, :half], x[:, half:]
    return jnp.concatenate([x1 * cos - x2 * sin, x1 * sin + x2 * cos], axis=-1)


def _ret_kernel(dm_ref, qd_ref, kd_ref, cd_ref, cos_ref, sin_ref, q_ref, k_ref, v_ref, g_ref, *rest,
                heads, has_state, chunks):
    if has_state:
        s0_ref, o_ref, sout_ref = rest
    else:
        o_ref, sout_ref = rest
    dk, dv = RET_DK, RET_DV
    hs = range(heads)

    if has_state and chunks == 1:
        s = [s0_ref[h] for h in hs]
    else:
        @pl.when(pl.program_id(2) == 0)
        def _():
            sout_ref[...] = s0_ref[...] if has_state else jnp.zeros_like(sout_ref)

        s = [sout_ref[h] for h in hs]

    cos, sin = cos_ref[...], sin_ref[...]
    q = [_rotate(q_ref[:, h * dk:(h + 1) * dk], cos, sin) * (dk ** -0.5) for h in hs]
    k = [_rotate(k_ref[:, h * dk:(h + 1) * dk], cos, sin) for h in hs]
    v = [v_ref[:, h * dv:(h + 1) * dv].astype(BF16) for h in hs]
    scores = [_dot_nt(q[h], k[h]) * dm_ref[h] for h in hs]
    cross = [_dot(q[h], s[h]) for h in hs]
    inner = [_dot(scores[h], v[h]) for h in hs]
    upd = [_dot_tn(k[h] * jnp.concatenate([kd_ref[h]] * (dk // LANES), axis=1), v[h]) for h in hs]
    for h in hs:
        sout_ref[h] = s[h] * cd_ref[h, 0:1, 0:1] + upd[h]
        o = inner[h] + cross[h] * jnp.concatenate([qd_ref[h]] * (dv // LANES), axis=1)
        on = o * lax.rsqrt(jnp.mean(o * o, -1, keepdims=True) + EPS)
        o_ref[:, h * dv:(h + 1) * dv] = (_silu(g_ref[:, h * dv:(h + 1) * dv]) * on).astype(o_ref.dtype)


def _retention(proj, s0, pos0, batch, seq, c, heads):
    nh, dk, dv = RET_HEADS, RET_DK, RET_DV
    nc = seq // c
    ng = nh // heads
    half = dk // 2
    log_g = jnp.log1p(-jnp.exp2(-5.0 - jnp.arange(nh, dtype=F32)))
    idx = jnp.arange(c, dtype=F32)
    diff = idx[:, None] - idx[None, :]
    causal = diff >= 0
    dmask = jnp.where(causal[None], jnp.exp(log_g[:, None, None] * jnp.where(causal, diff, 0.0)[None]), 0.0)
    lanes = lambda x: jnp.broadcast_to(x[:, :, None], (nh, x.shape[1], LANES))
    q_decay = lanes(jnp.exp(log_g[:, None] * (idx[None, :] + 1.0)))
    k_decay = lanes(jnp.exp(log_g[:, None] * (c - 1.0 - idx)[None, :]))
    chunk_decay = jnp.broadcast_to(jnp.exp(log_g * c)[:, None, None], (nh, SUBLANES, LANES))
    inv = 1.0 / (ROPE_BASE ** jnp.linspace(0.0, 1.0, half, dtype=F32))
    ang = (pos0 + jnp.arange(seq)).astype(F32)[:, None] * inv[None, :]
    cos, sin = jnp.cos(ang), jnp.sin(ang)
    has_state = s0 is not None
    tokb = lambda b, g, n: b * nc + n
    in_specs = [pl.BlockSpec((heads, c, c), lambda b, g, n: (g, 0, 0)),
                pl.BlockSpec((heads, c, LANES), lambda b, g, n: (g, 0, 0)),
                pl.BlockSpec((heads, c, LANES), lambda b, g, n: (g, 0, 0)),
                pl.BlockSpec((heads, SUBLANES, LANES), lambda b, g, n: (g, 0, 0)),
                pl.BlockSpec((c, half), lambda b, g, n: (n, 0)),
                pl.BlockSpec((c, half), lambda b, g, n: (n, 0)),
                pl.BlockSpec((c, heads * dk), lambda b, g, n: (tokb(b, g, n), g)),
                pl.BlockSpec((c, heads * dk), lambda b, g, n: (tokb(b, g, n), ng + g)),
                pl.BlockSpec((c, heads * dv), lambda b, g, n: (tokb(b, g, n), ng + g)),
                pl.BlockSpec((c, heads * dv), lambda b, g, n: (tokb(b, g, n), 2 * ng + g))]
    args = [dmask, q_decay, k_decay, chunk_decay, cos, sin, proj, proj, proj, proj]
    if has_state:
        in_specs.append(pl.BlockSpec((None, heads, dk, dv), lambda b, g, n: (b, g, 0, 0)))
        args.append(s0)
    return pl.pallas_call(
        functools.partial(_ret_kernel, heads=heads, has_state=has_state, chunks=nc),
        grid=(batch, ng, nc),
        in_specs=in_specs,
        out_specs=[pl.BlockSpec((c, heads * dv), lambda b, g, n: (tokb(b, g, n), g)),
                   pl.BlockSpec((None, heads, dk, dv), lambda b, g, n: (b, g, 0, 0))],
        out_shape=[jax.ShapeDtypeStruct((batch * seq, nh * dv), BF16 if c % (2 * SUBLANES) == 0 else F32),
                   jax.ShapeDtypeStruct((batch, nh, dk, dv), F32)],
        compiler_params=_params(("parallel", "parallel", "arbitrary")),
        name="retention",
    )(*args)


def _gdn_ab_weight(w_in):
    d = w_in.shape[0]
    ab = w_in[:, GDN_CONV_CH + GDN_V:]
    a = ab[:, :GDN_V_HEADS].reshape(d, GDN_K_HEADS, 2)
    b = ab[:, GDN_V_HEADS:].reshape(d, GDN_K_HEADS, 2)
    tail = jnp.concatenate([a, b, jnp.zeros((d, GDN_K_HEADS, LANES - 4), w_in.dtype)], axis=-1)
    return tail.reshape(d, GDN_K_HEADS * LANES).astype(BF16)


def _gdn_head_tables(a_log, dt_bias):
    def tab(x):
        row = jnp.concatenate([x.reshape(GDN_K_HEADS, 2), jnp.zeros((GDN_K_HEADS, LANES - 2), F32)], -1)
        return jnp.broadcast_to(row[:, None, :], (GDN_K_HEADS, SUBLANES, LANES))
    return tab(a_log), tab(dt_bias)


GDN_C = 64
GDN_SUB = 256


def _chunk_cumsum(x, c):
    pos = lax.broadcasted_iota(jnp.int32, x.shape, 0) % c
    s = 1
    while s < c:
        x = x + jnp.where(pos >= s, pltpu.roll(x, s, axis=0), 0.0)
        s *= 2
    return x


GDN_BASE = 8


def _inverse_masks(c, width):
    i = jnp.arange(c)[:, None]
    j = (jnp.arange(width) % c)[None, :]
    r = (jnp.arange(width) % (2 * c))[:, None]
    s = jnp.arange(2 * c)[None, :]
    lanes = [(i // GDN_BASE) == (j // GDN_BASE)]
    pair = [(r // c) == (s // c), (r // GDN_BASE) == (s // GDN_BASE)]
    b = GDN_BASE
    while b < c:
        lanes.append(((i // (2 * b)) == (j // (2 * b))) & ((i // b) % 2 == 1) & ((j // b) % 2 == 0))
        pair.append(((r // (2 * b)) == (s // (2 * b))) & ((r // b) % 2 == 1) & ((s // b) % 2 == 0))
        b *= 2
    causal = jnp.stack([pair[0] & (r >= s), pair[0] & (r > s)]).astype(F32)
    return jnp.stack(lanes).astype(F32), jnp.stack(pair).astype(BF16), causal


def _mask_dict(lanes_ref, pair_ref):
    masks = {"base_lanes": lanes_ref[0], "same_pair": pair_ref[0], "base_pair": pair_ref[1]}
    for n in range(1, lanes_ref.shape[0]):
        masks[f"lanes{GDN_BASE << (n - 1)}"] = lanes_ref[n]
        masks[f"pair{GDN_BASE << (n - 1)}"] = pair_ref[n + 1]
    return masks


def _pair_to_diag(p):
    width, pc = p.shape
    n = width // pc
    zero = jnp.zeros((pc, pc), p.dtype)
    return jnp.concatenate(
        [jnp.concatenate([p[i * pc:(i + 1) * pc] if j == i else zero for j in range(n)], axis=1)
         for i in range(n)], axis=0)


def _lanes_to_diag(p, masks):
    c, width = p.shape
    pb = p.astype(BF16)
    pair = jnp.concatenate([jnp.concatenate([pb[:, i * 2 * c:(i + 1) * 2 * c]] * 2, axis=0)
                            for i in range(width // (2 * c))], axis=0)
    return _pair_to_diag(pair * masks["same_pair"])


def _inverse_minus_eye_lanes(l_lbs, l_pairs, masks):
    c = l_lbs[0].shape[0]
    ps = range(len(l_lbs))
    mm = lambda a, b: jnp.dot(a.astype(BF16), b, preferred_element_type=F32)

    lb = [l_lbs[i] * masks["base_lanes"] for i in ps]
    x = [-lb[i] for i in ps]
    p = [mm(lb[i], _pair_to_diag(l_pairs[i] * masks["base_pair"])) for i in ps]
    n = 2
    while 2 * n < GDN_BASE:
        xp = [mm(jnp.concatenate([x[i], p[i]], axis=0), _lanes_to_diag(p[i], masks)) for i in ps]
        x = [x[i] + p[i] + xp[i][:c] for i in ps]
        p = [xp[i][c:] for i in ps]
        n *= 2
    xp = [mm(x[i], _lanes_to_diag(p[i], masks)) for i in ps]
    x = [x[i] + p[i] + xp[i] for i in ps]

    b = GDN_BASE
    while b < c:
        t = [l_lbs[i] * masks[f"lanes{b}"]
             + mm(x[i], _pair_to_diag(l_pairs[i] * masks[f"pair{b}"])) for i in ps]
        tx = [mm(t[i], _lanes_to_diag(x[i], masks)) for i in ps]
        x = [x[i] - t[i] - tx[i] for i in ps]
        b *= 2
    return x


def _gdn_prep_kernel(q_ref, k_ref, v_ref, ab_ref, qp_ref, kp_ref, vp_ref, wq_ref, wk_ref, wv_ref,
                     alog_ref, dt_ref, ml_ref, mp_ref, mc_ref, u_ref, w_ref, qg_ref, kgt_ref, at_ref, eg_ref,
                     bq, bk, bv, gc_scr, *, tb):
    c, sub = GDN_C, GDN_SUB
    first = pl.program_id(1) == 0

    def conv_silu(buf, prev_ref, x_ref, cw_ref):
        buf[pl.ds(0, SUBLANES), :] = jnp.where(first, 0.0, prev_ref[...])
        buf[pl.ds(SUBLANES, tb), :] = x_ref[...]
        cw = cw_ref[...]
        out = buf[pl.ds(SUBLANES - CONV_HIST, tb), :] * cw[0:1, :]
        for j in range(1, CONV_W):
            out = out + buf[pl.ds(SUBLANES - CONV_HIST + j, tb), :] * cw[j:j + 1, :]
        return _silu(out)

    q = conv_silu(bq, qp_ref, q_ref, wq_ref)
    k = conv_silu(bk, kp_ref, k_ref, wk_ref)
    v2 = conv_silu(bv, vp_ref, v_ref, wv_ref)
    q = q * lax.rsqrt(jnp.sum(q * q, -1, keepdims=True) + EPS) * (GDN_DK ** -0.5)
    k = k * lax.rsqrt(jnp.sum(k * k, -1, keepdims=True) + EPS)

    ab = ab_ref[...]
    g_all = -jnp.exp(alog_ref[0:1, :]) * jax.nn.softplus(ab + dt_ref[0:1, :])
    beta_all = jax.nn.sigmoid(ab)
    gc_all = _chunk_cumsum(g_all, c)
    gc_t = gc_all.T

    masks = _mask_dict(ml_ref, mp_ref)
    tri = mc_ref[0] > 0.0
    strict = mc_ref[1] > 0.0

    nh, nchunk, pc = tb // sub, sub // c, 2 * c
    rows = [slice(hf * sub, (hf + 1) * sub) for hf in range(nh)]
    pairs = [slice(i * pc, (i + 1) * pc) for i in range(tb // pc)]
    kb, qb = k.astype(BF16), q.astype(BF16)
    kq = [_dot_nt(kb[pr], kb[pr]) for pr in pairs] + [_dot_nt(qb[pr], kb[pr]) for pr in pairs]
    npair = sub // pc
    kk = [jnp.concatenate([kq[hf * npair + i] for i in range(npair)], axis=0) for hf in range(nh)]
    qk = [jnp.concatenate([kq[len(pairs) + hf * npair + i] for i in range(npair)], axis=0) for hf in range(nh)]
    probs = [(hf, e) for hf in range(nh) for e in range(2)]
    gcb, betab, gl, l_pair, l_lb = [], [], [], [], []
    for p, (hf, e) in enumerate(probs):
        lanes = slice(e * GDN_DV, (e + 1) * GDN_DV)
        gcb.append(jnp.broadcast_to(gc_all[rows[hf], e:e + 1], (sub, LANES)))
        betab.append(jnp.broadcast_to(beta_all[rows[hf], 2 + e:3 + e], (sub, LANES)))
        gc_scr[p] = gcb[p]
        gl.append(gc_scr[p, pl.ds(c - 1, nchunk, stride=c), :])
        gc_cols = jnp.concatenate(
            [jnp.broadcast_to(gc_t[e:e + 1, hf * sub + i * pc:hf * sub + (i + 1) * pc], (pc, pc))
             for i in range(npair)], axis=0)
        decay = jnp.where(tri, jnp.exp(jnp.where(tri, gcb[p] - gc_cols, 0.0)), 0.0)
        lp = jnp.where(strict, kk[hf] * betab[p] * decay, 0.0)
        at_ref[rows[hf], lanes] = (qk[hf] * decay).astype(BF16)
        l_pair.append(lp.astype(BF16))
        l_lb.append(jnp.concatenate([lp[2 * i * c:(2 * i + 1) * c] + lp[(2 * i + 1) * c:(2 * i + 2) * c]
                                     for i in range(npair)], axis=1))
    x_lb = _inverse_minus_eye_lanes(l_lb, l_pair, masks)
    rhs = [jnp.concatenate([v2[rows[hf], e * GDN_DV:(e + 1) * GDN_DV] * betab[p],
                            k[rows[hf]] * betab[p] * jnp.exp(gcb[p])], axis=1)
           for p, (hf, e) in enumerate(probs)]
    corr = [jnp.dot(_lanes_to_diag(x_lb[p], masks), rhs[p].astype(BF16), preferred_element_type=F32)
            for p in range(len(probs))]
    for p, (hf, e) in enumerate(probs):
        lanes = slice(e * GDN_DV, (e + 1) * GDN_DV)
        sol = rhs[p] + corr[p]
        u_ref[rows[hf], lanes] = sol[:, :GDN_DV]
        w_ref[rows[hf], lanes] = sol[:, GDN_DV:].astype(BF16)
        qg_ref[rows[hf], lanes] = (q[rows[hf]] * jnp.exp(gcb[p])).astype(BF16)
        gl_b = jnp.concatenate([jnp.broadcast_to(gl[p][i:i + 1, :], (c, LANES)) for i in range(nchunk)], axis=0)
        kg = k[rows[hf]] * jnp.exp(gl_b - gcb[p])
        kgt_ref[rows[hf], lanes] = jnp.concatenate(
            [kg[i * pc:(i + 1) * pc].T for i in range(npair)], axis=0).astype(BF16)
        eg_ref[hf * nchunk:(hf + 1) * nchunk, lanes] = jnp.exp(gl[p])


def _gdn_scan_kernel(u_ref, w_ref, qg_ref, kgt_ref, at_ref, eg_ref, z_ref, nw_ref, o_ref, sout_ref,
                     *, tb, heads):
    c = GDN_C

    @pl.when(pl.program_id(2) == 0)
    def _():
        sout_ref[...] = jnp.zeros_like(sout_ref)

    nw = nw_ref[...]
    hl = [slice(h * GDN_DV, (h + 1) * GDN_DV) for h in range(heads)]
    s = [sout_ref[h] for h in range(heads)]
    zeros = jnp.zeros((c, GDN_DV), BF16)
    for i in range(tb // c):
        rows = slice(i * c, (i + 1) * c)
        pair_rows = slice((i // 2) * 2 * c, (i // 2 + 1) * 2 * c)
        ws = [jnp.dot(jnp.concatenate([w_ref[rows, hl[h]], qg_ref[rows, hl[h]]], axis=0),
                      s[h].astype(BF16), preferred_element_type=F32) for h in range(heads)]
        v_new = [(u_ref[rows, hl[h]] - ws[h][:c]).astype(BF16) for h in range(heads)]
        pair = [jnp.concatenate([v_new[h], zeros] if i % 2 == 0 else [zeros, v_new[h]], axis=0)
                for h in range(heads)]
        for h in range(heads):
            o = ws[h][c:] + jnp.dot(at_ref[rows, hl[h]], pair[h], preferred_element_type=F32)
            s[h] = s[h] * eg_ref[i:i + 1, hl[h]] + jnp.dot(kgt_ref[pair_rows, hl[h]], pair[h],
                                                          preferred_element_type=F32)
            on = o * lax.rsqrt(jnp.mean(o * o, -1, keepdims=True) + EPS) * nw
            o_ref[rows, hl[h]] = (on * _silu(z_ref[rows, hl[h]])).astype(o_ref.dtype)
    for h in range(heads):
        sout_ref[h] = s[h]


def _gdn_prompt(proj, proj_ab, conv_w, a_log, dt_bias, norm_w, batch, seq, tb_prep=1024, tb_scan=512,
                heads=8):
    hk, dk, dv = GDN_K_HEADS, GDN_DK, GDN_DV
    assert seq % tb_prep == 0 and seq % tb_scan == 0
    t = batch * seq
    tb = tb_prep
    nt = seq // tb
    kq0 = GDN_QK // dk
    v0 = 2 * GDN_QK // (2 * dv)
    alog_tab, dt_tab = _gdn_head_tables(a_log, dt_bias)
    m_lanes, m_pair, m_causal = _inverse_masks(GDN_C, GDN_SUB)
    tok = lambda b, n, j: b * nt + n
    prev = lambda b, n, j: jnp.maximum((b * seq + n * tb) // SUBLANES - 1, 0)
    u, w, qg, kg, at, eg = pl.pallas_call(
        functools.partial(_gdn_prep_kernel, tb=tb),
        grid=(batch, nt, hk),
        in_specs=[pl.BlockSpec((tb, dk), lambda b, n, j: (tok(b, n, j), j)),
                  pl.BlockSpec((tb, dk), lambda b, n, j: (tok(b, n, j), kq0 + j)),
                  pl.BlockSpec((tb, 2 * dv), lambda b, n, j: (tok(b, n, j), v0 + j)),
                  pl.BlockSpec((tb, LANES), lambda b, n, j: (tok(b, n, j), j)),
                  pl.BlockSpec((SUBLANES, dk), lambda b, n, j: (prev(b, n, j), j)),
                  pl.BlockSpec((SUBLANES, dk), lambda b, n, j: (prev(b, n, j), kq0 + j)),
                  pl.BlockSpec((SUBLANES, 2 * dv), lambda b, n, j: (prev(b, n, j), v0 + j)),
                  pl.BlockSpec((CONV_W, dk), lambda b, n, j: (0, j)),
                  pl.BlockSpec((CONV_W, dk), lambda b, n, j: (0, kq0 + j)),
                  pl.BlockSpec((CONV_W, 2 * dv), lambda b, n, j: (0, v0 + j)),
                  pl.BlockSpec((None, SUBLANES, LANES), lambda b, n, j: (j, 0, 0)),
                  pl.BlockSpec((None, SUBLANES, LANES), lambda b, n, j: (j, 0, 0)),
                  pl.BlockSpec(m_lanes.shape, lambda b, n, j: (0, 0, 0)),
                  pl.BlockSpec(m_pair.shape, lambda b, n, j: (0, 0, 0)),
                  pl.BlockSpec(m_causal.shape, lambda b, n, j: (0, 0, 0))],
        out_specs=[pl.BlockSpec((tb, 2 * dv), lambda b, n, j: (tok(b, n, j), j))] * 5
        + [pl.BlockSpec((tb // GDN_C, 2 * dv), lambda b, n, j: (tok(b, n, j), j))],
        out_shape=[jax.ShapeDtypeStruct((t, GDN_V), F32)]
        + [jax.ShapeDtypeStruct((t, GDN_V), BF16)] * 4
        + [jax.ShapeDtypeStruct((t // GDN_C, GDN_V), F32)],
        scratch_shapes=[pltpu.VMEM((SUBLANES + tb, dk), F32),
                        pltpu.VMEM((SUBLANES + tb, dk), F32),
                        pltpu.VMEM((SUBLANES + tb, 2 * dv), F32),
                        pltpu.VMEM((2 * tb // GDN_SUB, GDN_SUB, LANES), F32)],
        compiler_params=_params(("parallel", "parallel", "parallel")),
        name="gdn_prep",
    )(proj, proj, proj, proj_ab, proj, proj, proj, conv_w, conv_w, conv_w, alog_tab, dt_tab,
      m_lanes, m_pair, m_causal)

    tb = tb_scan
    nt = seq // tb
    wide = heads * dv
    z0 = GDN_CONV_CH // wide
    blk = lambda b, g, n: (b * nt + n, g)
    return pl.pallas_call(
        functools.partial(_gdn_scan_kernel, tb=tb, heads=heads),
        grid=(batch, GDN_V_HEADS // heads, nt),
        in_specs=[pl.BlockSpec((tb, wide), blk)] * 5
        + [pl.BlockSpec((tb // GDN_C, wide), blk),
           pl.BlockSpec((tb, wide), lambda b, g, n: (b * nt + n, z0 + g)),
           pl.BlockSpec((1, dv), lambda b, g, n: (0, 0))],
        out_specs=[pl.BlockSpec((tb, wide), blk),
                   pl.BlockSpec((None, heads, dk, dv), lambda b, g, n: (b, g, 0, 0))],
        out_shape=[jax.ShapeDtypeStruct((t, GDN_V), BF16),
                   jax.ShapeDtypeStruct((batch, GDN_V_HEADS, dk, dv), F32)],
        compiler_params=_params(("parallel", "parallel", "arbitrary")),
        name="gdn_scan",
    )(u, w, qg, kg, at, eg, proj, norm_w.reshape(1, dv))


def _gdn_step_kernel(q_ref, k_ref, v_ref, z_ref, ab_ref, cq_ref, ck_ref, cv_ref, wq_ref, wk_ref, wv_ref,
                     alog_ref, dt_ref, nw_ref, s0_ref, o_ref, sout_ref, cout_ref, bq, bk, bv, g_scr, *, seq, nb):
    hv, hk, dk, dv = GDN_V_HEADS, GDN_K_HEADS, GDN_DK, GDN_DV
    n = hv * seq
    rep = hv // hk
    es = range(nb)

    def conv_silu(e, buf, c_ref, x_ref, cw_ref, col):
        width = buf.shape[-1]
        buf[e, pl.ds(SUBLANES - CONV_HIST, CONV_HIST), :] = c_ref[e]
        buf[e, pl.ds(SUBLANES, seq), :] = x_ref[e * seq:(e + 1) * seq, :]
        cout_ref[e, :, col:col + width] = buf[e, pl.ds(SUBLANES + seq - CONV_HIST, CONV_HIST), :]
        cw = cw_ref[...]
        out = buf[e, pl.ds(SUBLANES - CONV_HIST, seq), :] * cw[0:1, :]
        for j in range(1, CONV_W):
            out = out + buf[e, pl.ds(SUBLANES - CONV_HIST + j, seq), :] * cw[j:j + 1, :]
        return _silu(out)

    def stack(x, width):
        return jnp.concatenate([x[:, h * width:(h + 1) * width] for h in range(x.shape[1] // width)], axis=0)

    def per_v_head(x):
        return jnp.concatenate([x[(h // rep) * seq:(h // rep + 1) * seq] for h in range(hv)], axis=0)

    def unit(x):
        return x * lax.rsqrt(jnp.sum(x * x, -1, keepdims=True) + EPS)

    def per_head_rows(x_st, lane0):
        cols = [jnp.broadcast_to(x_st[:, lane0 + e:lane0 + e + 1], (hk * seq, LANES)) for e in range(rep)]
        return jnp.concatenate([cols[h % rep][(h // rep) * seq:(h // rep + 1) * seq] for h in range(hv)], axis=0)

    q = [per_v_head(unit(stack(conv_silu(e, bq, cq_ref, q_ref, wq_ref, 0), dk))) * (dk ** -0.5) for e in es]
    k = [per_v_head(unit(stack(conv_silu(e, bk, ck_ref, k_ref, wk_ref, GDN_QK), dk))) for e in es]
    v = [stack(conv_silu(e, bv, cv_ref, v_ref, wv_ref, 2 * GDN_QK), dv) for e in es]

    ab = [stack(ab_ref[e * seq:(e + 1) * seq, :], LANES) for e in es]
    gc_st = [_chunk_cumsum(-jnp.exp(alog_ref[...]) * jax.nn.softplus(ab[e] + dt_ref[...]), seq) for e in es]
    gcb = [per_head_rows(gc_st[e], 0) for e in es]
    betab = [per_head_rows(jax.nn.sigmoid(ab[e]), rep) for e in es]
    gc_row = [gcb[e].T[0:1, :] for e in es]
    gl = []
    for e in es:
        g_scr[e] = gcb[e]
        gl.append(g_scr[e, pl.ds(seq - 1, hv, stride=seq), :])
    eg = [jnp.exp(gl[e]) for e in es]

    r = lax.broadcasted_iota(jnp.int32, (n, n), 0)
    s = lax.broadcasted_iota(jnp.int32, (n, n), 1)
    same = (r // seq) == (s // seq)
    tri = same & (r >= s)
    strict = same & (r > s)
    wide = lambda x: jnp.concatenate([x] * (n // LANES), axis=1)
    decay = [jnp.where(tri, jnp.exp(jnp.where(tri, wide(gcb[e]) - gc_row[e], 0.0)), 0.0) for e in es]
    kq = [_dot_nt(jnp.concatenate([k[e], q[e]], axis=0), k[e]) for e in es]
    l_bd = [jnp.where(strict, kq[e][:n] * wide(betab[e]) * decay[e], 0.0) for e in es]
    a_bd = [kq[e][n:] * decay[e] for e in es]

    x = [-l_bd[e] for e in es]
    p = [_dot(l_bd[e], l_bd[e]) for e in es]
    m = 2
    while 2 * m < seq:
        xp = [_dot(jnp.concatenate([x[e], p[e]], axis=0), p[e]) for e in es]
        x = [x[e] + p[e] + xp[e][:n] for e in es]
        p = [xp[e][n:] for e in es]
        m *= 2
    xp = [_dot(x[e], p[e]) for e in es]
    x = [x[e] + p[e] + xp[e] for e in es]

    rhs = [jnp.concatenate([v[e] * betab[e], k[e] * betab[e] * jnp.exp(gcb[e])], axis=1) for e in es]
    corr = [_dot(x[e], rhs[e]) for e in es]
    sol = [rhs[e] + corr[e] for e in es]
    qg = [q[e] * jnp.exp(gcb[e]) for e in es]
    kg = []
    for e in es:
        gl_b = jnp.concatenate([jnp.broadcast_to(gl[e][h:h + 1, :], (seq, LANES)) for h in range(hv)], axis=0)
        kg.append(k[e] * jnp.exp(gl_b - gcb[e]))

    eh = [(e, h) for e in es for h in range(hv)]
    hrows = [slice(h * seq, (h + 1) * seq) for h in range(hv)]
    ws = {(e, h): _dot(jnp.concatenate([sol[e][hrows[h], dv:], qg[e][hrows[h]]], axis=0), s0_ref[e, h])
          for e, h in eh}
    v_new = {(e, h): sol[e][hrows[h], :dv] - ws[e, h][:seq] for e, h in eh}
    upd = {(e, h): _dot_tn(kg[e][hrows[h]], v_new[e, h]) for e, h in eh}
    for e, h in eh:
        sout_ref[e, h] = s0_ref[e, h] * eg[e][h:h + 1, :] + upd[e, h]
    inner = [_dot(a_bd[e], jnp.concatenate([v_new[e, h] for h in range(hv)], axis=0)) for e in es]
    for e in es:
        o = jnp.concatenate([ws[e, h][seq:] for h in range(hv)], axis=0) + inner[e]
        z = stack(z_ref[e * seq:(e + 1) * seq, :], dv)
        on = o * lax.rsqrt(jnp.mean(o * o, -1, keepdims=True) + EPS) * nw_ref[...] * _silu(z)
        for h in range(hv):
            o_ref[e * seq:(e + 1) * seq, h * dv:(h + 1) * dv] = on[h * seq:(h + 1) * seq]


def _gdn_step(proj, proj_ab, conv_w, a_log, dt_bias, norm_w, s0, buf0, batch, seq, nb=4):
    hv, hk, dk, dv = GDN_V_HEADS, GDN_K_HEADS, GDN_DK, GDN_DV
    assert seq == SUBLANES and batch % nb == 0
    qw, vw, abw = GDN_QK, GDN_V, hk * LANES
    alog_tab, dt_tab = _gdn_head_tables(a_log, dt_bias)
    alog_tab = alog_tab.reshape(hk * seq, LANES)
    dt_tab = dt_tab.reshape(hk * seq, LANES)
    return pl.pallas_call(
        functools.partial(_gdn_step_kernel, seq=seq, nb=nb),
        grid=(batch // nb,),
        in_specs=[pl.BlockSpec((nb * seq, qw), lambda b: (b, 0)),
                  pl.BlockSpec((nb * seq, qw), lambda b: (b, 1)),
                  pl.BlockSpec((nb * seq, vw), lambda b: (b, 2 * qw // vw)),
                  pl.BlockSpec((nb * seq, vw), lambda b: (b, GDN_CONV_CH // vw)),
                  pl.BlockSpec((nb * seq, abw), lambda b: (b, 0)),
                  pl.BlockSpec((nb, CONV_W - 1, qw), lambda b: (b, 0, 0)),
                  pl.BlockSpec((nb, CONV_W - 1, qw), lambda b: (b, 0, 1)),
                  pl.BlockSpec((nb, CONV_W - 1, vw), lambda b: (b, 0, 2 * qw // vw)),
                  pl.BlockSpec((CONV_W, qw), lambda b: (0, 0)),
                  pl.BlockSpec((CONV_W, qw), lambda b: (0, 1)),
                  pl.BlockSpec((CONV_W, vw), lambda b: (0, 2 * qw // vw)),
                  pl.BlockSpec((hk * seq, LANES), lambda b: (0, 0)),
                  pl.BlockSpec((hk * seq, LANES), lambda b: (0, 0)),
                  pl.BlockSpec((1, dv), lambda b: (0, 0)),
                  pl.BlockSpec((nb, hv, dk, dv), lambda b: (b, 0, 0, 0))],
        out_specs=[pl.BlockSpec((nb * seq, vw), lambda b: (b, 0)),
                   pl.BlockSpec((nb, hv, dk, dv), lambda b: (b, 0, 0, 0)),
                   pl.BlockSpec((nb, CONV_W - 1, GDN_CONV_CH), lambda b: (b, 0, 0))],
        out_shape=[jax.ShapeDtypeStruct((batch * seq, GDN_V), F32),
                   jax.ShapeDtypeStruct((batch, hv, dk, dv), F32),
                   jax.ShapeDtypeStruct((batch, CONV_W - 1, GDN_CONV_CH), F32)],
        scratch_shapes=[pltpu.VMEM((nb, 2 * SUBLANES, qw), F32),
                        pltpu.VMEM((nb, 2 * SUBLANES, qw), F32),
                        pltpu.VMEM((nb, 2 * SUBLANES, vw), F32),
                        pltpu.VMEM((nb, hv * seq, LANES), F32)],
        compiler_params=_params(("parallel",)),
        name="gdn_step",
    )(proj, proj, proj, proj, proj_ab, buf0, buf0, buf0, conv_w, conv_w, conv_w, alog_tab, dt_tab,
      norm_w.reshape(1, dv), s0)


def _trunk(x, mod_fn, per_row, batch, seq, pos0, ret_state, gdn_state, conv_state, wts, tm):
    (norm_pre, norm_post, w_gu, w_down, ret_w_in, ret_w_out, gdn_w_in, gdn_w_ab, gdn_conv_w, gdn_a_log,
     gdn_dt_bias, gdn_norm_w, gdn_w_out) = wts
    kw = dict(per_row=per_row, rows_per_batch=seq)
    depth = norm_pre.shape[0]
    new_ret, new_gdn, new_conv = [], [], []
    for i in range(depth):
        mod = mod_fn(i)

        def ffn(x, sub, f):
            a = _mod_mm(x, mod, sub, norm_pre[i, sub], w_gu, (i, f), glu=True, out_dtype=BF16,
                        tm=tm, tn=_col_tile(w_gu.shape[-1] // 2, GLU_COL_TILE), **kw)
            return _mm_out(a, w_down, (i, f), x, mod, sub, norm_post[i, sub], res_scale=FFN_RES,
                           tm=256, **kw)

        x = ffn(x, 0, 0)
        r = i // 2
        if i % 2 == 0:
            proj = _mod_mm(x, mod, 1, norm_pre[i, 1], ret_w_in, (r,), glu=False, out_dtype=F32,
                           tm=tm, tn=_col_tile(ret_w_in.shape[-1], COL_TILE), **kw)
            y, s = _retention(proj, None if ret_state is None else ret_state[r], pos0, batch, seq,
                              c=math.gcd(256, seq), heads=4 if seq >= 256 else RET_HEADS)
            new_ret.append(s)
            x = _mm_out(y, ret_w_out, (r,), x, mod, 1, norm_post[i, 1], res_scale=1.0,
                        tm=256, **kw)
        else:
            wide = GDN_CONV_CH + GDN_V
            proj = _mod_mm(x, mod, 1, norm_pre[i, 1], gdn_w_in, (r,), glu=False, out_dtype=F32,
                           tm=tm, tn=_col_tile(wide, COL_TILE), ncols=wide, **kw)
            proj_ab = _mod_mm(x, mod, 1, norm_pre[i, 1], gdn_w_ab, (r,), glu=False, out_dtype=F32,
                              tm=tm, tn=_col_tile(gdn_w_ab.shape[-1], COL_TILE), **kw)
            if gdn_state is None:
                y, s = _gdn_prompt(proj, proj_ab, gdn_conv_w[r], gdn_a_log[r], gdn_dt_bias[r],
                                   gdn_norm_w[r], batch, seq)
                cs = proj.reshape(batch, seq, -1)[:, seq - (CONV_W - 1):, :GDN_CONV_CH]
            else:
                y, s, cs = _gdn_step(proj, proj_ab, gdn_conv_w[r], gdn_a_log[r], gdn_dt_bias[r],
                                     gdn_norm_w[r], gdn_state[r], conv_state[r], batch, seq)
            new_gdn.append(s)
            new_conv.append(cs)
            x = _mm_out(y, gdn_w_out, (r,), x, mod, 1, norm_post[i, 1], res_scale=1.0,
                        tm=256, **kw)
        x = ffn(x, 2, 1)
    stack = lambda xs: xs[0][None] if len(xs) == 1 else jnp.stack(xs)
    return x, stack(new_ret), stack(new_gdn), stack(new_conv)


def kernel(x_prompt, x_sample, c_prompt, c_sample, state_ret, state_gdn, state_conv, w_ada, b_ada,
           norm_pre, norm_post, ffn_w_gu, ffn_w_down, ret_w_in, ret_w_out, gdn_w_in, gdn_conv_w,
           gdn_a_log, gdn_dt_bias, gdn_norm_w, gdn_w_out):
    bp, lp, d = x_prompt.shape
    bs, ls, _ = x_sample.shape
    wts = (norm_pre, norm_post, ffn_w_gu.astype(BF16), ffn_w_down.astype(BF16),
           ret_w_in.astype(BF16), ret_w_out.astype(BF16),
           gdn_w_in.astype(BF16), jax.vmap(_gdn_ab_weight)(gdn_w_in),
           gdn_conv_w, gdn_a_log, gdn_dt_bias, gdn_norm_w, gdn_w_out.astype(BF16))

    cs_rows = c_sample
    cp_rows = jnp.pad(c_prompt, ((0, 2 * SUBLANES - bp), (0, 0)))
    mods = [_ada(cs_rows, cp_rows, w_ada, b_ada, i) for i in range(w_ada.shape[0])]

    y_p, ret_p, gdn_p, conv_p = _trunk(
        x_prompt.reshape(bp * lp, d), lambda i: mods[i][1], False, bp, lp, 0,
        None, None, None, wts, tm=1024)
    y_s, ret_s, gdn_s, conv_s = _trunk(
        x_sample.reshape(bs * ls, d), lambda i: mods[i][0], True, bs, ls, PAST_LEN,
        state_ret, state_gdn, state_conv, wts, tm=1024)
    return (y_p.reshape(bp, lp, d), y_s.reshape(bs, ls, d), ret_p, ret_s, gdn_p, gdn_s, conv_p, conv_s)
```

```python
import functools
import math

import jax
import jax.numpy as jnp
from jax import lax
from jax.experimental import pallas as pl
from jax.experimental.pallas import tpu as pltpu

F32 = jnp.float32
BF16 = jnp.bfloat16

EPS = 1e-6
ROPE_BASE = 10000.0
FFN_RES = 0.5
CONV_W = 4
CONV_HIST = CONV_W - 1
PAST_LEN = 16384

RET_HEADS = 8
RET_DK = 256
RET_DV = 512
GDN_K_HEADS = 16
GDN_V_HEADS = 32
GDN_DK = 128
GDN_DV = 128
GDN_QK = GDN_K_HEADS * GDN_DK
GDN_V = GDN_V_HEADS * GDN_DV
GDN_CONV_CH = 2 * GDN_QK + GDN_V

LANES = 128
SUBLANES = 8
VMEM_LIMIT = 56 * 1024 * 1024
COL_TILE = 1024
GLU_COL_TILE = 512


def _col_tile(n, cap):
    return max(t for t in range(LANES, cap + 1, LANES) if n % t == 0)


def _params(sem):
    return pltpu.CompilerParams(dimension_semantics=sem, vmem_limit_bytes=VMEM_LIMIT)


def _silu(x):
    return x * jax.nn.sigmoid(x)


def _dot(a, b):
    return jnp.dot(a.astype(BF16), b.astype(BF16), preferred_element_type=F32)


def _dot_nt(a, b):
    return lax.dot_general(a.astype(BF16), b.astype(BF16), (((1,), (1,)), ((), ())),
                           preferred_element_type=F32)


def _dot_tn(a, b):
    return lax.dot_general(a.astype(BF16), b.astype(BF16), (((0,), (0,)), ((), ())),
                           preferred_element_type=F32)


def _ada_kernel(cs_ref, cp_ref, w_ref, b_ref, os_ref, op_ref, as_scr, ap_scr):
    @pl.when(pl.program_id(0) == 0)
    def _():
        as_scr[...] = _silu(cs_ref[...]).astype(BF16)
        ap_scr[...] = _silu(cp_ref[...]).astype(BF16)

    w = w_ref[...].astype(BF16)
    b = b_ref[...]
    os_ref[...] = jnp.dot(as_scr[...], w, preferred_element_type=F32) + b
    op_ref[...] = jnp.dot(ap_scr[...], w, preferred_element_type=F32) + b


def _ada(cs, cp, w, b, layer, tn=COL_TILE):
    ms, d = cs.shape
    mp = cp.shape[0]
    depth, _, n = w.shape
    return pl.pallas_call(
        _ada_kernel,
        grid=(n // tn,),
        in_specs=[pl.BlockSpec((ms, d), lambda j: (0, 0)),
                  pl.BlockSpec((mp, d), lambda j: (0, 0)),
                  pl.BlockSpec((None, d, tn), lambda j: (layer, 0, j)),
                  pl.BlockSpec((None, 1, tn), lambda j: (layer, 0, j))],
        out_specs=[pl.BlockSpec((ms, tn), lambda j: (0, j)),
                   pl.BlockSpec((mp, tn), lambda j: (0, j))],
        out_shape=[jax.ShapeDtypeStruct((ms, n), F32), jax.ShapeDtypeStruct((mp, n), F32)],
        scratch_shapes=[pltpu.VMEM((ms, d), BF16), pltpu.VMEM((mp, d), BF16)],
        compiler_params=_params(("arbitrary",)),
        name="ada_table",
    )(cs, cp, w, b.reshape(depth, 1, n))


ROW_CHUNK = 128


def _mod_rows(ref, per_row, tiles_per_batch, chunk=None):
    if per_row:
        if chunk is None:
            return jnp.repeat(ref[...], per_row, axis=0)
        n = ROW_CHUNK // per_row
        return jnp.repeat(ref[pl.ds(pl.multiple_of(chunk * n, n), n), :], per_row, axis=0)
    return ref[pl.ds(pl.program_id(0) // tiles_per_batch, 1), :]


def _mod_mm_kernel(x_ref, sh_ref, sc_ref, nw_ref, *rest, glu, per_row, tiles_per_batch):
    if glu:
        wg_ref, wu_ref, o_ref, h_scr = rest
    else:
        w_ref, o_ref, h_scr = rest

    @pl.when(pl.program_id(1) == 0)
    def _():
        nw = nw_ref[...]

        def chunk(r, carry):
            start = pl.multiple_of(r * ROW_CHUNK, ROW_CHUNK)
            x = x_ref[pl.ds(start, ROW_CHUNK), :]
            y = x * lax.rsqrt(jnp.mean(x * x, -1, keepdims=True) + EPS) * nw
            sc = _mod_rows(sc_ref, per_row, tiles_per_batch, r)
            sh = _mod_rows(sh_ref, per_row, tiles_per_batch, r)
            h_scr[pl.ds(start, ROW_CHUNK), :] = (y * (1.0 + sc) + sh).astype(BF16)
            return carry

        lax.fori_loop(0, x_ref.shape[0] // ROW_CHUNK, chunk, 0)

    h = h_scr[...]
    if glu:
        g = jnp.dot(h, wg_ref[...], preferred_element_type=F32)
        u = jnp.dot(h, wu_ref[...], preferred_element_type=F32)
        o_ref[...] = (_silu(g) * u).astype(o_ref.dtype)
    else:
        o_ref[...] = jnp.dot(h, w_ref[...], preferred_element_type=F32).astype(o_ref.dtype)


def _mod_mm(x, mod, sub, nw, w, widx, *, glu, per_row, rows_per_batch, out_dtype, tm, tn, ncols=None):
    t, d = x.shape
    n = w.shape[-1] // 2 if glu else (ncols or w.shape[-1])
    nj = n // tn
    lead = (None,) * len(widx)
    tiles_per_batch = max(rows_per_batch // tm, 1)
    per_row = rows_per_batch if per_row else 0
    mrows = tm // rows_per_batch if per_row else mod.shape[0]

    def mod_spec(c):
        col = sub * 3 + c
        if per_row:
            return pl.BlockSpec((mrows, d), lambda i, j: (i, col))
        return pl.BlockSpec((mrows, d), lambda i, j: (0, col))

    in_specs = [pl.BlockSpec((tm, d), lambda i, j: (i, 0)),
                mod_spec(0), mod_spec(1),
                pl.BlockSpec((1, d), lambda i, j: (0, 0))]
    args = [x, mod, mod, nw.reshape(1, d)]
    if glu:
        in_specs += [pl.BlockSpec(lead + (d, tn), lambda i, j: widx + (0, j)),
                     pl.BlockSpec(lead + (d, tn), lambda i, j: widx + (0, j + nj))]
        args += [w, w]
    else:
        in_specs += [pl.BlockSpec(lead + (d, tn), lambda i, j: widx + (0, j))]
        args += [w]
    return pl.pallas_call(
        functools.partial(_mod_mm_kernel, glu=glu, per_row=per_row, tiles_per_batch=tiles_per_batch),
        grid=(t // tm, nj),
        in_specs=in_specs,
        out_specs=pl.BlockSpec((tm, tn), lambda i, j: (i, j)),
        out_shape=jax.ShapeDtypeStruct((t, n), out_dtype),
        scratch_shapes=[pltpu.VMEM((tm, d), BF16)],
        compiler_params=_params(("parallel", "arbitrary")),
        name="mod_mm_glu" if glu else "mod_mm",
    )(*args)


def _mm_out_kernel(a_ref, w_ref, x_ref, g_ref, nw_ref, o_ref, *, res_scale, per_row, tiles_per_batch):
    y = jnp.dot(a_ref[...].astype(BF16), w_ref[...], preferred_element_type=F32)
    yn = y * lax.rsqrt(jnp.mean(y * y, -1, keepdims=True) + EPS) * nw_ref[...]
    gate = _mod_rows(g_ref, per_row, tiles_per_batch)
    o_ref[...] = x_ref[...] + res_scale * (gate * yn)


def _mm_out(a, w, widx, x, mod, sub, nw, *, res_scale, per_row, rows_per_batch, tm):
    t, kdim = a.shape
    d = w.shape[-1]
    lead = (None,) * len(widx)
    tiles_per_batch = max(rows_per_batch // tm, 1)
    per_row = rows_per_batch if per_row else 0
    col = sub * 3 + 2
    if per_row:
        g_spec = pl.BlockSpec((tm // rows_per_batch, d), lambda i: (i, col))
    else:
        g_spec = pl.BlockSpec((mod.shape[0], d), lambda i: (0, col))
    return pl.pallas_call(
        functools.partial(_mm_out_kernel, res_scale=res_scale, per_row=per_row,
                          tiles_per_batch=tiles_per_batch),
        grid=(t // tm,),
        in_specs=[pl.BlockSpec((tm, kdim), lambda i: (i, 0)),
                  pl.BlockSpec(lead + (kdim, d), lambda i: widx + (0, 0), pipeline_mode=pl.Buffered(1)),
                  pl.BlockSpec((tm, d), lambda i: (i, 0)),
                  g_spec,
                  pl.BlockSpec((1, d), lambda i: (0, 0))],
        out_specs=pl.BlockSpec((tm, d), lambda i: (i, 0)),
        out_shape=jax.ShapeDtypeStruct((t, d), F32),
        compiler_params=_params(("parallel",)),
        name="mm_out",
    )(a, w, x, mod, nw.reshape(1, d))


def _rotate(x, cos, sin):
    half = x.shape[-1] // 2
    x1, x2 = x[:, :half], x[:, half:]
    return jnp.concatenate([x1 * cos - x2 * sin, x1 * sin + x2 * cos], axis=-1)


def _ret_kernel(dm_ref, qd_ref, kd_ref, cd_ref, cos_ref, sin_ref, q_ref, k_ref, v_ref, g_ref, *rest,
                heads, has_state, chunks):
    if has_state:
        s0_ref, o_ref, sout_ref = rest
    else:
        o_ref, sout_ref = rest
    dk, dv = RET_DK, RET_DV
    hs = range(heads)

    if has_state and chunks == 1:
        s = [s0_ref[h] for h in hs]
    else:
        @pl.when(pl.program_id(2) == 0)
        def _():
            sout_ref[...] = s0_ref[...] if has_state else jnp.zeros_like(sout_ref)

        s = [sout_ref[h] for h in hs]

    cos, sin = cos_ref[...], sin_ref[...]
    q = [_rotate(q_ref[:, h * dk:(h + 1) * dk], cos, sin) * (dk ** -0.5) for h in hs]
    k = [_rotate(k_ref[:, h * dk:(h + 1) * dk], cos, sin) for h in hs]
    v = [v_ref[:, h * dv:(h + 1) * dv].astype(BF16) for h in hs]
    scores = [_dot_nt(q[h], k[h]) * dm_ref[h] for h in hs]
    cross = [_dot(q[h], s[h]) for h in hs]
    inner = [_dot(scores[h], v[h]) for h in hs]
    upd = [_dot_tn(k[h] * jnp.concatenate([kd_ref[h]] * (dk // LANES), axis=1), v[h]) for h in hs]
    for h in hs:
        sout_ref[h] = s[h] * cd_ref[h, 0:1, 0:1] + upd[h]
        o = inner[h] + cross[h] * jnp.concatenate([qd_ref[h]] * (dv // LANES), axis=1)
        on = o * lax.rsqrt(jnp.mean(o * o, -1, keepdims=True) + EPS)
        o_ref[:, h * dv:(h + 1) * dv] = (_silu(g_ref[:, h * dv:(h + 1) * dv]) * on).astype(o_ref.dtype)


def _retention(proj, s0, pos0, batch, seq, c, heads):
    nh, dk, dv = RET_HEADS, RET_DK, RET_DV
    nc = seq // c
    ng = nh // heads
    half = dk // 2
    log_g = jnp.log1p(-jnp.exp2(-5.0 - jnp.arange(nh, dtype=F32)))
    idx = jnp.arange(c, dtype=F32)
    diff = idx[:, None] - idx[None, :]
    causal = diff >= 0
    dmask = jnp.where(causal[None], jnp.exp(log_g[:, None, None] * jnp.where(causal, diff, 0.0)[None]), 0.0)
    lanes = lambda x: jnp.broadcast_to(x[:, :, None], (nh, x.shape[1], LANES))
    q_decay = lanes(jnp.exp(log_g[:, None] * (idx[None, :] + 1.0)))
    k_decay = lanes(jnp.exp(log_g[:, None] * (c - 1.0 - idx)[None, :]))
    chunk_decay = jnp.broadcast_to(jnp.exp(log_g * c)[:, None, None], (nh, SUBLANES, LANES))
    inv = 1.0 / (ROPE_BASE ** jnp.linspace(0.0, 1.0, half, dtype=F32))
    ang = (pos0 + jnp.arange(seq)).astype(F32)[:, None] * inv[None, :]
    cos, sin = jnp.cos(ang), jnp.sin(ang)
    has_state = s0 is not None
    tokb = lambda b, g, n: b * nc + n
    in_specs = [pl.BlockSpec((heads, c, c), lambda b, g, n: (g, 0, 0)),
                pl.BlockSpec((heads, c, LANES), lambda b, g, n: (g, 0, 0)),
                pl.BlockSpec((heads, c, LANES), lambda b, g, n: (g, 0, 0)),
                pl.BlockSpec((heads, SUBLANES, LANES), lambda b, g, n: (g, 0, 0)),
                pl.BlockSpec((c, half), lambda b, g, n: (n, 0)),
                pl.BlockSpec((c, half), lambda b, g, n: (n, 0)),
                pl.BlockSpec((c, heads * dk), lambda b, g, n: (tokb(b, g, n), g)),
                pl.BlockSpec((c, heads * dk), lambda b, g, n: (tokb(b, g, n), ng + g)),
                pl.BlockSpec((c, heads * dv), lambda b, g, n: (tokb(b, g, n), ng + g)),
                pl.BlockSpec((c, heads * dv), lambda b, g, n: (tokb(b, g, n), 2 * ng + g))]
    args = [dmask, q_decay, k_decay, chunk_decay, cos, sin, proj, proj, proj, proj]
    if has_state:
        in_specs.append(pl.BlockSpec((None, heads, dk, dv), lambda b, g, n: (b, g, 0, 0)))
        args.append(s0)
    return pl.pallas_call(
        functools.partial(_ret_kernel, heads=heads, has_state=has_state, chunks=nc),
        grid=(batch, ng, nc),
        in_specs=in_specs,
        out_specs=[pl.BlockSpec((c, heads * dv), lambda b, g, n: (tokb(b, g, n), g)),
                   pl.BlockSpec((None, heads, dk, dv), lambda b, g, n: (b, g, 0, 0))],
        out_shape=[jax.ShapeDtypeStruct((batch * seq, nh * dv), BF16 if c % (2 * SUBLANES) == 0 else F32),
                   jax.ShapeDtypeStruct((batch, nh, dk, dv), F32)],
        compiler_params=_params(("parallel", "parallel", "arbitrary")),
        name="retention",
    )(*args)


AB_GROUP = 4
AB_LANES = 4


def _gdn_ab_weight(w_in):
    d = w_in.shape[0]
    ab = w_in[:, GDN_CONV_CH + GDN_V:]
    a = ab[:, :GDN_V_HEADS].reshape(d, GDN_K_HEADS, 2)
    b = ab[:, GDN_V_HEADS:].reshape(d, GDN_K_HEADS, 2)
    groups = GDN_K_HEADS // AB_GROUP
    packed = jnp.concatenate([a, b], axis=-1).reshape(d, groups, AB_GROUP * AB_LANES)
    tail = jnp.concatenate([packed, jnp.zeros((d, groups, LANES - AB_GROUP * AB_LANES), w_in.dtype)], axis=-1)
    return tail.reshape(d, groups * LANES).astype(BF16)


def _ab_to_lane0(x, j):
    return pltpu.roll(x, (LANES - AB_LANES * (j % AB_GROUP)) % LANES, axis=1)


def _gdn_head_tables(a_log, dt_bias):
    def tab(x):
        row = jnp.concatenate([x.reshape(GDN_K_HEADS, 2), jnp.zeros((GDN_K_HEADS, LANES - 2), F32)], -1)
        return jnp.broadcast_to(row[:, None, :], (GDN_K_HEADS, SUBLANES, LANES))
    return tab(a_log), tab(dt_bias)


GDN_C = 64
GDN_SUB = 256


def _chunk_cumsum(x, c):
    pos = lax.broadcasted_iota(jnp.int32, x.shape, 0) % c
    s = 1
    while s < c:
        x = x + jnp.where(pos >= s, pltpu.roll(x, s, axis=0), 0.0)
        s *= 2
    return x


GDN_BASE = 8


def _inverse_masks(c, width):
    i = jnp.arange(c)[:, None]
    j = (jnp.arange(width) % c)[None, :]
    r = (jnp.arange(width) % (2 * c))[:, None]
    s = jnp.arange(2 * c)[None, :]
    lanes = [(i // GDN_BASE) == (j // GDN_BASE)]
    pair = [(r // c) == (s // c), (r // GDN_BASE) == (s // GDN_BASE)]
    b = GDN_BASE
    while b < c:
        lanes.append(((i // (2 * b)) == (j // (2 * b))) & ((i // b) % 2 == 1) & ((j // b) % 2 == 0))
        pair.append(((r // (2 * b)) == (s // (2 * b))) & ((r // b) % 2 == 1) & ((s // b) % 2 == 0))
        b *= 2
    causal = jnp.stack([pair[0] & (r >= s), pair[0] & (r > s)]).astype(F32)
    return jnp.stack(lanes).astype(F32), jnp.stack(pair).astype(BF16), causal


def _mask_dict(lanes_ref, pair_ref):
    masks = {"base_lanes": lanes_ref[0], "same_pair": pair_ref[0], "base_pair": pair_ref[1]}
    for n in range(1, lanes_ref.shape[0]):
        masks[f"lanes{GDN_BASE << (n - 1)}"] = lanes_ref[n]
        masks[f"pair{GDN_BASE << (n - 1)}"] = pair_ref[n + 1]
    return masks


def _pair_to_diag(p):
    width, pc = p.shape
    n = width // pc
    zero = jnp.zeros((pc, pc), p.dtype)
    return jnp.concatenate(
        [jnp.concatenate([p[i * pc:(i + 1) * pc] if j == i else zero for j in range(n)], axis=1)
         for i in range(n)], axis=0)


def _lanes_to_diag(p, masks):
    c, width = p.shape
    pb = p.astype(BF16)
    pair = jnp.concatenate([jnp.concatenate([pb[:, i * 2 * c:(i + 1) * 2 * c]] * 2, axis=0)
                            for i in range(width // (2 * c))], axis=0)
    return _pair_to_diag(pair * masks["same_pair"])


def _inverse_minus_eye_lanes(l_lbs, l_pairs, masks):
    c = l_lbs[0].shape[0]
    ps = range(len(l_lbs))
    mm = lambda a, b: jnp.dot(a.astype(BF16), b, preferred_element_type=F32)

    lb = [l_lbs[i] * masks["base_lanes"] for i in ps]
    x = [-lb[i] for i in ps]
    p = [mm(lb[i], _pair_to_diag(l_pairs[i] * masks["base_pair"])) for i in ps]
    n = 2
    while 2 * n < GDN_BASE:
        xp = [mm(jnp.concatenate([x[i], p[i]], axis=0), _lanes_to_diag(p[i], masks)) for i in ps]
        x = [x[i] + p[i] + xp[i][:c] for i in ps]
        p = [xp[i][c:] for i in ps]
        n *= 2
    xp = [mm(x[i], _lanes_to_diag(p[i], masks)) for i in ps]
    x = [x[i] + p[i] + xp[i] for i in ps]

    b = GDN_BASE
    while b < c:
        t = [l_lbs[i] * masks[f"lanes{b}"]
             + mm(x[i], _pair_to_diag(l_pairs[i] * masks[f"pair{b}"])) for i in ps]
        tx = [mm(t[i], _lanes_to_diag(x[i], masks)) for i in ps]
        x = [x[i] - t[i] - tx[i] for i in ps]
        b *= 2
    return x


def _gdn_prep_kernel(q_ref, k_ref, v_ref, ab_ref, qp_ref, kp_ref, vp_ref, wq_ref, wk_ref, wv_ref,
                     alog_ref, dt_ref, ml_ref, mp_ref, mc_ref, u_ref, w_ref, qg_ref, kgt_ref, at_ref, eg_ref,
                     bq, bk, bv, gc_scr, *, tb):
    c, sub = GDN_C, GDN_SUB
    first = pl.program_id(1) == 0

    def conv_silu(buf, prev_ref, x_ref, cw_ref):
        buf[pl.ds(0, SUBLANES), :] = jnp.where(first, 0.0, prev_ref[...])
        buf[pl.ds(SUBLANES, tb), :] = x_ref[...]
        cw = cw_ref[...]
        out = buf[pl.ds(SUBLANES - CONV_HIST, tb), :] * cw[0:1, :]
        for j in range(1, CONV_W):
            out = out + buf[pl.ds(SUBLANES - CONV_HIST + j, tb), :] * cw[j:j + 1, :]
        return _silu(out)

    q = conv_silu(bq, qp_ref, q_ref, wq_ref)
    k = conv_silu(bk, kp_ref, k_ref, wk_ref)
    v2 = conv_silu(bv, vp_ref, v_ref, wv_ref)
    q = q * lax.rsqrt(jnp.sum(q * q, -1, keepdims=True) + EPS) * (GDN_DK ** -0.5)
    k = k * lax.rsqrt(jnp.sum(k * k, -1, keepdims=True) + EPS)

    ab = _ab_to_lane0(ab_ref[...], pl.program_id(2))
    g_all = -jnp.exp(alog_ref[0:1, :]) * jax.nn.softplus(ab + dt_ref[0:1, :])
    beta_all = jax.nn.sigmoid(ab)
    gc_all = _chunk_cumsum(g_all, c)
    gc_t = gc_all.T

    masks = _mask_dict(ml_ref, mp_ref)
    tri = mc_ref[0] > 0.0
    strict = mc_ref[1] > 0.0

    nh, nchunk, pc = tb // sub, sub // c, 2 * c
    rows = [slice(hf * sub, (hf + 1) * sub) for hf in range(nh)]
    pairs = [slice(i * pc, (i + 1) * pc) for i in range(tb // pc)]
    kb, qb = k.astype(BF16), q.astype(BF16)
    kq = [_dot_nt(kb[pr], kb[pr]) for pr in pairs] + [_dot_nt(qb[pr], kb[pr]) for pr in pairs]
    npair = sub // pc
    kk = [jnp.concatenate([kq[hf * npair + i] for i in range(npair)], axis=0) for hf in range(nh)]
    qk = [jnp.concatenate([kq[len(pairs) + hf * npair + i] for i in range(npair)], axis=0) for hf in range(nh)]
    probs = [(hf, e) for hf in range(nh) for e in range(2)]
    gcb, betab, gl, l_pair, l_lb = [], [], [], [], []
    for p, (hf, e) in enumerate(probs):
        lanes = slice(e * GDN_DV, (e + 1) * GDN_DV)
        gcb.append(jnp.broadcast_to(gc_all[rows[hf], e:e + 1], (sub, LANES)))
        betab.append(jnp.broadcast_to(beta_all[rows[hf], 2 + e:3 + e], (sub, LANES)))
        gc_scr[p] = gcb[p]
        gl.append(gc_scr[p, pl.ds(c - 1, nchunk, stride=c), :])
        gc_cols = jnp.concatenate(
            [jnp.broadcast_to(gc_t[e:e + 1, hf * sub + i * pc:hf * sub + (i + 1) * pc], (pc, pc))
             for i in range(npair)], axis=0)
        decay = jnp.where(tri, jnp.exp(jnp.where(tri, gcb[p] - gc_cols, 0.0)), 0.0)
        lp = jnp.where(strict, kk[hf] * betab[p] * decay, 0.0)
        at_ref[rows[hf], lanes] = (qk[hf] * decay).astype(BF16)
        l_pair.append(lp.astype(BF16))
        l_lb.append(jnp.concatenate([lp[2 * i * c:(2 * i + 1) * c] + lp[(2 * i + 1) * c:(2 * i + 2) * c]
                                     for i in range(npair)], axis=1))
    x_lb = _inverse_minus_eye_lanes(l_lb, l_pair, masks)
    rhs = [jnp.concatenate([v2[rows[hf], e * GDN_DV:(e + 1) * GDN_DV] * betab[p],
                            k[rows[hf]] * betab[p] * jnp.exp(gcb[p])], axis=1)
           for p, (hf, e) in enumerate(probs)]
    corr = [jnp.dot(_lanes_to_diag(x_lb[p], masks), rhs[p].astype(BF16), preferred_element_type=F32)
            for p in range(len(probs))]
    for p, (hf, e) in enumerate(probs):
        lanes = slice(e * GDN_DV, (e + 1) * GDN_DV)
        sol = rhs[p] + corr[p]
        u_ref[rows[hf], lanes] = sol[:, :GDN_DV]
        w_ref[rows[hf], lanes] = sol[:, GDN_DV:].astype(BF16)
        qg_ref[rows[hf], lanes] = (q[rows[hf]] * jnp.exp(gcb[p])).astype(BF16)
        gl_b = jnp.concatenate([jnp.broadcast_to(gl[p][i:i + 1, :], (c, LANES)) for i in range(nchunk)], axis=0)
        kg = k[rows[hf]] * jnp.exp(gl_b - gcb[p])
        kgt_ref[rows[hf], lanes] = jnp.concatenate(
            [kg[i * pc:(i + 1) * pc].T for i in range(npair)], axis=0).astype(BF16)
        eg_ref[hf * nchunk:(hf + 1) * nchunk, lanes] = jnp.exp(gl[p])


def _gdn_scan_kernel(u_ref, w_ref, qg_ref, kgt_ref, at_ref, eg_ref, z_ref, nw_ref, o_ref, sout_ref,
                     *, tb, heads):
    c = GDN_C

    @pl.when(pl.program_id(2) == 0)
    def _():
        sout_ref[...] = jnp.zeros_like(sout_ref)

    nw = nw_ref[...]
    hl = [slice(h * GDN_DV, (h + 1) * GDN_DV) for h in range(heads)]
    s = [sout_ref[h] for h in range(heads)]
    zeros = jnp.zeros((c, GDN_DV), BF16)
    for i in range(tb // c):
        rows = slice(i * c, (i + 1) * c)
        pair_rows = slice((i // 2) * 2 * c, (i // 2 + 1) * 2 * c)
        ws = [jnp.dot(jnp.concatenate([w_ref[rows, hl[h]], qg_ref[rows, hl[h]]], axis=0),
                      s[h].astype(BF16), preferred_element_type=F32) for h in range(heads)]
        v_new = [(u_ref[rows, hl[h]] - ws[h][:c]).astype(BF16) for h in range(heads)]
        pair = [jnp.concatenate([v_new[h], zeros] if i % 2 == 0 else [zeros, v_new[h]], axis=0)
                for h in range(heads)]
        for h in range(heads):
            o = ws[h][c:] + jnp.dot(at_ref[rows, hl[h]], pair[h], preferred_element_type=F32)
            s[h] = s[h] * eg_ref[i:i + 1, hl[h]] + jnp.dot(kgt_ref[pair_rows, hl[h]], pair[h],
                                                          preferred_element_type=F32)
            on = o * lax.rsqrt(jnp.mean(o * o, -1, keepdims=True) + EPS) * nw
            o_ref[rows, hl[h]] = (on * _silu(z_ref[rows, hl[h]])).astype(o_ref.dtype)
    for h in range(heads):
        sout_ref[h] = s[h]


def _gdn_prompt(proj, proj_ab, conv_w, a_log, dt_bias, norm_w, batch, seq, tb_prep=1024, tb_scan=512,
                heads=8):
    hk, dk, dv = GDN_K_HEADS, GDN_DK, GDN_DV
    assert seq % tb_prep == 0 and seq % tb_scan == 0
    t = batch * seq
    tb = tb_prep
    nt = seq // tb
    kq0 = GDN_QK // dk
    v0 = 2 * GDN_QK // (2 * dv)
    alog_tab, dt_tab = _gdn_head_tables(a_log, dt_bias)
    m_lanes, m_pair, m_causal = _inverse_masks(GDN_C, GDN_SUB)
    tok = lambda b, n, j: b * nt + n
    prev = lambda b, n, j: jnp.maximum((b * seq + n * tb) // SUBLANES - 1, 0)
    u, w, qg, kg, at, eg = pl.pallas_call(
        functools.partial(_gdn_prep_kernel, tb=tb),
        grid=(batch, nt, hk),
        in_specs=[pl.BlockSpec((tb, dk), lambda b, n, j: (tok(b, n, j), j)),
                  pl.BlockSpec((tb, dk), lambda b, n, j: (tok(b, n, j), kq0 + j)),
                  pl.BlockSpec((tb, 2 * dv), lambda b, n, j: (tok(b, n, j), v0 + j)),
                  pl.BlockSpec((tb, LANES), lambda b, n, j: (tok(b, n, j), j // AB_GROUP)),
                  pl.BlockSpec((SUBLANES, dk), lambda b, n, j: (prev(b, n, j), j)),
                  pl.BlockSpec((SUBLANES, dk), lambda b, n, j: (prev(b, n, j), kq0 + j)),
                  pl.BlockSpec((SUBLANES, 2 * dv), lambda b, n, j: (prev(b, n, j), v0 + j)),
                  pl.BlockSpec((CONV_W, dk), lambda b, n, j: (0, j)),
                  pl.BlockSpec((CONV_W, dk), lambda b, n, j: (0, kq0 + j)),
                  pl.BlockSpec((CONV_W, 2 * dv), lambda b, n, j: (0, v0 + j)),
                  pl.BlockSpec((None, SUBLANES, LANES), lambda b, n, j: (j, 0, 0)),
                  pl.BlockSpec((None, SUBLANES, LANES), lambda b, n, j: (j, 0, 0)),
                  pl.BlockSpec(m_lanes.shape, lambda b, n, j: (0, 0, 0)),
                  pl.BlockSpec(m_pair.shape, lambda b, n, j: (0, 0, 0)),
                  pl.BlockSpec(m_causal.shape, lambda b, n, j: (0, 0, 0))],
        out_specs=[pl.BlockSpec((tb, 2 * dv), lambda b, n, j: (tok(b, n, j), j))] * 5
        + [pl.BlockSpec((tb // GDN_C, 2 * dv), lambda b, n, j: (tok(b, n, j), j))],
        out_shape=[jax.ShapeDtypeStruct((t, GDN_V), F32)]
        + [jax.ShapeDtypeStruct((t, GDN_V), BF16)] * 4
        + [jax.ShapeDtypeStruct((t // GDN_C, GDN_V), F32)],
        scratch_shapes=[pltpu.VMEM((SUBLANES + tb, dk), F32),
                        pltpu.VMEM((SUBLANES + tb, dk), F32),
                        pltpu.VMEM((SUBLANES + tb, 2 * dv), F32),
                        pltpu.VMEM((2 * tb // GDN_SUB, GDN_SUB, LANES), F32)],
        compiler_params=_params(("parallel", "parallel", "parallel")),
        name="gdn_prep",
    )(proj, proj, proj, proj_ab, proj, proj, proj, conv_w, conv_w, conv_w, alog_tab, dt_tab,
      m_lanes, m_pair, m_causal)

    tb = tb_scan
    nt = seq // tb
    wide = heads * dv
    z0 = GDN_CONV_CH // wide
    blk = lambda b, g, n: (b * nt + n, g)
    return pl.pallas_call(
        functools.partial(_gdn_scan_kernel, tb=tb, heads=heads),
        grid=(batch, GDN_V_HEADS // heads, nt),
        in_specs=[pl.BlockSpec((tb, wide), blk)] * 5
        + [pl.BlockSpec((tb // GDN_C, wide), blk),
           pl.BlockSpec((tb, wide), lambda b, g, n: (b * nt + n, z0 + g)),
           pl.BlockSpec((1, dv), lambda b, g, n: (0, 0))],
        out_specs=[pl.BlockSpec((tb, wide), blk),
                   pl.BlockSpec((None, heads, dk, dv), lambda b, g, n: (b, g, 0, 0))],
        out_shape=[jax.ShapeDtypeStruct((t, GDN_V), BF16),
                   jax.ShapeDtypeStruct((batch, GDN_V_HEADS, dk, dv), F32)],
        compiler_params=_params(("parallel", "parallel", "arbitrary")),
        name="gdn_scan",
    )(u, w, qg, kg, at, eg, proj, norm_w.reshape(1, dv))


def _gdn_step_kernel(q_ref, k_ref, v_ref, z_ref, ab_ref, cq_ref, ck_ref, cv_ref, wq_ref, wk_ref, wv_ref,
                     alog_ref, dt_ref, nw_ref, s0_ref, o_ref, sout_ref, cout_ref, bq, bk, bv, g_scr, *, seq, nb):
    hv, hk, dk, dv = GDN_V_HEADS, GDN_K_HEADS, GDN_DK, GDN_DV
    n = hv * seq
    rep = hv // hk
    es = range(nb)

    def conv_silu(e, buf, c_ref, x_ref, cw_ref, col):
        width = buf.shape[-1]
        buf[e, pl.ds(SUBLANES - CONV_HIST, CONV_HIST), :] = c_ref[e]
        buf[e, pl.ds(SUBLANES, seq), :] = x_ref[e * seq:(e + 1) * seq, :]
        cout_ref[e, :, col:col + width] = buf[e, pl.ds(SUBLANES + seq - CONV_HIST, CONV_HIST), :]
        cw = cw_ref[...]
        out = buf[e, pl.ds(SUBLANES - CONV_HIST, seq), :] * cw[0:1, :]
        for j in range(1, CONV_W):
            out = out + buf[e, pl.ds(SUBLANES - CONV_HIST + j, seq), :] * cw[j:j + 1, :]
        return _silu(out)

    def stack(x, width):
        return jnp.concatenate([x[:, h * width:(h + 1) * width] for h in range(x.shape[1] // width)], axis=0)

    def per_v_head(x):
        return jnp.concatenate([x[(h // rep) * seq:(h // rep + 1) * seq] for h in range(hv)], axis=0)

    def unit(x):
        return x * lax.rsqrt(jnp.sum(x * x, -1, keepdims=True) + EPS)

    def per_head_rows(x_st, lane0):
        cols = [jnp.broadcast_to(x_st[:, lane0 + e:lane0 + e + 1], (hk * seq, LANES)) for e in range(rep)]
        return jnp.concatenate([cols[h % rep][(h // rep) * seq:(h // rep + 1) * seq] for h in range(hv)], axis=0)

    q = [per_v_head(unit(stack(conv_silu(e, bq, cq_ref, q_ref, wq_ref, 0), dk))) * (dk ** -0.5) for e in es]
    k = [per_v_head(unit(stack(conv_silu(e, bk, ck_ref, k_ref, wk_ref, GDN_QK), dk))) for e in es]
    v = [stack(conv_silu(e, bv, cv_ref, v_ref, wv_ref, 2 * GDN_QK), dv) for e in es]

    ab = [jnp.concatenate(
        [_ab_to_lane0(ab_ref[e * seq:(e + 1) * seq, (j // AB_GROUP) * LANES:(j // AB_GROUP + 1) * LANES], j)
         for j in range(hk)], axis=0) for e in es]
    gc_st = [_chunk_cumsum(-jnp.exp(alog_ref[...]) * jax.nn.softplus(ab[e] + dt_ref[...]), seq) for e in es]
    gcb = [per_head_rows(gc_st[e], 0) for e in es]
    betab = [per_head_rows(jax.nn.sigmoid(ab[e]), rep) for e in es]
    gc_row = [gcb[e].T[0:1, :] for e in es]
    gl = []
    for e in es:
        g_scr[e] = gcb[e]
        gl.append(g_scr[e, pl.ds(seq - 1, hv, stride=seq), :])
    eg = [jnp.exp(gl[e]) for e in es]

    r = lax.broadcasted_iota(jnp.int32, (n, n), 0)
    s = lax.broadcasted_iota(jnp.int32, (n, n), 1)
    same = (r // seq) == (s // seq)
    tri = same & (r >= s)
    strict = same & (r > s)
    wide = lambda x: jnp.concatenate([x] * (n // LANES), axis=1)
    decay = [jnp.where(tri, jnp.exp(jnp.where(tri, wide(gcb[e]) - gc_row[e], 0.0)), 0.0) for e in es]
    kq = [_dot_nt(jnp.concatenate([k[e], q[e]], axis=0), k[e]) for e in es]
    l_bd = [jnp.where(strict, kq[e][:n] * wide(betab[e]) * decay[e], 0.0) for e in es]
    a_bd = [kq[e][n:] * decay[e] for e in es]

    x = [-l_bd[e] for e in es]
    p = [_dot(l_bd[e], l_bd[e]) for e in es]
    m = 2
    while 2 * m < seq:
        xp = [_dot(jnp.concatenate([x[e], p[e]], axis=0), p[e]) for e in es]
        x = [x[e] + p[e] + xp[e][:n] for e in es]
        p = [xp[e][n:] for e in es]
        m *= 2
    xp = [_dot(x[e], p[e]) for e in es]
    x = [x[e] + p[e] + xp[e] for e in es]

    rhs = [jnp.concatenate([v[e] * betab[e], k[e] * betab[e] * jnp.exp(gcb[e])], axis=1) for e in es]
    corr = [_dot(x[e], rhs[e]) for e in es]
    sol = [rhs[e] + corr[e] for e in es]
    qg = [q[e] * jnp.exp(gcb[e]) for e in es]
    kg = []
    for e in es:
        gl_b = jnp.concatenate([jnp.broadcast_to(gl[e][h:h + 1, :], (seq, LANES)) for h in range(hv)], axis=0)
        kg.append(k[e] * jnp.exp(gl_b - gcb[e]))

    eh = [(e, h) for e in es for h in range(hv)]
    hrows = [slice(h * seq, (h + 1) * seq) for h in range(hv)]
    ws = {(e, h): _dot(jnp.concatenate([sol[e][hrows[h], dv:], qg[e][hrows[h]]], axis=0), s0_ref[e, h])
          for e, h in eh}
    v_new = {(e, h): sol[e][hrows[h], :dv] - ws[e, h][:seq] for e, h in eh}
    upd = {(e, h): _dot_tn(kg[e][hrows[h]], v_new[e, h]) for e, h in eh}
    for e, h in eh:
        sout_ref[e, h] = s0_ref[e, h] * eg[e][h:h + 1, :] + upd[e, h]
    inner = [_dot(a_bd[e], jnp.concatenate([v_new[e, h] for h in range(hv)], axis=0)) for e in es]
    for e in es:
        o = jnp.concatenate([ws[e, h][seq:] for h in range(hv)], axis=0) + inner[e]
        z = stack(z_ref[e * seq:(e + 1) * seq, :], dv)
        on = o * lax.rsqrt(jnp.mean(o * o, -1, keepdims=True) + EPS) * nw_ref[...] * _silu(z)
        for h in range(hv):
            o_ref[e * seq:(e + 1) * seq, h * dv:(h + 1) * dv] = on[h * seq:(h + 1) * seq]


def _gdn_step(proj, proj_ab, conv_w, a_log, dt_bias, norm_w, s0, buf0, batch, seq, nb=4):
    hv, hk, dk, dv = GDN_V_HEADS, GDN_K_HEADS, GDN_DK, GDN_DV
    assert seq == SUBLANES and batch % nb == 0
    qw, vw, abw = GDN_QK, GDN_V, hk // AB_GROUP * LANES
    alog_tab, dt_tab = _gdn_head_tables(a_log, dt_bias)
    alog_tab = alog_tab.reshape(hk * seq, LANES)
    dt_tab = dt_tab.reshape(hk * seq, LANES)
    return pl.pallas_call(
        functools.partial(_gdn_step_kernel, seq=seq, nb=nb),
        grid=(batch // nb,),
        in_specs=[pl.BlockSpec((nb * seq, qw), lambda b: (b, 0)),
                  pl.BlockSpec((nb * seq, qw), lambda b: (b, 1)),
                  pl.BlockSpec((nb * seq, vw), lambda b: (b, 2 * qw // vw)),
                  pl.BlockSpec((nb * seq, vw), lambda b: (b, GDN_CONV_CH // vw)),
                  pl.BlockSpec((nb * seq, abw), lambda b: (b, 0)),
                  pl.BlockSpec((nb, CONV_W - 1, qw), lambda b: (b, 0, 0)),
                  pl.BlockSpec((nb, CONV_W - 1, qw), lambda b: (b, 0, 1)),
                  pl.BlockSpec((nb, CONV_W - 1, vw), lambda b: (b, 0, 2 * qw // vw)),
                  pl.BlockSpec((CONV_W, qw), lambda b: (0, 0)),
                  pl.BlockSpec((CONV_W, qw), lambda b: (0, 1)),
                  pl.BlockSpec((CONV_W, vw), lambda b: (0, 2 * qw // vw)),
                  pl.BlockSpec((hk * seq, LANES), lambda b: (0, 0)),
                  pl.BlockSpec((hk * seq, LANES), lambda b: (0, 0)),
                  pl.BlockSpec((1, dv), lambda b: (0, 0)),
                  pl.BlockSpec((nb, hv, dk, dv), lambda b: (b, 0, 0, 0))],
        out_specs=[pl.BlockSpec((nb * seq, vw), lambda b: (b, 0)),
                   pl.BlockSpec((nb, hv, dk, dv), lambda b: (b, 0, 0, 0)),
                   pl.BlockSpec((nb, CONV_W - 1, GDN_CONV_CH), lambda b: (b, 0, 0))],
        out_shape=[jax.ShapeDtypeStruct((batch * seq, GDN_V), F32),
                   jax.ShapeDtypeStruct((batch, hv, dk, dv), F32),
                   jax.ShapeDtypeStruct((batch, CONV_W - 1, GDN_CONV_CH), F32)],
        scratch_shapes=[pltpu.VMEM((nb, 2 * SUBLANES, qw), F32),
                        pltpu.VMEM((nb, 2 * SUBLANES, qw), F32),
                        pltpu.VMEM((nb, 2 * SUBLANES, vw), F32),
                        pltpu.VMEM((nb, hv * seq, LANES), F32)],
        compiler_params=_params(("parallel",)),
        name="gdn_step",
    )(proj, proj, proj, proj, proj_ab, buf0, buf0, buf0, conv_w, conv_w, conv_w, alog_tab, dt_tab,
      norm_w.reshape(1, dv), s0)


def _trunk(x, mod_fn, per_row, batch, seq, pos0, ret_state, gdn_state, conv_state, wts, tm):
    (norm_pre, norm_post, w_gu, w_down, ret_w_in, ret_w_out, gdn_w_in, gdn_w_ab, gdn_conv_w, gdn_a_log,
     gdn_dt_bias, gdn_norm_w, gdn_w_out) = wts
    kw = dict(per_row=per_row, rows_per_batch=seq)
    depth = norm_pre.shape[0]
    new_ret, new_gdn, new_conv = [], [], []
    for i in range(depth):
        mod = mod_fn(i)

        def ffn(x, sub, f):
            a = _mod_mm(x, mod, sub, norm_pre[i, sub], w_gu, (i, f), glu=True, out_dtype=BF16,
                        tm=tm, tn=_col_tile(w_gu.shape[-1] // 2, GLU_COL_TILE), **kw)
            return _mm_out(a, w_down, (i, f), x, mod, sub, norm_post[i, sub], res_scale=FFN_RES,
                           tm=256, **kw)

        x = ffn(x, 0, 0)
        r = i // 2
        if i % 2 == 0:
            proj = _mod_mm(x, mod, 1, norm_pre[i, 1], ret_w_in, (r,), glu=False, out_dtype=F32,
                           tm=tm, tn=_col_tile(ret_w_in.shape[-1], COL_TILE), **kw)
            y, s = _retention(proj, None if ret_state is None else ret_state[r], pos0, batch, seq,
                              c=math.gcd(256, seq), heads=4 if seq >= 256 else RET_HEADS)
            new_ret.append(s)
            x = _mm_out(y, ret_w_out, (r,), x, mod, 1, norm_post[i, 1], res_scale=1.0,
                        tm=256, **kw)
        else:
            wide = GDN_CONV_CH + GDN_V
            proj = _mod_mm(x, mod, 1, norm_pre[i, 1], gdn_w_in, (r,), glu=False, out_dtype=F32,
                           tm=tm, tn=_col_tile(wide, COL_TILE), ncols=wide, **kw)
            proj_ab = _mod_mm(x, mod, 1, norm_pre[i, 1], gdn_w_ab, (r,), glu=False, out_dtype=F32,
                              tm=tm, tn=_col_tile(gdn_w_ab.shape[-1], COL_TILE), **kw)
            if gdn_state is None:
                y, s = _gdn_prompt(proj, proj_ab, gdn_conv_w[r], gdn_a_log[r], gdn_dt_bias[r],
                                   gdn_norm_w[r], batch, seq)
                cs = proj.reshape(batch, seq, -1)[:, seq - (CONV_W - 1):, :GDN_CONV_CH]
            else:
                y, s, cs = _gdn_step(proj, proj_ab, gdn_conv_w[r], gdn_a_log[r], gdn_dt_bias[r],
                                     gdn_norm_w[r], gdn_state[r], conv_state[r], batch, seq)
            new_gdn.append(s)
            new_conv.append(cs)
            x = _mm_out(y, gdn_w_out, (r,), x, mod, 1, norm_post[i, 1], res_scale=1.0,
                        tm=256, **kw)
        x = ffn(x, 2, 1)
    stack = lambda xs: xs[0][None] if len(xs) == 1 else jnp.stack(xs)
    return x, stack(new_ret), stack(new_gdn), stack(new_conv)


def kernel(x_prompt, x_sample, c_prompt, c_sample, state_ret, state_gdn, state_conv, w_ada, b_ada,
           norm_pre, norm_post, ffn_w_gu, ffn_w_down, ret_w_in, ret_w_out, gdn_w_in, gdn_conv_w,
           gdn_a_log, gdn_dt_bias, gdn_norm_w, gdn_w_out):
    bp, lp, d = x_prompt.shape
    bs, ls, _ = x_sample.shape
    wts = (norm_pre, norm_post, ffn_w_gu.astype(BF16), ffn_w_down.astype(BF16),
           ret_w_in.astype(BF16), ret_w_out.astype(BF16),
           gdn_w_in.astype(BF16), jax.vmap(_gdn_ab_weight)(gdn_w_in),
           gdn_conv_w, gdn_a_log, gdn_dt_bias, gdn_norm_w, gdn_w_out.astype(BF16))

    cs_rows = c_sample
    cp_rows = jnp.pad(c_prompt, ((0, 2 * SUBLANES - bp), (0, 0)))
    mods = [_ada(cs_rows, cp_rows, w_ada, b_ada, i) for i in range(w_ada.shape[0])]

    y_p, ret_p, gdn_p, conv_p = _trunk(
        x_prompt.reshape(bp * lp, d), lambda i: mods[i][1], False, bp, lp, 0,
        None, None, None, wts, tm=1024)
    y_s, ret_s, gdn_s, conv_s = _trunk(
        x_sample.reshape(bs * ls, d), lambda i: mods[i][0], True, bs, ls, PAST_LEN,
        state_ret, state_gdn, state_conv, wts, tm=1024)
    return (y_p.reshape(bp, lp, d), y_s.reshape(bs, ls, d), ret_p, ret_s, gdn_p, gdn_s, conv_p, conv_s)
```

```python
import functools
import math

import jax
import jax.numpy as jnp
from jax import lax
from jax.experimental import pallas as pl
from jax.experimental.pallas import tpu as pltpu

F32 = jnp.float32
BF16 = jnp.bfloat16

EPS = 1e-6
ROPE_BASE = 10000.0
FFN_RES = 0.5
CONV_W = 4
CONV_HIST = CONV_W - 1
PAST_LEN = 16384

RET_HEADS = 8
RET_DK = 256
RET_DV = 512
GDN_K_HEADS = 16
GDN_V_HEADS = 32
GDN_DK = 128
GDN_DV = 128
GDN_QK = GDN_K_HEADS * GDN_DK
GDN_V = GDN_V_HEADS * GDN_DV
GDN_CONV_CH = 2 * GDN_QK + GDN_V

LANES = 128
SUBLANES = 8
VMEM_LIMIT = 56 * 1024 * 1024
COL_TILE = 1024
GLU_COL_TILE = 512


def _col_tile(n, cap):
    return max(t for t in range(LANES, cap + 1, LANES) if n % t == 0)


def _params(sem):
    return pltpu.CompilerParams(dimension_semantics=sem, vmem_limit_bytes=VMEM_LIMIT)


def _silu(x):
    return x * jax.nn.sigmoid(x)


def _dot(a, b):
    return jnp.dot(a.astype(BF16), b.astype(BF16), preferred_element_type=F32)


def _dot_nt(a, b):
    return lax.dot_general(a.astype(BF16), b.astype(BF16), (((1,), (1,)), ((), ())),
                           preferred_element_type=F32)


def _dot_tn(a, b):
    return lax.dot_general(a.astype(BF16), b.astype(BF16), (((0,), (0,)), ((), ())),
                           preferred_element_type=F32)


def _ada_kernel(cs_ref, cp_ref, w_ref, b_ref, os_ref, op_ref, as_scr, ap_scr):
    @pl.when(pl.program_id(0) == 0)
    def _():
        as_scr[...] = _silu(cs_ref[...]).astype(BF16)
        ap_scr[...] = _silu(cp_ref[...]).astype(BF16)

    w = w_ref[...].astype(BF16)
    b = b_ref[...]
    os_ref[...] = jnp.dot(as_scr[...], w, preferred_element_type=F32) + b
    op_ref[...] = jnp.dot(ap_scr[...], w, preferred_element_type=F32) + b


def _ada(cs, cp, w, b, layer, tn=COL_TILE):
    ms, d = cs.shape
    mp = cp.shape[0]
    depth, _, n = w.shape
    return pl.pallas_call(
        _ada_kernel,
        grid=(n // tn,),
        in_specs=[pl.BlockSpec((ms, d), lambda j: (0, 0)),
                  pl.BlockSpec((mp, d), lambda j: (0, 0)),
                  pl.BlockSpec((None, d, tn), lambda j: (layer, 0, j)),
                  pl.BlockSpec((None, 1, tn), lambda j: (layer, 0, j))],
        out_specs=[pl.BlockSpec((ms, tn), lambda j: (0, j)),
                   pl.BlockSpec((mp, tn), lambda j: (0, j))],
        out_shape=[jax.ShapeDtypeStruct((ms, n), F32), jax.ShapeDtypeStruct((mp, n), F32)],
        scratch_shapes=[pltpu.VMEM((ms, d), BF16), pltpu.VMEM((mp, d), BF16)],
        compiler_params=_params(("arbitrary",)),
        name="ada_table",
    )(cs, cp, w, b.reshape(depth, 1, n))


ROW_CHUNK = 128


def _mod_rows(ref, per_row, tiles_per_batch, chunk=None):
    if per_row:
        if chunk is None:
            return jnp.repeat(ref[...], per_row, axis=0)
        n = ROW_CHUNK // per_row
        return jnp.repeat(ref[pl.ds(pl.multiple_of(chunk * n, n), n), :], per_row, axis=0)
    return ref[pl.ds(pl.program_id(0) // tiles_per_batch, 1), :]


def _mod_mm_kernel(x_ref, sh_ref, sc_ref, nw_ref, *rest, glu, per_row, tiles_per_batch):
    if glu:
        wg_ref, wu_ref, o_ref, h_scr = rest
    else:
        w_ref, o_ref, h_scr = rest

    @pl.when(pl.program_id(1) == 0)
    def _():
        nw = nw_ref[...]

        def chunk(r, carry):
            start = pl.multiple_of(r * ROW_CHUNK, ROW_CHUNK)
            x = x_ref[pl.ds(start, ROW_CHUNK), :]
            y = x * lax.rsqrt(jnp.mean(x * x, -1, keepdims=True) + EPS) * nw
            sc = _mod_rows(sc_ref, per_row, tiles_per_batch, r)
            sh = _mod_rows(sh_ref, per_row, tiles_per_batch, r)
            h_scr[pl.ds(start, ROW_CHUNK), :] = (y * (1.0 + sc) + sh).astype(BF16)
            return carry

        lax.fori_loop(0, x_ref.shape[0] // ROW_CHUNK, chunk, 0)

    h = h_scr[...]
    if glu:
        g = jnp.dot(h, wg_ref[...], preferred_element_type=F32)
        u = jnp.dot(h, wu_ref[...], preferred_element_type=F32)
        o_ref[...] = (_silu(g) * u).astype(o_ref.dtype)
    else:
        o_ref[...] = jnp.dot(h, w_ref[...], preferred_element_type=F32).astype(o_ref.dtype)


def _mod_mm(x, mod, sub, nw, w, widx, *, glu, per_row, rows_per_batch, out_dtype, tm, tn, ncols=None):
    t, d = x.shape
    n = w.shape[-1] // 2 if glu else (ncols or w.shape[-1])
    nj = n // tn
    lead = (None,) * len(widx)
    tiles_per_batch = max(rows_per_batch // tm, 1)
    per_row = rows_per_batch if per_row else 0
    mrows = tm // rows_per_batch if per_row else mod.shape[0]

    def mod_spec(c):
        col = sub * 3 + c
        if per_row:
            return pl.BlockSpec((mrows, d), lambda i, j: (i, col))
        return pl.BlockSpec((mrows, d), lambda i, j: (0, col))

    in_specs = [pl.BlockSpec((tm, d), lambda i, j: (i, 0)),
                mod_spec(0), mod_spec(1),
                pl.BlockSpec((1, d), lambda i, j: (0, 0))]
    args = [x, mod, mod, nw.reshape(1, d)]
    if glu:
        in_specs += [pl.BlockSpec(lead + (d, tn), lambda i, j: widx + (0, j)),
                     pl.BlockSpec(lead + (d, tn), lambda i, j: widx + (0, j + nj))]
        args += [w, w]
    else:
        in_specs += [pl.BlockSpec(lead + (d, tn), lambda i, j: widx + (0, j))]
        args += [w]
    return pl.pallas_call(
        functools.partial(_mod_mm_kernel, glu=glu, per_row=per_row, tiles_per_batch=tiles_per_batch),
        grid=(t // tm, nj),
        in_specs=in_specs,
        out_specs=pl.BlockSpec((tm, tn), lambda i, j: (i, j)),
        out_shape=jax.ShapeDtypeStruct((t, n), out_dtype),
        scratch_shapes=[pltpu.VMEM((tm, d), BF16)],
        compiler_params=_params(("parallel", "arbitrary")),
        name="mod_mm_glu" if glu else "mod_mm",
    )(*args)


def _mm_out_kernel(a_ref, w_ref, x_ref, g_ref, nw_ref, o_ref, *, res_scale, per_row, tiles_per_batch):
    y = jnp.dot(a_ref[...].astype(BF16), w_ref[...], preferred_element_type=F32)
    yn = y * lax.rsqrt(jnp.mean(y * y, -1, keepdims=True) + EPS) * nw_ref[...]
    gate = _mod_rows(g_ref, per_row, tiles_per_batch)
    o_ref[...] = x_ref[...] + res_scale * (gate * yn)


def _mm_out(a, w, widx, x, mod, sub, nw, *, res_scale, per_row, rows_per_batch, tm):
    t, kdim = a.shape
    d = w.shape[-1]
    lead = (None,) * len(widx)
    tiles_per_batch = max(rows_per_batch // tm, 1)
    per_row = rows_per_batch if per_row else 0
    col = sub * 3 + 2
    if per_row:
        g_spec = pl.BlockSpec((tm // rows_per_batch, d), lambda i: (i, col))
    else:
        g_spec = pl.BlockSpec((mod.shape[0], d), lambda i: (0, col))
    return pl.pallas_call(
        functools.partial(_mm_out_kernel, res_scale=res_scale, per_row=per_row,
                          tiles_per_batch=tiles_per_batch),
        grid=(t // tm,),
        in_specs=[pl.BlockSpec((tm, kdim), lambda i: (i, 0)),
                  pl.BlockSpec(lead + (kdim, d), lambda i: widx + (0, 0), pipeline_mode=pl.Buffered(1)),
                  pl.BlockSpec((tm, d), lambda i: (i, 0)),
                  g_spec,
                  pl.BlockSpec((1, d), lambda i: (0, 0))],
        out_specs=pl.BlockSpec((tm, d), lambda i: (i, 0)),
        out_shape=jax.ShapeDtypeStruct((t, d), F32),
        compiler_params=_params(("parallel",)),
        name="mm_out",
    )(a, w, x, mod, nw.reshape(1, d))


def _rotate(x, cos, sin):
    half = x.shape[-1] // 2
    x1, x2 = x[:, :half], x[:, half:]
    return jnp.concatenate([x1 * cos - x2 * sin, x1 * sin + x2 * cos], axis=-1)


def _ret_kernel(dm_ref, qd_ref, kd_ref, cd_ref, cos_ref, sin_ref, q_ref, k_ref, v_ref, g_ref, *rest,
                heads, has_state, chunks, seqs):
    if has_state:
        s0_ref, o_ref, sout_ref = rest
    else:
        o_ref, sout_ref = rest
    dk, dv = RET_DK, RET_DV
    c = q_ref.shape[0] // seqs
    units = [(e, h) for e in range(seqs) for h in range(heads)]
    rows = lambda e: slice(e * c, (e + 1) * c)

    if has_state and chunks == 1:
        s = {u: s0_ref[u] for u in units}
    else:
        @pl.when(pl.program_id(2) == 0)
        def _():
            sout_ref[...] = s0_ref[...] if has_state else jnp.zeros_like(sout_ref)

        s = {u: sout_ref[u] for u in units}

    cos, sin = cos_ref[...], sin_ref[...]
    q = {(e, h): _rotate(q_ref[rows(e), h * dk:(h + 1) * dk], cos, sin) * (dk ** -0.5) for e, h in units}
    k = {(e, h): _rotate(k_ref[rows(e), h * dk:(h + 1) * dk], cos, sin) for e, h in units}
    v = {(e, h): v_ref[rows(e), h * dv:(h + 1) * dv].astype(BF16) for e, h in units}
    scores = {(e, h): _dot_nt(q[e, h], k[e, h]) * dm_ref[h] for e, h in units}
    cross = {u: _dot(q[u], s[u]) for u in units}
    inner = {u: _dot(scores[u], v[u]) for u in units}
    upd = {(e, h): _dot_tn(k[e, h] * jnp.concatenate([kd_ref[h]] * (dk // LANES), axis=1), v[e, h])
           for e, h in units}
    for e, h in units:
        sout_ref[e, h] = s[e, h] * cd_ref[h, 0:1, 0:1] + upd[e, h]
        o = inner[e, h] + cross[e, h] * jnp.concatenate([qd_ref[h]] * (dv // LANES), axis=1)
        on = o * lax.rsqrt(jnp.mean(o * o, -1, keepdims=True) + EPS)
        o_ref[rows(e), h * dv:(h + 1) * dv] = (
            _silu(g_ref[rows(e), h * dv:(h + 1) * dv]) * on).astype(o_ref.dtype)


def _retention(proj, s0, pos0, batch, seq, c, heads, seqs=1):
    nh, dk, dv = RET_HEADS, RET_DK, RET_DV
    nc = seq // c
    assert seqs == 1 or (nc == 1 and batch % seqs == 0)
    ng = nh // heads
    rb = seqs * c
    half = dk // 2
    log_g = jnp.log1p(-jnp.exp2(-5.0 - jnp.arange(nh, dtype=F32)))
    idx = jnp.arange(c, dtype=F32)
    diff = idx[:, None] - idx[None, :]
    causal = diff >= 0
    dmask = jnp.where(causal[None], jnp.exp(log_g[:, None, None] * jnp.where(causal, diff, 0.0)[None]), 0.0)
    lanes = lambda x: jnp.broadcast_to(x[:, :, None], (nh, x.shape[1], LANES))
    q_decay = lanes(jnp.exp(log_g[:, None] * (idx[None, :] + 1.0)))
    k_decay = lanes(jnp.exp(log_g[:, None] * (c - 1.0 - idx)[None, :]))
    chunk_decay = jnp.broadcast_to(jnp.exp(log_g * c)[:, None, None], (nh, SUBLANES, LANES))
    inv = 1.0 / (ROPE_BASE ** jnp.linspace(0.0, 1.0, half, dtype=F32))
    ang = (pos0 + jnp.arange(seq)).astype(F32)[:, None] * inv[None, :]
    cos, sin = jnp.cos(ang), jnp.sin(ang)
    has_state = s0 is not None
    tokb = lambda b, g, n: b * nc + n
    in_specs = [pl.BlockSpec((heads, c, c), lambda b, g, n: (g, 0, 0)),
                pl.BlockSpec((heads, c, LANES), lambda b, g, n: (g, 0, 0)),
                pl.BlockSpec((heads, c, LANES), lambda b, g, n: (g, 0, 0)),
                pl.BlockSpec((heads, SUBLANES, LANES), lambda b, g, n: (g, 0, 0)),
                pl.BlockSpec((c, half), lambda b, g, n: (n, 0)),
                pl.BlockSpec((c, half), lambda b, g, n: (n, 0)),
                pl.BlockSpec((rb, heads * dk), lambda b, g, n: (tokb(b, g, n), g)),
                pl.BlockSpec((rb, heads * dk), lambda b, g, n: (tokb(b, g, n), ng + g)),
                pl.BlockSpec((rb, heads * dv), lambda b, g, n: (tokb(b, g, n), ng + g)),
                pl.BlockSpec((rb, heads * dv), lambda b, g, n: (tokb(b, g, n), 2 * ng + g))]
    args = [dmask, q_decay, k_decay, chunk_decay, cos, sin, proj, proj, proj, proj]
    if has_state:
        in_specs.append(pl.BlockSpec((seqs, heads, dk, dv), lambda b, g, n: (b, g, 0, 0)))
        args.append(s0)
    return pl.pallas_call(
        functools.partial(_ret_kernel, heads=heads, has_state=has_state, chunks=nc, seqs=seqs),
        grid=(batch // seqs, ng, nc),
        in_specs=in_specs,
        out_specs=[pl.BlockSpec((rb, heads * dv), lambda b, g, n: (tokb(b, g, n), g)),
                   pl.BlockSpec((seqs, heads, dk, dv), lambda b, g, n: (b, g, 0, 0))],
        out_shape=[jax.ShapeDtypeStruct((batch * seq, nh * dv), BF16 if rb % (2 * SUBLANES) == 0 else F32),
                   jax.ShapeDtypeStruct((batch, nh, dk, dv), F32)],
        compiler_params=_params(("parallel", "parallel", "arbitrary")),
        name="retention",
    )(*args)


AB_GROUP = 4
AB_LANES = 4


def _gdn_ab_weight(w_in):
    d = w_in.shape[0]
    ab = w_in[:, GDN_CONV_CH + GDN_V:]
    a = ab[:, :GDN_V_HEADS].reshape(d, GDN_K_HEADS, 2)
    b = ab[:, GDN_V_HEADS:].reshape(d, GDN_K_HEADS, 2)
    groups = GDN_K_HEADS // AB_GROUP
    packed = jnp.concatenate([a, b], axis=-1).reshape(d, groups, AB_GROUP * AB_LANES)
    tail = jnp.concatenate([packed, jnp.zeros((d, groups, LANES - AB_GROUP * AB_LANES), w_in.dtype)], axis=-1)
    return tail.reshape(d, groups * LANES).astype(BF16)


def _ab_to_lane0(x, j):
    return pltpu.roll(x, (LANES - AB_LANES * (j % AB_GROUP)) % LANES, axis=1)


def _gdn_head_tables(a_log, dt_bias):
    def tab(x):
        row = jnp.concatenate([x.reshape(GDN_K_HEADS, 2), jnp.zeros((GDN_K_HEADS, LANES - 2), F32)], -1)
        return jnp.broadcast_to(row[:, None, :], (GDN_K_HEADS, SUBLANES, LANES))
    return tab(a_log), tab(dt_bias)


GDN_C = 64
GDN_SUB = 256


def _chunk_cumsum(x, c):
    pos = lax.broadcasted_iota(jnp.int32, x.shape, 0) % c
    s = 1
    while s < c:
        x = x + jnp.where(pos >= s, pltpu.roll(x, s, axis=0), 0.0)
        s *= 2
    return x


GDN_BASE = 8


def _inverse_masks(c, width):
    i = jnp.arange(c)[:, None]
    j = (jnp.arange(width) % c)[None, :]
    r = (jnp.arange(width) % (2 * c))[:, None]
    s = jnp.arange(2 * c)[None, :]
    lanes = [(i // GDN_BASE) == (j // GDN_BASE)]
    pair = [(r // c) == (s // c), (r // GDN_BASE) == (s // GDN_BASE)]
    b = GDN_BASE
    while b < c:
        lanes.append(((i // (2 * b)) == (j // (2 * b))) & ((i // b) % 2 == 1) & ((j // b) % 2 == 0))
        pair.append(((r // (2 * b)) == (s // (2 * b))) & ((r // b) % 2 == 1) & ((s // b) % 2 == 0))
        b *= 2
    causal = jnp.stack([pair[0] & (r >= s), pair[0] & (r > s)]).astype(F32)
    return jnp.stack(lanes).astype(F32), jnp.stack(pair).astype(BF16), causal


def _mask_dict(lanes_ref, pair_ref):
    masks = {"base_lanes": lanes_ref[0], "same_pair": pair_ref[0], "base_pair": pair_ref[1]}
    for n in range(1, lanes_ref.shape[0]):
        masks[f"lanes{GDN_BASE << (n - 1)}"] = lanes_ref[n]
        masks[f"pair{GDN_BASE << (n - 1)}"] = pair_ref[n + 1]
    return masks


def _pair_to_diag(p):
    width, pc = p.shape
    n = width // pc
    zero = jnp.zeros((pc, pc), p.dtype)
    return jnp.concatenate(
        [jnp.concatenate([p[i * pc:(i + 1) * pc] if j == i else zero for j in range(n)], axis=1)
         for i in range(n)], axis=0)


def _lanes_to_diag(p, masks):
    c, width = p.shape
    pb = p.astype(BF16)
    pair = jnp.concatenate([jnp.concatenate([pb[:, i * 2 * c:(i + 1) * 2 * c]] * 2, axis=0)
                            for i in range(width // (2 * c))], axis=0)
    return _pair_to_diag(pair * masks["same_pair"])


def _inverse_minus_eye_lanes(l_lbs, l_pairs, masks):
    c = l_lbs[0].shape[0]
    ps = range(len(l_lbs))
    mm = lambda a, b: jnp.dot(a.astype(BF16), b, preferred_element_type=F32)

    lb = [l_lbs[i] * masks["base_lanes"] for i in ps]
    x = [-lb[i] for i in ps]
    p = [mm(lb[i], _pair_to_diag(l_pairs[i] * masks["base_pair"])) for i in ps]
    n = 2
    while 2 * n < GDN_BASE:
        xp = [mm(jnp.concatenate([x[i], p[i]], axis=0), _lanes_to_diag(p[i], masks)) for i in ps]
        x = [x[i] + p[i] + xp[i][:c] for i in ps]
        p = [xp[i][c:] for i in ps]
        n *= 2
    xp = [mm(x[i], _lanes_to_diag(p[i], masks)) for i in ps]
    x = [x[i] + p[i] + xp[i] for i in ps]

    b = GDN_BASE
    while b < c:
        t = [l_lbs[i] * masks[f"lanes{b}"]
             + mm(x[i], _pair_to_diag(l_pairs[i] * masks[f"pair{b}"])) for i in ps]
        tx = [mm(t[i], _lanes_to_diag(x[i], masks)) for i in ps]
        x = [x[i] - t[i] - tx[i] for i in ps]
        b *= 2
    return x


def _gdn_prep_kernel(q_ref, k_ref, v_ref, ab_ref, qp_ref, kp_ref, vp_ref, wq_ref, wk_ref, wv_ref,
                     alog_ref, dt_ref, ml_ref, mp_ref, mc_ref, u_ref, w_ref, qg_ref, kgt_ref, at_ref, eg_ref,
                     bq, bk, bv, gc_scr, *, tb):
    c, sub = GDN_C, GDN_SUB
    first = pl.program_id(1) == 0

    def conv_silu(buf, prev_ref, x_ref, cw_ref):
        buf[pl.ds(0, SUBLANES), :] = jnp.where(first, 0.0, prev_ref[...])
        buf[pl.ds(SUBLANES, tb), :] = x_ref[...]
        cw = cw_ref[...]
        out = buf[pl.ds(SUBLANES - CONV_HIST, tb), :] * cw[0:1, :]
        for j in range(1, CONV_W):
            out = out + buf[pl.ds(SUBLANES - CONV_HIST + j, tb), :] * cw[j:j + 1, :]
        return _silu(out)

    q = conv_silu(bq, qp_ref, q_ref, wq_ref)
    k = conv_silu(bk, kp_ref, k_ref, wk_ref)
    v2 = conv_silu(bv, vp_ref, v_ref, wv_ref)
    q = q * lax.rsqrt(jnp.sum(q * q, -1, keepdims=True) + EPS) * (GDN_DK ** -0.5)
    k = k * lax.rsqrt(jnp.sum(k * k, -1, keepdims=True) + EPS)

    ab = _ab_to_lane0(ab_ref[...], pl.program_id(2))
    g_all = -jnp.exp(alog_ref[0:1, :]) * jax.nn.softplus(ab + dt_ref[0:1, :])
    beta_all = jax.nn.sigmoid(ab)
    gc_all = _chunk_cumsum(g_all, c)
    gc_t = gc_all.T

    masks = _mask_dict(ml_ref, mp_ref)
    tri = mc_ref[0] > 0.0
    strict = mc_ref[1] > 0.0

    nh, nchunk, pc = tb // sub, sub // c, 2 * c
    rows = [slice(hf * sub, (hf + 1) * sub) for hf in range(nh)]
    pairs = [slice(i * pc, (i + 1) * pc) for i in range(tb // pc)]
    kb, qb = k.astype(BF16), q.astype(BF16)
    kq = [_dot_nt(kb[pr], kb[pr]) for pr in pairs] + [_dot_nt(qb[pr], kb[pr]) for pr in pairs]
    npair = sub // pc
    kk = [jnp.concatenate([kq[hf * npair + i] for i in range(npair)], axis=0) for hf in range(nh)]
    qk = [jnp.concatenate([kq[len(pairs) + hf * npair + i] for i in range(npair)], axis=0) for hf in range(nh)]
    probs = [(hf, e) for hf in range(nh) for e in range(2)]
    gcb, betab, gl, l_pair, l_lb = [], [], [], [], []
    for p, (hf, e) in enumerate(probs):
        lanes = slice(e * GDN_DV, (e + 1) * GDN_DV)
        gcb.append(jnp.broadcast_to(gc_all[rows[hf], e:e + 1], (sub, LANES)))
        betab.append(jnp.broadcast_to(beta_all[rows[hf], 2 + e:3 + e], (sub, LANES)))
        gc_scr[p] = gcb[p]
        gl.append(gc_scr[p, pl.ds(c - 1, nchunk, stride=c), :])
        gc_cols = jnp.concatenate(
            [jnp.broadcast_to(gc_t[e:e + 1, hf * sub + i * pc:hf * sub + (i + 1) * pc], (pc, pc))
             for i in range(npair)], axis=0)
        decay = jnp.where(tri, jnp.exp(jnp.where(tri, gcb[p] - gc_cols, 0.0)), 0.0)
        lp = jnp.where(strict, kk[hf] * betab[p] * decay, 0.0)
        at_ref[rows[hf], lanes] = (qk[hf] * decay).astype(BF16)
        l_pair.append(lp.astype(BF16))
        l_lb.append(jnp.concatenate([lp[2 * i * c:(2 * i + 1) * c] + lp[(2 * i + 1) * c:(2 * i + 2) * c]
                                     for i in range(npair)], axis=1))
    x_lb = _inverse_minus_eye_lanes(l_lb, l_pair, masks)
    rhs = [jnp.concatenate([v2[rows[hf], e * GDN_DV:(e + 1) * GDN_DV] * betab[p],
                            k[rows[hf]] * betab[p] * jnp.exp(gcb[p])], axis=1)
           for p, (hf, e) in enumerate(probs)]
    corr = [jnp.dot(_lanes_to_diag(x_lb[p], masks), rhs[p].astype(BF16), preferred_element_type=F32)
            for p in range(len(probs))]
    for p, (hf, e) in enumerate(probs):
        lanes = slice(e * GDN_DV, (e + 1) * GDN_DV)
        sol = rhs[p] + corr[p]
        u_ref[rows[hf], lanes] = sol[:, :GDN_DV]
        w_ref[rows[hf], lanes] = sol[:, GDN_DV:].astype(BF16)
        qg_ref[rows[hf], lanes] = (q[rows[hf]] * jnp.exp(gcb[p])).astype(BF16)
        gl_b = jnp.concatenate([jnp.broadcast_to(gl[p][i:i + 1, :], (c, LANES)) for i in range(nchunk)], axis=0)
        kg = k[rows[hf]] * jnp.exp(gl_b - gcb[p])
        kgt_ref[rows[hf], lanes] = jnp.concatenate(
            [kg[i * pc:(i + 1) * pc].T for i in range(npair)], axis=0).astype(BF16)
        eg_ref[hf * nchunk:(hf + 1) * nchunk, lanes] = jnp.exp(gl[p])


def _gdn_scan_kernel(u_ref, w_ref, qg_ref, kgt_ref, at_ref, eg_ref, z_ref, nw_ref, o_ref, sout_ref,
                     *, tb, heads):
    c = GDN_C

    @pl.when(pl.program_id(2) == 0)
    def _():
        sout_ref[...] = jnp.zeros_like(sout_ref)

    nw = nw_ref[...]
    hl = [slice(h * GDN_DV, (h + 1) * GDN_DV) for h in range(heads)]
    s = [sout_ref[h] for h in range(heads)]
    zeros = jnp.zeros((c, GDN_DV), BF16)
    for i in range(tb // c):
        rows = slice(i * c, (i + 1) * c)
        pair_rows = slice((i // 2) * 2 * c, (i // 2 + 1) * 2 * c)
        ws = [jnp.dot(jnp.concatenate([w_ref[rows, hl[h]], qg_ref[rows, hl[h]]], axis=0),
                      s[h].astype(BF16), preferred_element_type=F32) for h in range(heads)]
        v_new = [(u_ref[rows, hl[h]] - ws[h][:c]).astype(BF16) for h in range(heads)]
        pair = [jnp.concatenate([v_new[h], zeros] if i % 2 == 0 else [zeros, v_new[h]], axis=0)
                for h in range(heads)]
        for h in range(heads):
            o = ws[h][c:] + jnp.dot(at_ref[rows, hl[h]], pair[h], preferred_element_type=F32)
            s[h] = s[h] * eg_ref[i:i + 1, hl[h]] + jnp.dot(kgt_ref[pair_rows, hl[h]], pair[h],
                                                          preferred_element_type=F32)
            on = o * lax.rsqrt(jnp.mean(o * o, -1, keepdims=True) + EPS) * nw
            o_ref[rows, hl[h]] = (on * _silu(z_ref[rows, hl[h]])).astype(o_ref.dtype)
    for h in range(heads):
        sout_ref[h] = s[h]


def _gdn_prompt(proj, proj_ab, conv_w, a_log, dt_bias, norm_w, batch, seq, tb_prep=1024, tb_scan=512,
                heads=8):
    hk, dk, dv = GDN_K_HEADS, GDN_DK, GDN_DV
    assert seq % tb_prep == 0 and seq % tb_scan == 0
    t = batch * seq
    tb = tb_prep
    nt = seq // tb
    kq0 = GDN_QK // dk
    v0 = 2 * GDN_QK // (2 * dv)
    alog_tab, dt_tab = _gdn_head_tables(a_log, dt_bias)
    m_lanes, m_pair, m_causal = _inverse_masks(GDN_C, GDN_SUB)
    tok = lambda b, n, j: b * nt + n
    prev = lambda b, n, j: jnp.maximum((b * seq + n * tb) // SUBLANES - 1, 0)
    u, w, qg, kg, at, eg = pl.pallas_call(
        functools.partial(_gdn_prep_kernel, tb=tb),
        grid=(batch, nt, hk),
        in_specs=[pl.BlockSpec((tb, dk), lambda b, n, j: (tok(b, n, j), j)),
                  pl.BlockSpec((tb, dk), lambda b, n, j: (tok(b, n, j), kq0 + j)),
                  pl.BlockSpec((tb, 2 * dv), lambda b, n, j: (tok(b, n, j), v0 + j)),
                  pl.BlockSpec((tb, LANES), lambda b, n, j: (tok(b, n, j), j // AB_GROUP)),
                  pl.BlockSpec((SUBLANES, dk), lambda b, n, j: (prev(b, n, j), j)),
                  pl.BlockSpec((SUBLANES, dk), lambda b, n, j: (prev(b, n, j), kq0 + j)),
                  pl.BlockSpec((SUBLANES, 2 * dv), lambda b, n, j: (prev(b, n, j), v0 + j)),
                  pl.BlockSpec((CONV_W, dk), lambda b, n, j: (0, j)),
                  pl.BlockSpec((CONV_W, dk), lambda b, n, j: (0, kq0 + j)),
                  pl.BlockSpec((CONV_W, 2 * dv), lambda b, n, j: (0, v0 + j)),
                  pl.BlockSpec((None, SUBLANES, LANES), lambda b, n, j: (j, 0, 0)),
                  pl.BlockSpec((None, SUBLANES, LANES), lambda b, n, j: (j, 0, 0)),
                  pl.BlockSpec(m_lanes.shape, lambda b, n, j: (0, 0, 0)),
                  pl.BlockSpec(m_pair.shape, lambda b, n, j: (0, 0, 0)),
                  pl.BlockSpec(m_causal.shape, lambda b, n, j: (0, 0, 0))],
        out_specs=[pl.BlockSpec((tb, 2 * dv), lambda b, n, j: (tok(b, n, j), j))] * 5
        + [pl.BlockSpec((tb // GDN_C, 2 * dv), lambda b, n, j: (tok(b, n, j), j))],
        out_shape=[jax.ShapeDtypeStruct((t, GDN_V), F32)]
        + [jax.ShapeDtypeStruct((t, GDN_V), BF16)] * 4
        + [jax.ShapeDtypeStruct((t // GDN_C, GDN_V), F32)],
        scratch_shapes=[pltpu.VMEM((SUBLANES + tb, dk), F32),
                        pltpu.VMEM((SUBLANES + tb, dk), F32),
                        pltpu.VMEM((SUBLANES + tb, 2 * dv), F32),
                        pltpu.VMEM((2 * tb // GDN_SUB, GDN_SUB, LANES), F32)],
        compiler_params=_params(("parallel", "parallel", "parallel")),
        name="gdn_prep",
    )(proj, proj, proj, proj_ab, proj, proj, proj, conv_w, conv_w, conv_w, alog_tab, dt_tab,
      m_lanes, m_pair, m_causal)

    tb = tb_scan
    nt = seq // tb
    wide = heads * dv
    z0 = GDN_CONV_CH // wide
    blk = lambda b, g, n: (b * nt + n, g)
    return pl.pallas_call(
        functools.partial(_gdn_scan_kernel, tb=tb, heads=heads),
        grid=(batch, GDN_V_HEADS // heads, nt),
        in_specs=[pl.BlockSpec((tb, wide), blk)] * 5
        + [pl.BlockSpec((tb // GDN_C, wide), blk),
           pl.BlockSpec((tb, wide), lambda b, g, n: (b * nt + n, z0 + g)),
           pl.BlockSpec((1, dv), lambda b, g, n: (0, 0))],
        out_specs=[pl.BlockSpec((tb, wide), blk),
                   pl.BlockSpec((None, heads, dk, dv), lambda b, g, n: (b, g, 0, 0))],
        out_shape=[jax.ShapeDtypeStruct((t, GDN_V), BF16),
                   jax.ShapeDtypeStruct((batch, GDN_V_HEADS, dk, dv), F32)],
        compiler_params=_params(("parallel", "parallel", "arbitrary")),
        name="gdn_scan",
    )(u, w, qg, kg, at, eg, proj, norm_w.reshape(1, dv))


def _gdn_step_kernel(q_ref, k_ref, v_ref, z_ref, ab_ref, cq_ref, ck_ref, cv_ref, wq_ref, wk_ref, wv_ref,
                     alog_ref, dt_ref, nw_ref, s0_ref, o_ref, sout_ref, cout_ref, bq, bk, bv, g_scr, *, seq, nb):
    hv, hk, dk, dv = GDN_V_HEADS, GDN_K_HEADS, GDN_DK, GDN_DV
    n = hv * seq
    rep = hv // hk
    es = range(nb)

    def conv_silu(e, buf, c_ref, x_ref, cw_ref, col):
        width = buf.shape[-1]
        buf[e, pl.ds(SUBLANES - CONV_HIST, CONV_HIST), :] = c_ref[e]
        buf[e, pl.ds(SUBLANES, seq), :] = x_ref[e * seq:(e + 1) * seq, :]
        cout_ref[e, :, col:col + width] = buf[e, pl.ds(SUBLANES + seq - CONV_HIST, CONV_HIST), :]
        cw = cw_ref[...]
        out = buf[e, pl.ds(SUBLANES - CONV_HIST, seq), :] * cw[0:1, :]
        for j in range(1, CONV_W):
            out = out + buf[e, pl.ds(SUBLANES - CONV_HIST + j, seq), :] * cw[j:j + 1, :]
        return _silu(out)

    def stack(x, width):
        return jnp.concatenate([x[:, h * width:(h + 1) * width] for h in range(x.shape[1] // width)], axis=0)

    def per_v_head(x):
        return jnp.concatenate([x[(h // rep) * seq:(h // rep + 1) * seq] for h in range(hv)], axis=0)

    def unit(x):
        return x * lax.rsqrt(jnp.sum(x * x, -1, keepdims=True) + EPS)

    def per_head_rows(x_st, lane0):
        cols = [jnp.broadcast_to(x_st[:, lane0 + e:lane0 + e + 1], (hk * seq, LANES)) for e in range(rep)]
        return jnp.concatenate([cols[h % rep][(h // rep) * seq:(h // rep + 1) * seq] for h in range(hv)], axis=0)

    q = [per_v_head(unit(stack(conv_silu(e, bq, cq_ref, q_ref, wq_ref, 0), dk))) * (dk ** -0.5) for e in es]
    k = [per_v_head(unit(stack(conv_silu(e, bk, ck_ref, k_ref, wk_ref, GDN_QK), dk))) for e in es]
    v = [stack(conv_silu(e, bv, cv_ref, v_ref, wv_ref, 2 * GDN_QK), dv) for e in es]

    ab = [jnp.concatenate(
        [_ab_to_lane0(ab_ref[e * seq:(e + 1) * seq, (j // AB_GROUP) * LANES:(j // AB_GROUP + 1) * LANES], j)
         for j in range(hk)], axis=0) for e in es]
    gc_st = [_chunk_cumsum(-jnp.exp(alog_ref[...]) * jax.nn.softplus(ab[e] + dt_ref[...]), seq) for e in es]
    gcb = [per_head_rows(gc_st[e], 0) for e in es]
    betab = [per_head_rows(jax.nn.sigmoid(ab[e]), rep) for e in es]
    gc_row = [gcb[e].T[0:1, :] for e in es]
    gl = []
    for e in es:
        g_scr[e] = gcb[e]
        gl.append(g_scr[e, pl.ds(seq - 1, hv, stride=seq), :])
    eg = [jnp.exp(gl[e]) for e in es]

    r = lax.broadcasted_iota(jnp.int32, (n, n), 0)
    s = lax.broadcasted_iota(jnp.int32, (n, n), 1)
    same = (r // seq) == (s // seq)
    tri = same & (r >= s)
    strict = same & (r > s)
    wide = lambda x: jnp.concatenate([x] * (n // LANES), axis=1)
    decay = [jnp.where(tri, jnp.exp(jnp.where(tri, wide(gcb[e]) - gc_row[e], 0.0)), 0.0) for e in es]
    kq = [_dot_nt(jnp.concatenate([k[e], q[e]], axis=0), k[e]) for e in es]
    l_bd = [jnp.where(strict, kq[e][:n] * wide(betab[e]) * decay[e], 0.0) for e in es]
    a_bd = [kq[e][n:] * decay[e] for e in es]

    x = [-l_bd[e] for e in es]
    p = [_dot(l_bd[e], l_bd[e]) for e in es]
    m = 2
    while 2 * m < seq:
        xp = [_dot(jnp.concatenate([x[e], p[e]], axis=0), p[e]) for e in es]
        x = [x[e] + p[e] + xp[e][:n] for e in es]
        p = [xp[e][n:] for e in es]
        m *= 2
    xp = [_dot(x[e], p[e]) for e in es]
    x = [x[e] + p[e] + xp[e] for e in es]

    rhs = [jnp.concatenate([v[e] * betab[e], k[e] * betab[e] * jnp.exp(gcb[e])], axis=1) for e in es]
    corr = [_dot(x[e], rhs[e]) for e in es]
    sol = [rhs[e] + corr[e] for e in es]
    qg = [q[e] * jnp.exp(gcb[e]) for e in es]
    kg = []
    for e in es:
        gl_b = jnp.concatenate([jnp.broadcast_to(gl[e][h:h + 1, :], (seq, LANES)) for h in range(hv)], axis=0)
        kg.append(k[e] * jnp.exp(gl_b - gcb[e]))

    eh = [(e, h) for e in es for h in range(hv)]
    hrows = [slice(h * seq, (h + 1) * seq) for h in range(hv)]
    ws = {(e, h): _dot(jnp.concatenate([sol[e][hrows[h], dv:], qg[e][hrows[h]]], axis=0), s0_ref[e, h])
          for e, h in eh}
    v_new = {(e, h): sol[e][hrows[h], :dv] - ws[e, h][:seq] for e, h in eh}
    upd = {(e, h): _dot_tn(kg[e][hrows[h]], v_new[e, h]) for e, h in eh}
    for e, h in eh:
        sout_ref[e, h] = s0_ref[e, h] * eg[e][h:h + 1, :] + upd[e, h]
    inner = [_dot(a_bd[e], jnp.concatenate([v_new[e, h] for h in range(hv)], axis=0)) for e in es]
    for e in es:
        o = jnp.concatenate([ws[e, h][seq:] for h in range(hv)], axis=0) + inner[e]
        z = stack(z_ref[e * seq:(e + 1) * seq, :], dv)
        on = o * lax.rsqrt(jnp.mean(o * o, -1, keepdims=True) + EPS) * nw_ref[...] * _silu(z)
        for h in range(hv):
            o_ref[e * seq:(e + 1) * seq, h * dv:(h + 1) * dv] = on[h * seq:(h + 1) * seq]


def _gdn_step(proj, proj_ab, conv_w, a_log, dt_bias, norm_w, s0, buf0, batch, seq, nb=4):
    hv, hk, dk, dv = GDN_V_HEADS, GDN_K_HEADS, GDN_DK, GDN_DV
    assert seq == SUBLANES and batch % nb == 0
    qw, vw, abw = GDN_QK, GDN_V, hk // AB_GROUP * LANES
    alog_tab, dt_tab = _gdn_head_tables(a_log, dt_bias)
    alog_tab = alog_tab.reshape(hk * seq, LANES)
    dt_tab = dt_tab.reshape(hk * seq, LANES)
    return pl.pallas_call(
        functools.partial(_gdn_step_kernel, seq=seq, nb=nb),
        grid=(batch // nb,),
        in_specs=[pl.BlockSpec((nb * seq, qw), lambda b: (b, 0)),
                  pl.BlockSpec((nb * seq, qw), lambda b: (b, 1)),
                  pl.BlockSpec((nb * seq, vw), lambda b: (b, 2 * qw // vw)),
                  pl.BlockSpec((nb * seq, vw), lambda b: (b, GDN_CONV_CH // vw)),
                  pl.BlockSpec((nb * seq, abw), lambda b: (b, 0)),
                  pl.BlockSpec((nb, CONV_W - 1, qw), lambda b: (b, 0, 0)),
                  pl.BlockSpec((nb, CONV_W - 1, qw), lambda b: (b, 0, 1)),
                  pl.BlockSpec((nb, CONV_W - 1, vw), lambda b: (b, 0, 2 * qw // vw)),
                  pl.BlockSpec((CONV_W, qw), lambda b: (0, 0)),
                  pl.BlockSpec((CONV_W, qw), lambda b: (0, 1)),
                  pl.BlockSpec((CONV_W, vw), lambda b: (0, 2 * qw // vw)),
                  pl.BlockSpec((hk * seq, LANES), lambda b: (0, 0)),
                  pl.BlockSpec((hk * seq, LANES), lambda b: (0, 0)),
                  pl.BlockSpec((1, dv), lambda b: (0, 0)),
                  pl.BlockSpec((nb, hv, dk, dv), lambda b: (b, 0, 0, 0))],
        out_specs=[pl.BlockSpec((nb * seq, vw), lambda b: (b, 0)),
                   pl.BlockSpec((nb, hv, dk, dv), lambda b: (b, 0, 0, 0)),
                   pl.BlockSpec((nb, CONV_W - 1, GDN_CONV_CH), lambda b: (b, 0, 0))],
        out_shape=[jax.ShapeDtypeStruct((batch * seq, GDN_V), F32),
                   jax.ShapeDtypeStruct((batch, hv, dk, dv), F32),
                   jax.ShapeDtypeStruct((batch, CONV_W - 1, GDN_CONV_CH), F32)],
        scratch_shapes=[pltpu.VMEM((nb, 2 * SUBLANES, qw), F32),
                        pltpu.VMEM((nb, 2 * SUBLANES, qw), F32),
                        pltpu.VMEM((nb, 2 * SUBLANES, vw), F32),
                        pltpu.VMEM((nb, hv * seq, LANES), F32)],
        compiler_params=_params(("parallel",)),
        name="gdn_step",
    )(proj, proj, proj, proj, proj_ab, buf0, buf0, buf0, conv_w, conv_w, conv_w, alog_tab, dt_tab,
      norm_w.reshape(1, dv), s0)


def _trunk(x, mod_fn, per_row, batch, seq, pos0, ret_state, gdn_state, conv_state, wts, tm):
    (norm_pre, norm_post, w_gu, w_down, ret_w_in, ret_w_out, gdn_w_in, gdn_w_ab, gdn_conv_w, gdn_a_log,
     gdn_dt_bias, gdn_norm_w, gdn_w_out) = wts
    kw = dict(per_row=per_row, rows_per_batch=seq)
    depth = norm_pre.shape[0]
    new_ret, new_gdn, new_conv = [], [], []
    for i in range(depth):
        mod = mod_fn(i)

        def ffn(x, sub, f):
            a = _mod_mm(x, mod, sub, norm_pre[i, sub], w_gu, (i, f), glu=True, out_dtype=BF16,
                        tm=tm, tn=_col_tile(w_gu.shape[-1] // 2, GLU_COL_TILE), **kw)
            return _mm_out(a, w_down, (i, f), x, mod, sub, norm_post[i, sub], res_scale=FFN_RES,
                           tm=256, **kw)

        x = ffn(x, 0, 0)
        r = i // 2
        if i % 2 == 0:
            proj = _mod_mm(x, mod, 1, norm_pre[i, 1], ret_w_in, (r,), glu=False, out_dtype=F32,
                           tm=tm, tn=_col_tile(ret_w_in.shape[-1], COL_TILE), **kw)
            y, s = _retention(proj, None if ret_state is None else ret_state[r], pos0, batch, seq,
                              c=math.gcd(256, seq), heads=4 if seq >= 256 else RET_HEADS,
                              seqs=1 if seq >= 256 else 2)
            new_ret.append(s)
            x = _mm_out(y, ret_w_out, (r,), x, mod, 1, norm_post[i, 1], res_scale=1.0,
                        tm=256, **kw)
        else:
            wide = GDN_CONV_CH + GDN_V
            proj = _mod_mm(x, mod, 1, norm_pre[i, 1], gdn_w_in, (r,), glu=False, out_dtype=F32,
                           tm=tm, tn=_col_tile(wide, COL_TILE), ncols=wide, **kw)
            proj_ab = _mod_mm(x, mod, 1, norm_pre[i, 1], gdn_w_ab, (r,), glu=False, out_dtype=F32,
                              tm=tm, tn=_col_tile(gdn_w_ab.shape[-1], COL_TILE), **kw)
            if gdn_state is None:
                y, s = _gdn_prompt(proj, proj_ab, gdn_conv_w[r], gdn_a_log[r], gdn_dt_bias[r],
                                   gdn_norm_w[r], batch, seq)
                cs = proj.reshape(batch, seq, -1)[:, seq - (CONV_W - 1):, :GDN_CONV_CH]
            else:
                y, s, cs = _gdn_step(proj, proj_ab, gdn_conv_w[r], gdn_a_log[r], gdn_dt_bias[r],
                                     gdn_norm_w[r], gdn_state[r], conv_state[r], batch, seq)
            new_gdn.append(s)
            new_conv.append(cs)
            x = _mm_out(y, gdn_w_out, (r,), x, mod, 1, norm_post[i, 1], res_scale=1.0,
                        tm=256, **kw)
        x = ffn(x, 2, 1)
    stack = lambda xs: xs[0][None] if len(xs) == 1 else jnp.stack(xs)
    return x, stack(new_ret), stack(new_gdn), stack(new_conv)


def kernel(x_prompt, x_sample, c_prompt, c_sample, state_ret, state_gdn, state_conv, w_ada, b_ada,
           norm_pre, norm_post, ffn_w_gu, ffn_w_down, ret_w_in, ret_w_out, gdn_w_in, gdn_conv_w,
           gdn_a_log, gdn_dt_bias, gdn_norm_w, gdn_w_out):
    bp, lp, d = x_prompt.shape
    bs, ls, _ = x_sample.shape
    wts = (norm_pre, norm_post, ffn_w_gu.astype(BF16), ffn_w_down.astype(BF16),
           ret_w_in.astype(BF16), ret_w_out.astype(BF16),
           gdn_w_in.astype(BF16), jax.vmap(_gdn_ab_weight)(gdn_w_in),
           gdn_conv_w, gdn_a_log, gdn_dt_bias, gdn_norm_w, gdn_w_out.astype(BF16))

    cs_rows = c_sample
    cp_rows = jnp.pad(c_prompt, ((0, 2 * SUBLANES - bp), (0, 0)))
    mods = [_ada(cs_rows, cp_rows, w_ada, b_ada, i) for i in range(w_ada.shape[0])]

    y_p, ret_p, gdn_p, conv_p = _trunk(
        x_prompt.reshape(bp * lp, d), lambda i: mods[i][1], False, bp, lp, 0,
        None, None, None, wts, tm=1024)
    y_s, ret_s, gdn_s, conv_s = _trunk(
        x_sample.reshape(bs * ls, d), lambda i: mods[i][0], True, bs, ls, PAST_LEN,
        state_ret, state_gdn, state_conv, wts, tm=1024)
    return (y_p.reshape(bp, lp, d), y_s.reshape(bs, ls, d), ret_p, ret_s, gdn_p, gdn_s, conv_p, conv_s)
```

```python
import functools
import math

import jax
import jax.numpy as jnp
from jax import lax
from jax.experimental import pallas as pl
from jax.experimental.pallas import tpu as pltpu

F32 = jnp.float32
BF16 = jnp.bfloat16

EPS = 1e-6
ROPE_BASE = 10000.0
FFN_RES = 0.5
CONV_W = 4
CONV_HIST = CONV_W - 1
PAST_LEN = 16384

RET_HEADS = 8
RET_DK = 256
RET_DV = 512
GDN_K_HEADS = 16
GDN_V_HEADS = 32
GDN_DK = 128
GDN_DV = 128
GDN_QK = GDN_K_HEADS * GDN_DK
GDN_V = GDN_V_HEADS * GDN_DV
GDN_CONV_CH = 2 * GDN_QK + GDN_V

LANES = 128
SUBLANES = 8
VMEM_LIMIT = 56 * 1024 * 1024
COL_TILE = 1024
GLU_COL_TILE = 512


def _col_tile(n, cap):
    return max(t for t in range(LANES, cap + 1, LANES) if n % t == 0)


def _params(sem):
    return pltpu.CompilerParams(dimension_semantics=sem, vmem_limit_bytes=VMEM_LIMIT)


def _silu(x):
    return x * jax.nn.sigmoid(x)


def _dot(a, b):
    return jnp.dot(a.astype(BF16), b.astype(BF16), preferred_element_type=F32)


def _dot_nt(a, b):
    return lax.dot_general(a.astype(BF16), b.astype(BF16), (((1,), (1,)), ((), ())),
                           preferred_element_type=F32)


def _dot_tn(a, b):
    return lax.dot_general(a.astype(BF16), b.astype(BF16), (((0,), (0,)), ((), ())),
                           preferred_element_type=F32)


def _ada_kernel(cs_ref, cp_ref, w_ref, b_ref, os_ref, op_ref, as_scr, ap_scr):
    @pl.when(pl.program_id(0) == 0)
    def _():
        as_scr[...] = _silu(cs_ref[...]).astype(BF16)
        ap_scr[...] = _silu(cp_ref[...]).astype(BF16)

    w = w_ref[...].astype(BF16)
    b = b_ref[...]
    os_ref[...] = jnp.dot(as_scr[...], w, preferred_element_type=F32) + b
    op_ref[...] = jnp.dot(ap_scr[...], w, preferred_element_type=F32) + b


def _ada(cs, cp, w, b, layer, tn=COL_TILE):
    ms, d = cs.shape
    mp = cp.shape[0]
    depth, _, n = w.shape
    return pl.pallas_call(
        _ada_kernel,
        grid=(n // tn,),
        in_specs=[pl.BlockSpec((ms, d), lambda j: (0, 0)),
                  pl.BlockSpec((mp, d), lambda j: (0, 0)),
                  pl.BlockSpec((None, d, tn), lambda j: (layer, 0, j)),
                  pl.BlockSpec((None, 1, tn), lambda j: (layer, 0, j))],
        out_specs=[pl.BlockSpec((ms, tn), lambda j: (0, j)),
                   pl.BlockSpec((mp, tn), lambda j: (0, j))],
        out_shape=[jax.ShapeDtypeStruct((ms, n), F32), jax.ShapeDtypeStruct((mp, n), F32)],
        scratch_shapes=[pltpu.VMEM((ms, d), BF16), pltpu.VMEM((mp, d), BF16)],
        compiler_params=_params(("arbitrary",)),
        name="ada_table",
    )(cs, cp, w, b.reshape(depth, 1, n))


ROW_CHUNK = 128


def _mod_rows(ref, per_row, tiles_per_batch, chunk=None):
    if per_row:
        if chunk is None:
            return jnp.repeat(ref[...], per_row, axis=0)
        n = ROW_CHUNK // per_row
        return jnp.repeat(ref[pl.ds(pl.multiple_of(chunk * n, n), n), :], per_row, axis=0)
    return ref[pl.ds(pl.program_id(0) // tiles_per_batch, 1), :]


def _mod_mm_kernel(x_ref, sh_ref, sc_ref, nw_ref, *rest, glu, per_row, tiles_per_batch):
    if glu:
        wg_ref, wu_ref, o_ref, h_scr = rest
    else:
        w_ref, o_ref, h_scr = rest

    @pl.when(pl.program_id(1) == 0)
    def _():
        nw = nw_ref[...]

        def chunk(r, carry):
            start = pl.multiple_of(r * ROW_CHUNK, ROW_CHUNK)
            x = x_ref[pl.ds(start, ROW_CHUNK), :]
            y = x * lax.rsqrt(jnp.mean(x * x, -1, keepdims=True) + EPS) * nw
            sc = _mod_rows(sc_ref, per_row, tiles_per_batch, r)
            sh = _mod_rows(sh_ref, per_row, tiles_per_batch, r)
            h_scr[pl.ds(start, ROW_CHUNK), :] = (y * (1.0 + sc) + sh).astype(BF16)
            return carry

        lax.fori_loop(0, x_ref.shape[0] // ROW_CHUNK, chunk, 0)

    h = h_scr[...]
    if glu:
        g = jnp.dot(h, wg_ref[...], preferred_element_type=F32)
        u = jnp.dot(h, wu_ref[...], preferred_element_type=F32)
        o_ref[...] = (_silu(g) * u).astype(o_ref.dtype)
    else:
        o_ref[...] = jnp.dot(h, w_ref[...], preferred_element_type=F32).astype(o_ref.dtype)


def _mod_mm(x, mod, sub, nw, w, widx, *, glu, per_row, rows_per_batch, out_dtype, tm, tn, ncols=None):
    t, d = x.shape
    n = w.shape[-1] // 2 if glu else (ncols or w.shape[-1])
    nj = n // tn
    lead = (None,) * len(widx)
    tiles_per_batch = max(rows_per_batch // tm, 1)
    per_row = rows_per_batch if per_row else 0
    mrows = tm // rows_per_batch if per_row else mod.shape[0]

    def mod_spec(c):
        col = sub * 3 + c
        if per_row:
            return pl.BlockSpec((mrows, d), lambda i, j: (i, col))
        return pl.BlockSpec((mrows, d), lambda i, j: (0, col))

    in_specs = [pl.BlockSpec((tm, d), lambda i, j: (i, 0)),
                mod_spec(0), mod_spec(1),
                pl.BlockSpec((1, d), lambda i, j: (0, 0))]
    args = [x, mod, mod, nw.reshape(1, d)]
    if glu:
        in_specs += [pl.BlockSpec(lead + (d, tn), lambda i, j: widx + (0, j)),
                     pl.BlockSpec(lead + (d, tn), lambda i, j: widx + (0, j + nj))]
        args += [w, w]
    else:
        in_specs += [pl.BlockSpec(lead + (d, tn), lambda i, j: widx + (0, j))]
        args += [w]
    return pl.pallas_call(
        functools.partial(_mod_mm_kernel, glu=glu, per_row=per_row, tiles_per_batch=tiles_per_batch),
        grid=(t // tm, nj),
        in_specs=in_specs,
        out_specs=pl.BlockSpec((tm, tn), lambda i, j: (i, j)),
        out_shape=jax.ShapeDtypeStruct((t, n), out_dtype),
        scratch_shapes=[pltpu.VMEM((tm, d), BF16)],
        compiler_params=_params(("parallel", "arbitrary")),
        name="mod_mm_glu" if glu else "mod_mm",
    )(*args)


def _mm_out_kernel(a_ref, w_ref, x_ref, g_ref, nw_ref, o_ref, *, res_scale, per_row, tiles_per_batch):
    y = jnp.dot(a_ref[...].astype(BF16), w_ref[...], preferred_element_type=F32)
    yn = y * lax.rsqrt(jnp.mean(y * y, -1, keepdims=True) + EPS) * nw_ref[...]
    gate = _mod_rows(g_ref, per_row, tiles_per_batch)
    o_ref[...] = x_ref[...] + res_scale * (gate * yn)


def _mm_out(a, w, widx, x, mod, sub, nw, *, res_scale, per_row, rows_per_batch, tm):
    t, kdim = a.shape
    d = w.shape[-1]
    lead = (None,) * len(widx)
    tiles_per_batch = max(rows_per_batch // tm, 1)
    per_row = rows_per_batch if per_row else 0
    col = sub * 3 + 2
    if per_row:
        g_spec = pl.BlockSpec((tm // rows_per_batch, d), lambda i: (i, col))
    else:
        g_spec = pl.BlockSpec((mod.shape[0], d), lambda i: (0, col))
    return pl.pallas_call(
        functools.partial(_mm_out_kernel, res_scale=res_scale, per_row=per_row,
                          tiles_per_batch=tiles_per_batch),
        grid=(t // tm,),
        in_specs=[pl.BlockSpec((tm, kdim), lambda i: (i, 0)),
                  pl.BlockSpec(lead + (kdim, d), lambda i: widx + (0, 0), pipeline_mode=pl.Buffered(1)),
                  pl.BlockSpec((tm, d), lambda i: (i, 0)),
                  g_spec,
                  pl.BlockSpec((1, d), lambda i: (0, 0))],
        out_specs=pl.BlockSpec((tm, d), lambda i: (i, 0)),
        out_shape=jax.ShapeDtypeStruct((t, d), F32),
        compiler_params=_params(("parallel",)),
        name="mm_out",
    )(a, w, x, mod, nw.reshape(1, d))


def _rotate(x, cos, sin):
    half = x.shape[-1] // 2
    x1, x2 = x[:, :half], x[:, half:]
    return jnp.concatenate([x1 * cos - x2 * sin, x1 * sin + x2 * cos], axis=-1)


def _ret_kernel(dm_ref, qd_ref, kd_ref, cd_ref, cos_ref, sin_ref, q_ref, k_ref, v_ref, g_ref, *rest,
                heads, has_state, chunks, seqs):
    if has_state:
        s0_ref, o_ref, sout_ref = rest
    else:
        o_ref, sout_ref = rest
    dk, dv = RET_DK, RET_DV
    c = q_ref.shape[0] // seqs
    units = [(e, h) for e in range(seqs) for h in range(heads)]
    rows = lambda e: slice(e * c, (e + 1) * c)

    if has_state and chunks == 1:
        s = {u: s0_ref[u] for u in units}
    else:
        @pl.when(pl.program_id(2) == 0)
        def _():
            sout_ref[...] = s0_ref[...] if has_state else jnp.zeros_like(sout_ref)

        s = {u: sout_ref[u] for u in units}

    cos, sin = cos_ref[...], sin_ref[...]
    q = {(e, h): _rotate(q_ref[rows(e), h * dk:(h + 1) * dk], cos, sin) * (dk ** -0.5) for e, h in units}
    k = {(e, h): _rotate(k_ref[rows(e), h * dk:(h + 1) * dk], cos, sin) for e, h in units}
    v = {(e, h): v_ref[rows(e), h * dv:(h + 1) * dv].astype(BF16) for e, h in units}
    scores = {(e, h): _dot_nt(q[e, h], k[e, h]) * dm_ref[h] for e, h in units}
    cross = {u: _dot(q[u], s[u]) for u in units}
    inner = {u: _dot(scores[u], v[u]) for u in units}
    upd = {(e, h): _dot_tn(k[e, h] * jnp.concatenate([kd_ref[h]] * (dk // LANES), axis=1), v[e, h])
           for e, h in units}
    for e, h in units:
        sout_ref[e, h] = s[e, h] * cd_ref[h, 0:1, 0:1] + upd[e, h]
        o = inner[e, h] + cross[e, h] * jnp.concatenate([qd_ref[h]] * (dv // LANES), axis=1)
        on = o * lax.rsqrt(jnp.mean(o * o, -1, keepdims=True) + EPS)
        o_ref[rows(e), h * dv:(h + 1) * dv] = (
            _silu(g_ref[rows(e), h * dv:(h + 1) * dv]) * on).astype(o_ref.dtype)


def _retention(proj, s0, pos0, batch, seq, c, heads, seqs=1):
    nh, dk, dv = RET_HEADS, RET_DK, RET_DV
    nc = seq // c
    assert seqs == 1 or (nc == 1 and batch % seqs == 0)
    ng = nh // heads
    rb = seqs * c
    half = dk // 2
    log_g = jnp.log1p(-jnp.exp2(-5.0 - jnp.arange(nh, dtype=F32)))
    idx = jnp.arange(c, dtype=F32)
    diff = idx[:, None] - idx[None, :]
    causal = diff >= 0
    dmask = jnp.where(causal[None], jnp.exp(log_g[:, None, None] * jnp.where(causal, diff, 0.0)[None]), 0.0)
    lanes = lambda x: jnp.broadcast_to(x[:, :, None], (nh, x.shape[1], LANES))
    q_decay = lanes(jnp.exp(log_g[:, None] * (idx[None, :] + 1.0)))
    k_decay = lanes(jnp.exp(log_g[:, None] * (c - 1.0 - idx)[None, :]))
    chunk_decay = jnp.broadcast_to(jnp.exp(log_g * c)[:, None, None], (nh, SUBLANES, LANES))
    inv = 1.0 / (ROPE_BASE ** jnp.linspace(0.0, 1.0, half, dtype=F32))
    ang = (pos0 + jnp.arange(seq)).astype(F32)[:, None] * inv[None, :]
    cos, sin = jnp.cos(ang), jnp.sin(ang)
    has_state = s0 is not None
    tokb = lambda b, g, n: b * nc + n
    in_specs = [pl.BlockSpec((heads, c, c), lambda b, g, n: (g, 0, 0)),
                pl.BlockSpec((heads, c, LANES), lambda b, g, n: (g, 0, 0)),
                pl.BlockSpec((heads, c, LANES), lambda b, g, n: (g, 0, 0)),
                pl.BlockSpec((heads, SUBLANES, LANES), lambda b, g, n: (g, 0, 0)),
                pl.BlockSpec((c, half), lambda b, g, n: (n, 0)),
                pl.BlockSpec((c, half), lambda b, g, n: (n, 0)),
                pl.BlockSpec((rb, heads * dk), lambda b, g, n: (tokb(b, g, n), g)),
                pl.BlockSpec((rb, heads * dk), lambda b, g, n: (tokb(b, g, n), ng + g)),
                pl.BlockSpec((rb, heads * dv), lambda b, g, n: (tokb(b, g, n), ng + g)),
                pl.BlockSpec((rb, heads * dv), lambda b, g, n: (tokb(b, g, n), 2 * ng + g))]
    args = [dmask, q_decay, k_decay, chunk_decay, cos, sin, proj, proj, proj, proj]
    if has_state:
        in_specs.append(pl.BlockSpec((seqs, heads, dk, dv), lambda b, g, n: (b, g, 0, 0)))
        args.append(s0)
    return pl.pallas_call(
        functools.partial(_ret_kernel, heads=heads, has_state=has_state, chunks=nc, seqs=seqs),
        grid=(batch // seqs, ng, nc),
        in_specs=in_specs,
        out_specs=[pl.BlockSpec((rb, heads * dv), lambda b, g, n: (tokb(b, g, n), g)),
                   pl.BlockSpec((seqs, heads, dk, dv), lambda b, g, n: (b, g, 0, 0))],
        out_shape=[jax.ShapeDtypeStruct((batch * seq, nh * dv), BF16 if rb % (2 * SUBLANES) == 0 else F32),
                   jax.ShapeDtypeStruct((batch, nh, dk, dv), F32)],
        compiler_params=_params(("parallel", "parallel", "arbitrary")),
        name="retention",
    )(*args)


AB_GROUP = 4
AB_LANES = 4


def _gdn_ab_weight(w_in):
    d = w_in.shape[0]
    ab = w_in[:, GDN_CONV_CH + GDN_V:]
    a = ab[:, :GDN_V_HEADS].reshape(d, GDN_K_HEADS, 2)
    b = ab[:, GDN_V_HEADS:].reshape(d, GDN_K_HEADS, 2)
    groups = GDN_K_HEADS // AB_GROUP
    packed = jnp.concatenate([a, b], axis=-1).reshape(d, groups, AB_GROUP * AB_LANES)
    tail = jnp.concatenate([packed, jnp.zeros((d, groups, LANES - AB_GROUP * AB_LANES), w_in.dtype)], axis=-1)
    return tail.reshape(d, groups * LANES).astype(BF16)


def _gdn_group_tables(a_log, dt_bias):
    groups = GDN_K_HEADS // AB_GROUP

    def tab(x):
        per_head = jnp.concatenate([x.reshape(GDN_K_HEADS, 2), jnp.zeros((GDN_K_HEADS, AB_LANES - 2), F32)], -1)
        row = jnp.concatenate([per_head.reshape(groups, AB_GROUP * AB_LANES),
                               jnp.zeros((groups, LANES - AB_GROUP * AB_LANES), F32)], -1)
        return jnp.broadcast_to(row[:, None, :], (groups, SUBLANES, LANES))
    return tab(a_log), tab(dt_bias)


def _ab_to_lane0(x, j):
    return pltpu.roll(x, (LANES - AB_LANES * (j % AB_GROUP)) % LANES, axis=1)


def _gdn_head_tables(a_log, dt_bias):
    def tab(x):
        row = jnp.concatenate([x.reshape(GDN_K_HEADS, 2), jnp.zeros((GDN_K_HEADS, LANES - 2), F32)], -1)
        return jnp.broadcast_to(row[:, None, :], (GDN_K_HEADS, SUBLANES, LANES))
    return tab(a_log), tab(dt_bias)


GDN_C = 64
GDN_SUB = 256


def _chunk_cumsum(x, c):
    pos = lax.broadcasted_iota(jnp.int32, x.shape, 0) % c
    s = 1
    while s < c:
        x = x + jnp.where(pos >= s, pltpu.roll(x, s, axis=0), 0.0)
        s *= 2
    return x


GDN_BASE = 8


def _inverse_masks(c, width):
    i = jnp.arange(c)[:, None]
    j = (jnp.arange(width) % c)[None, :]
    r = (jnp.arange(width) % (2 * c))[:, None]
    s = jnp.arange(2 * c)[None, :]
    lanes = [(i // GDN_BASE) == (j // GDN_BASE)]
    pair = [(r // c) == (s // c), (r // GDN_BASE) == (s // GDN_BASE)]
    b = GDN_BASE
    while b < c:
        lanes.append(((i // (2 * b)) == (j // (2 * b))) & ((i // b) % 2 == 1) & ((j // b) % 2 == 0))
        pair.append(((r // (2 * b)) == (s // (2 * b))) & ((r // b) % 2 == 1) & ((s // b) % 2 == 0))
        b *= 2
    causal = jnp.stack([pair[0] & (r >= s), pair[0] & (r > s)]).astype(F32)
    return jnp.stack(lanes).astype(F32), jnp.stack(pair).astype(BF16), causal


def _mask_dict(lanes_ref, pair_ref):
    masks = {"base_lanes": lanes_ref[0], "same_pair": pair_ref[0], "base_pair": pair_ref[1]}
    for n in range(1, lanes_ref.shape[0]):
        masks[f"lanes{GDN_BASE << (n - 1)}"] = lanes_ref[n]
        masks[f"pair{GDN_BASE << (n - 1)}"] = pair_ref[n + 1]
    return masks


def _pair_to_diag(p):
    width, pc = p.shape
    n = width // pc
    zero = jnp.zeros((pc, pc), p.dtype)
    return jnp.concatenate(
        [jnp.concatenate([p[i * pc:(i + 1) * pc] if j == i else zero for j in range(n)], axis=1)
         for i in range(n)], axis=0)


def _lanes_to_diag(p, masks):
    c, width = p.shape
    pb = p.astype(BF16)
    pair = jnp.concatenate([jnp.concatenate([pb[:, i * 2 * c:(i + 1) * 2 * c]] * 2, axis=0)
                            for i in range(width // (2 * c))], axis=0)
    return _pair_to_diag(pair * masks["same_pair"])


def _inverse_minus_eye_lanes(l_lbs, l_pairs, masks):
    c = l_lbs[0].shape[0]
    ps = range(len(l_lbs))
    mm = lambda a, b: jnp.dot(a.astype(BF16), b, preferred_element_type=F32)

    lb = [l_lbs[i] * masks["base_lanes"] for i in ps]
    x = [-lb[i] for i in ps]
    p = [mm(lb[i], _pair_to_diag(l_pairs[i] * masks["base_pair"])) for i in ps]
    n = 2
    while 2 * n < GDN_BASE:
        xp = [mm(jnp.concatenate([x[i], p[i]], axis=0), _lanes_to_diag(p[i], masks)) for i in ps]
        x = [x[i] + p[i] + xp[i][:c] for i in ps]
        p = [xp[i][c:] for i in ps]
        n *= 2
    xp = [mm(x[i], _lanes_to_diag(p[i], masks)) for i in ps]
    x = [x[i] + p[i] + xp[i] for i in ps]

    b = GDN_BASE
    while b < c:
        t = [l_lbs[i] * masks[f"lanes{b}"]
             + mm(x[i], _pair_to_diag(l_pairs[i] * masks[f"pair{b}"])) for i in ps]
        tx = [mm(t[i], _lanes_to_diag(x[i], masks)) for i in ps]
        x = [x[i] - t[i] - tx[i] for i in ps]
        b *= 2
    return x


def _gdn_prep_kernel(q_ref, k_ref, v_ref, ab_ref, qp_ref, kp_ref, vp_ref, wq_ref, wk_ref, wv_ref,
                     alog_ref, dt_ref, ml_ref, mp_ref, mc_ref, u_ref, w_ref, qg_ref, kgt_ref, at_ref, eg_ref,
                     bq, bk, bv, gc_scr, *, tb):
    c, sub, dk, dv = GDN_C, GDN_SUB, GDN_DK, GDN_DV
    first = pl.program_id(1) == 0

    def conv_silu(buf, prev_ref, x_ref, cw_ref):
        buf[pl.ds(0, SUBLANES), :] = jnp.where(first, 0.0, prev_ref[...])
        buf[pl.ds(SUBLANES, tb), :] = x_ref[...]
        cw = cw_ref[...]
        out = buf[pl.ds(SUBLANES - CONV_HIST, tb), :] * cw[0:1, :]
        for j in range(1, CONV_W):
            out = out + buf[pl.ds(SUBLANES - CONV_HIST + j, tb), :] * cw[j:j + 1, :]
        return _silu(out)

    def unit(x):
        return x * lax.rsqrt(jnp.sum(x * x, -1, keepdims=True) + EPS)

    khs = range(AB_GROUP)
    q_all = conv_silu(bq, qp_ref, q_ref, wq_ref)
    k_all = conv_silu(bk, kp_ref, k_ref, wk_ref)
    v_all = conv_silu(bv, vp_ref, v_ref, wv_ref)
    q = [unit(q_all[:, i * dk:(i + 1) * dk]) * (dk ** -0.5) for i in khs]
    k = [unit(k_all[:, i * dk:(i + 1) * dk]) for i in khs]

    ab = ab_ref[...]
    g_all = -jnp.exp(alog_ref[0:1, :]) * jax.nn.softplus(ab + dt_ref[0:1, :])
    beta_all = jax.nn.sigmoid(ab)
    gc_all = _chunk_cumsum(g_all, c)
    gc_t = gc_all.T

    masks = _mask_dict(ml_ref, mp_ref)
    tri = mc_ref[0] > 0.0
    strict = mc_ref[1] > 0.0

    nh, nchunk, pc = tb // sub, sub // c, 2 * c
    rows = [slice(hf * sub, (hf + 1) * sub) for hf in range(nh)]
    pairs = [slice(i * pc, (i + 1) * pc) for i in range(tb // pc)]
    kb, qb = [x.astype(BF16) for x in k], [x.astype(BF16) for x in q]
    npair = sub // pc
    kk = {(i, pr): _dot_nt(kb[i][pairs[pr]], kb[i][pairs[pr]]) for i in khs for pr in range(len(pairs))}
    qk = {(i, pr): _dot_nt(qb[i][pairs[pr]], kb[i][pairs[pr]]) for i in khs for pr in range(len(pairs))}
    half_rows = lambda d, i, hf: jnp.concatenate([d[i, hf * npair + pr] for pr in range(npair)], axis=0)
    probs = [(i, hf, e) for i in khs for hf in range(nh) for e in range(2)]
    gcb, betab, gl, l_pair, l_lb = [], [], [], [], []
    for p, (i, hf, e) in enumerate(probs):
        lanes = slice((2 * i + e) * dv, (2 * i + e + 1) * dv)
        la, lb = AB_LANES * i + e, AB_LANES * i + 2 + e
        gcb.append(jnp.broadcast_to(gc_all[rows[hf], la:la + 1], (sub, LANES)))
        betab.append(jnp.broadcast_to(beta_all[rows[hf], lb:lb + 1], (sub, LANES)))
        gc_scr[p] = gcb[p]
        gl.append(gc_scr[p, pl.ds(c - 1, nchunk, stride=c), :])
        gc_cols = jnp.concatenate(
            [jnp.broadcast_to(gc_t[la:la + 1, hf * sub + pr * pc:hf * sub + (pr + 1) * pc], (pc, pc))
             for pr in range(npair)], axis=0)
        decay = jnp.where(tri, jnp.exp(jnp.where(tri, gcb[p] - gc_cols, 0.0)), 0.0)
        lp = jnp.where(strict, half_rows(kk, i, hf) * betab[p] * decay, 0.0)
        at_ref[rows[hf], lanes] = (half_rows(qk, i, hf) * decay).astype(BF16)
        l_pair.append(lp.astype(BF16))
        l_lb.append(jnp.concatenate([lp[2 * pr * c:(2 * pr + 1) * c] + lp[(2 * pr + 1) * c:(2 * pr + 2) * c]
                                     for pr in range(npair)], axis=1))
    x_lb = _inverse_minus_eye_lanes(l_lb, l_pair, masks)
    rhs = [jnp.concatenate([v_all[rows[hf], (2 * i + e) * dv:(2 * i + e + 1) * dv] * betab[p],
                            k[i][rows[hf]] * betab[p] * jnp.exp(gcb[p])], axis=1)
           for p, (i, hf, e) in enumerate(probs)]
    corr = [jnp.dot(_lanes_to_diag(x_lb[p], masks), rhs[p].astype(BF16), preferred_element_type=F32)
            for p in range(len(probs))]
    for p, (i, hf, e) in enumerate(probs):
        lanes = slice((2 * i + e) * dv, (2 * i + e + 1) * dv)
        sol = rhs[p] + corr[p]
        u_ref[rows[hf], lanes] = sol[:, :dv]
        w_ref[rows[hf], lanes] = sol[:, dv:].astype(BF16)
        qg_ref[rows[hf], lanes] = (q[i][rows[hf]] * jnp.exp(gcb[p])).astype(BF16)
        gl_b = jnp.concatenate([jnp.broadcast_to(gl[p][ch:ch + 1, :], (c, LANES)) for ch in range(nchunk)], axis=0)
        kg = k[i][rows[hf]] * jnp.exp(gl_b - gcb[p])
        kgt_ref[rows[hf], lanes] = jnp.concatenate(
            [kg[pr * pc:(pr + 1) * pc].T for pr in range(npair)], axis=0).astype(BF16)
        eg_ref[hf * nchunk:(hf + 1) * nchunk, lanes] = jnp.exp(gl[p])


def _gdn_scan_kernel(u_ref, w_ref, qg_ref, kgt_ref, at_ref, eg_ref, z_ref, nw_ref, o_ref, sout_ref,
                     *, tb, heads):
    c = GDN_C

    @pl.when(pl.program_id(2) == 0)
    def _():
        sout_ref[...] = jnp.zeros_like(sout_ref)

    nw = nw_ref[...]
    hl = [slice(h * GDN_DV, (h + 1) * GDN_DV) for h in range(heads)]
    s = [sout_ref[h] for h in range(heads)]
    zeros = jnp.zeros((c, GDN_DV), BF16)
    for i in range(tb // c):
        rows = slice(i * c, (i + 1) * c)
        pair_rows = slice((i // 2) * 2 * c, (i // 2 + 1) * 2 * c)
        ws = [jnp.dot(jnp.concatenate([w_ref[rows, hl[h]], qg_ref[rows, hl[h]]], axis=0),
                      s[h].astype(BF16), preferred_element_type=F32) for h in range(heads)]
        v_new = [(u_ref[rows, hl[h]] - ws[h][:c]).astype(BF16) for h in range(heads)]
        pair = [jnp.concatenate([v_new[h], zeros] if i % 2 == 0 else [zeros, v_new[h]], axis=0)
                for h in range(heads)]
        for h in range(heads):
            o = ws[h][c:] + jnp.dot(at_ref[rows, hl[h]], pair[h], preferred_element_type=F32)
            s[h] = s[h] * eg_ref[i:i + 1, hl[h]] + jnp.dot(kgt_ref[pair_rows, hl[h]], pair[h],
                                                          preferred_element_type=F32)
            on = o * lax.rsqrt(jnp.mean(o * o, -1, keepdims=True) + EPS) * nw
            o_ref[rows, hl[h]] = (on * _silu(z_ref[rows, hl[h]])).astype(o_ref.dtype)
    for h in range(heads):
        sout_ref[h] = s[h]


def _gdn_prompt(proj, proj_ab, conv_w, a_log, dt_bias, norm_w, batch, seq, tb_prep=512, tb_scan=512,
                heads=8):
    hk, dk, dv = GDN_K_HEADS, GDN_DK, GDN_DV
    assert seq % tb_prep == 0 and seq % tb_scan == 0
    t = batch * seq
    tb = tb_prep
    nt = seq // tb
    kw_, vw_ = AB_GROUP * dk, AB_GROUP * 2 * dv
    kq0 = GDN_QK // kw_
    v0 = 2 * GDN_QK // vw_
    alog_tab, dt_tab = _gdn_group_tables(a_log, dt_bias)
    m_lanes, m_pair, m_causal = _inverse_masks(GDN_C, GDN_SUB)
    tok = lambda b, n, j: b * nt + n
    prev = lambda b, n, j: jnp.maximum((b * seq + n * tb) // SUBLANES - 1, 0)
    u, w, qg, kg, at, eg = pl.pallas_call(
        functools.partial(_gdn_prep_kernel, tb=tb),
        grid=(batch, nt, hk // AB_GROUP),
        in_specs=[pl.BlockSpec((tb, kw_), lambda b, n, j: (tok(b, n, j), j)),
                  pl.BlockSpec((tb, kw_), lambda b, n, j: (tok(b, n, j), kq0 + j)),
                  pl.BlockSpec((tb, vw_), lambda b, n, j: (tok(b, n, j), v0 + j)),
                  pl.BlockSpec((tb, LANES), lambda b, n, j: (tok(b, n, j), j)),
                  pl.BlockSpec((SUBLANES, kw_), lambda b, n, j: (prev(b, n, j), j)),
                  pl.BlockSpec((SUBLANES, kw_), lambda b, n, j: (prev(b, n, j), kq0 + j)),
                  pl.BlockSpec((SUBLANES, vw_), lambda b, n, j: (prev(b, n, j), v0 + j)),
                  pl.BlockSpec((CONV_W, kw_), lambda b, n, j: (0, j)),
                  pl.BlockSpec((CONV_W, kw_), lambda b, n, j: (0, kq0 + j)),
                  pl.BlockSpec((CONV_W, vw_), lambda b, n, j: (0, v0 + j)),
                  pl.BlockSpec((None, SUBLANES, LANES), lambda b, n, j: (j, 0, 0)),
                  pl.BlockSpec((None, SUBLANES, LANES), lambda b, n, j: (j, 0, 0)),
                  pl.BlockSpec(m_lanes.shape, lambda b, n, j: (0, 0, 0)),
                  pl.BlockSpec(m_pair.shape, lambda b, n, j: (0, 0, 0)),
                  pl.BlockSpec(m_causal.shape, lambda b, n, j: (0, 0, 0))],
        out_specs=[pl.BlockSpec((tb, vw_), lambda b, n, j: (tok(b, n, j), j))] * 5
        + [pl.BlockSpec((tb // GDN_C, vw_), lambda b, n, j: (tok(b, n, j), j))],
        out_shape=[jax.ShapeDtypeStruct((t, GDN_V), F32)]
        + [jax.ShapeDtypeStruct((t, GDN_V), BF16)] * 4
        + [jax.ShapeDtypeStruct((t // GDN_C, GDN_V), F32)],
        scratch_shapes=[pltpu.VMEM((SUBLANES + tb, kw_), F32),
                        pltpu.VMEM((SUBLANES + tb, kw_), F32),
                        pltpu.VMEM((SUBLANES + tb, vw_), F32),
                        pltpu.VMEM((2 * AB_GROUP * tb // GDN_SUB, GDN_SUB, LANES), F32)],
        compiler_params=_params(("parallel", "parallel", "parallel")),
        name="gdn_prep",
    )(proj, proj, proj, proj_ab, proj, proj, proj, conv_w, conv_w, conv_w, alog_tab, dt_tab,
      m_lanes, m_pair, m_causal)

    tb = tb_scan
    nt = seq // tb
    wide = heads * dv
    z0 = GDN_CONV_CH // wide
    blk = lambda b, g, n: (b * nt + n, g)
    return pl.pallas_call(
        functools.partial(_gdn_scan_kernel, tb=tb, heads=heads),
        grid=(batch, GDN_V_HEADS // heads, nt),
        in_specs=[pl.BlockSpec((tb, wide), blk)] * 5
        + [pl.BlockSpec((tb // GDN_C, wide), blk),
           pl.BlockSpec((tb, wide), lambda b, g, n: (b * nt + n, z0 + g)),
           pl.BlockSpec((1, dv), lambda b, g, n: (0, 0))],
        out_specs=[pl.BlockSpec((tb, wide), blk),
                   pl.BlockSpec((None, heads, dk, dv), lambda b, g, n: (b, g, 0, 0))],
        out_shape=[jax.ShapeDtypeStruct((t, GDN_V), BF16),
                   jax.ShapeDtypeStruct((batch, GDN_V_HEADS, dk, dv), F32)],
        compiler_params=_params(("parallel", "parallel", "arbitrary")),
        name="gdn_scan",
    )(u, w, qg, kg, at, eg, proj, norm_w.reshape(1, dv))


def _gdn_step_kernel(q_ref, k_ref, v_ref, z_ref, ab_ref, cq_ref, ck_ref, cv_ref, wq_ref, wk_ref, wv_ref,
                     alog_ref, dt_ref, nw_ref, s0_ref, o_ref, sout_ref, cout_ref, bq, bk, bv, g_scr, *, seq, nb):
    hv, hk, dk, dv = GDN_V_HEADS, GDN_K_HEADS, GDN_DK, GDN_DV
    n = hv * seq
    rep = hv // hk
    es = range(nb)

    def conv_silu(e, buf, c_ref, x_ref, cw_ref, col):
        width = buf.shape[-1]
        buf[e, pl.ds(SUBLANES - CONV_HIST, CONV_HIST), :] = c_ref[e]
        buf[e, pl.ds(SUBLANES, seq), :] = x_ref[e * seq:(e + 1) * seq, :]
        cout_ref[e, :, col:col + width] = buf[e, pl.ds(SUBLANES + seq - CONV_HIST, CONV_HIST), :]
        cw = cw_ref[...]
        out = buf[e, pl.ds(SUBLANES - CONV_HIST, seq), :] * cw[0:1, :]
        for j in range(1, CONV_W):
            out = out + buf[e, pl.ds(SUBLANES - CONV_HIST + j, seq), :] * cw[j:j + 1, :]
        return _silu(out)

    def stack(x, width):
        return jnp.concatenate([x[:, h * width:(h + 1) * width] for h in range(x.shape[1] // width)], axis=0)

    def per_v_head(x):
        return jnp.concatenate([x[(h // rep) * seq:(h // rep + 1) * seq] for h in range(hv)], axis=0)

    def unit(x):
        return x * lax.rsqrt(jnp.sum(x * x, -1, keepdims=True) + EPS)

    def per_head_rows(x_st, lane0):
        cols = [jnp.broadcast_to(x_st[:, lane0 + e:lane0 + e + 1], (hk * seq, LANES)) for e in range(rep)]
        return jnp.concatenate([cols[h % rep][(h // rep) * seq:(h // rep + 1) * seq] for h in range(hv)], axis=0)

    q = [per_v_head(unit(stack(conv_silu(e, bq, cq_ref, q_ref, wq_ref, 0), dk))) * (dk ** -0.5) for e in es]
    k = [per_v_head(unit(stack(conv_silu(e, bk, ck_ref, k_ref, wk_ref, GDN_QK), dk))) for e in es]
    v = [stack(conv_silu(e, bv, cv_ref, v_ref, wv_ref, 2 * GDN_QK), dv) for e in es]

    ab = [jnp.concatenate(
        [_ab_to_lane0(ab_ref[e * seq:(e + 1) * seq, (j // AB_GROUP) * LANES:(j // AB_GROUP + 1) * LANES], j)
         for j in range(hk)], axis=0) for e in es]
    gc_st = [_chunk_cumsum(-jnp.exp(alog_ref[...]) * jax.nn.softplus(ab[e] + dt_ref[...]), seq) for e in es]
    gcb = [per_head_rows(gc_st[e], 0) for e in es]
    betab = [per_head_rows(jax.nn.sigmoid(ab[e]), rep) for e in es]
    gc_row = [gcb[e].T[0:1, :] for e in es]
    gl = []
    for e in es:
        g_scr[e] = gcb[e]
        gl.append(g_scr[e, pl.ds(seq - 1, hv, stride=seq), :])
    eg = [jnp.exp(gl[e]) for e in es]

    r = lax.broadcasted_iota(jnp.int32, (n, n), 0)
    s = lax.broadcasted_iota(jnp.int32, (n, n), 1)
    same = (r // seq) == (s // seq)
    tri = same & (r >= s)
    strict = same & (r > s)
    wide = lambda x: jnp.concatenate([x] * (n // LANES), axis=1)
    decay = [jnp.where(tri, jnp.exp(jnp.where(tri, wide(gcb[e]) - gc_row[e], 0.0)), 0.0) for e in es]
    kq = [_dot_nt(jnp.concatenate([k[e], q[e]], axis=0), k[e]) for e in es]
    l_bd = [jnp.where(strict, kq[e][:n] * wide(betab[e]) * decay[e], 0.0) for e in es]
    a_bd = [kq[e][n:] * decay[e] for e in es]

    x = [-l_bd[e] for e in es]
    p = [_dot(l_bd[e], l_bd[e]) for e in es]
    m = 2
    while 2 * m < seq:
        xp = [_dot(jnp.concatenate([x[e], p[e]], axis=0), p[e]) for e in es]
        x = [x[e] + p[e] + xp[e][:n] for e in es]
        p = [xp[e][n:] for e in es]
        m *= 2
    xp = [_dot(x[e], p[e]) for e in es]
    x = [x[e] + p[e] + xp[e] for e in es]

    rhs = [jnp.concatenate([v[e] * betab[e], k[e] * betab[e] * jnp.exp(gcb[e])], axis=1) for e in es]
    corr = [_dot(x[e], rhs[e]) for e in es]
    sol = [rhs[e] + corr[e] for e in es]
    qg = [q[e] * jnp.exp(gcb[e]) for e in es]
    kg = []
    for e in es:
        gl_b = jnp.concatenate([jnp.broadcast_to(gl[e][h:h + 1, :], (seq, LANES)) for h in range(hv)], axis=0)
        kg.append(k[e] * jnp.exp(gl_b - gcb[e]))

    eh = [(e, h) for e in es for h in range(hv)]
    hrows = [slice(h * seq, (h + 1) * seq) for h in range(hv)]
    ws = {(e, h): _dot(jnp.concatenate([sol[e][hrows[h], dv:], qg[e][hrows[h]]], axis=0), s0_ref[e, h])
          for e, h in eh}
    v_new = {(e, h): sol[e][hrows[h], :dv] - ws[e, h][:seq] for e, h in eh}
    upd = {(e, h): _dot_tn(kg[e][hrows[h]], v_new[e, h]) for e, h in eh}
    for e, h in eh:
        sout_ref[e, h] = s0_ref[e, h] * eg[e][h:h + 1, :] + upd[e, h]
    inner = [_dot(a_bd[e], jnp.concatenate([v_new[e, h] for h in range(hv)], axis=0)) for e in es]
    for e in es:
        o = jnp.concatenate([ws[e, h][seq:] for h in range(hv)], axis=0) + inner[e]
        z = stack(z_ref[e * seq:(e + 1) * seq, :], dv)
        on = o * lax.rsqrt(jnp.mean(o * o, -1, keepdims=True) + EPS) * nw_ref[...] * _silu(z)
        for h in range(hv):
            o_ref[e * seq:(e + 1) * seq, h * dv:(h + 1) * dv] = on[h * seq:(h + 1) * seq]


def _gdn_step(proj, proj_ab, conv_w, a_log, dt_bias, norm_w, s0, buf0, batch, seq, nb=4):
    hv, hk, dk, dv = GDN_V_HEADS, GDN_K_HEADS, GDN_DK, GDN_DV
    assert seq == SUBLANES and batch % nb == 0
    qw, vw, abw = GDN_QK, GDN_V, hk // AB_GROUP * LANES
    alog_tab, dt_tab = _gdn_head_tables(a_log, dt_bias)
    alog_tab = alog_tab.reshape(hk * seq, LANES)
    dt_tab = dt_tab.reshape(hk * seq, LANES)
    return pl.pallas_call(
        functools.partial(_gdn_step_kernel, seq=seq, nb=nb),
        grid=(batch // nb,),
        in_specs=[pl.BlockSpec((nb * seq, qw), lambda b: (b, 0)),
                  pl.BlockSpec((nb * seq, qw), lambda b: (b, 1)),
                  pl.BlockSpec((nb * seq, vw), lambda b: (b, 2 * qw // vw)),
                  pl.BlockSpec((nb * seq, vw), lambda b: (b, GDN_CONV_CH // vw)),
                  pl.BlockSpec((nb * seq, abw), lambda b: (b, 0)),
                  pl.BlockSpec((nb, CONV_W - 1, qw), lambda b: (b, 0, 0)),
                  pl.BlockSpec((nb, CONV_W - 1, qw), lambda b: (b, 0, 1)),
                  pl.BlockSpec((nb, CONV_W - 1, vw), lambda b: (b, 0, 2 * qw // vw)),
                  pl.BlockSpec((CONV_W, qw), lambda b: (0, 0)),
                  pl.BlockSpec((CONV_W, qw), lambda b: (0, 1)),
                  pl.BlockSpec((CONV_W, vw), lambda b: (0, 2 * qw // vw)),
                  pl.BlockSpec((hk * seq, LANES), lambda b: (0, 0)),
                  pl.BlockSpec((hk * seq, LANES), lambda b: (0, 0)),
                  pl.BlockSpec((1, dv), lambda b: (0, 0)),
                  pl.BlockSpec((nb, hv, dk, dv), lambda b: (b, 0, 0, 0))],
        out_specs=[pl.BlockSpec((nb * seq, vw), lambda b: (b, 0)),
                   pl.BlockSpec((nb, hv, dk, dv), lambda b: (b, 0, 0, 0)),
                   pl.BlockSpec((nb, CONV_W - 1, GDN_CONV_CH), lambda b: (b, 0, 0))],
        out_shape=[jax.ShapeDtypeStruct((batch * seq, GDN_V), F32),
                   jax.ShapeDtypeStruct((batch, hv, dk, dv), F32),
                   jax.ShapeDtypeStruct((batch, CONV_W - 1, GDN_CONV_CH), F32)],
        scratch_shapes=[pltpu.VMEM((nb, 2 * SUBLANES, qw), F32),
                        pltpu.VMEM((nb, 2 * SUBLANES, qw), F32),
                        pltpu.VMEM((nb, 2 * SUBLANES, vw), F32),
                        pltpu.VMEM((nb, hv * seq, LANES), F32)],
        compiler_params=_params(("parallel",)),
        name="gdn_step",
    )(proj, proj, proj, proj, proj_ab, buf0, buf0, buf0, conv_w, conv_w, conv_w, alog_tab, dt_tab,
      norm_w.reshape(1, dv), s0)


def _trunk(x, mod_fn, per_row, batch, seq, pos0, ret_state, gdn_state, conv_state, wts, tm):
    (norm_pre, norm_post, w_gu, w_down, ret_w_in, ret_w_out, gdn_w_in, gdn_w_ab, gdn_conv_w, gdn_a_log,
     gdn_dt_bias, gdn_norm_w, gdn_w_out) = wts
    kw = dict(per_row=per_row, rows_per_batch=seq)
    depth = norm_pre.shape[0]
    new_ret, new_gdn, new_conv = [], [], []
    for i in range(depth):
        mod = mod_fn(i)

        def ffn(x, sub, f):
            a = _mod_mm(x, mod, sub, norm_pre[i, sub], w_gu, (i, f), glu=True, out_dtype=BF16,
                        tm=tm, tn=_col_tile(w_gu.shape[-1] // 2, GLU_COL_TILE), **kw)
            return _mm_out(a, w_down, (i, f), x, mod, sub, norm_post[i, sub], res_scale=FFN_RES,
                           tm=256, **kw)

        x = ffn(x, 0, 0)
        r = i // 2
        if i % 2 == 0:
            proj = _mod_mm(x, mod, 1, norm_pre[i, 1], ret_w_in, (r,), glu=False, out_dtype=F32,
                           tm=tm, tn=_col_tile(ret_w_in.shape[-1], COL_TILE), **kw)
            y, s = _retention(proj, None if ret_state is None else ret_state[r], pos0, batch, seq,
                              c=math.gcd(256, seq), heads=4 if seq >= 256 else RET_HEADS,
                              seqs=1 if seq >= 256 else 2)
            new_ret.append(s)
            x = _mm_out(y, ret_w_out, (r,), x, mod, 1, norm_post[i, 1], res_scale=1.0,
                        tm=256, **kw)
        else:
            wide = GDN_CONV_CH + GDN_V
            proj = _mod_mm(x, mod, 1, norm_pre[i, 1], gdn_w_in, (r,), glu=False, out_dtype=F32,
                           tm=tm, tn=_col_tile(wide, COL_TILE), ncols=wide, **kw)
            proj_ab = _mod_mm(x, mod, 1, norm_pre[i, 1], gdn_w_ab, (r,), glu=False, out_dtype=F32,
                              tm=tm, tn=_col_tile(gdn_w_ab.shape[-1], COL_TILE), **kw)
            if gdn_state is None:
                y, s = _gdn_prompt(proj, proj_ab, gdn_conv_w[r], gdn_a_log[r], gdn_dt_bias[r],
                                   gdn_norm_w[r], batch, seq)
                cs = proj.reshape(batch, seq, -1)[:, seq - (CONV_W - 1):, :GDN_CONV_CH]
            else:
                y, s, cs = _gdn_step(proj, proj_ab, gdn_conv_w[r], gdn_a_log[r], gdn_dt_bias[r],
                                     gdn_norm_w[r], gdn_state[r], conv_state[r], batch, seq)
            new_gdn.append(s)
            new_conv.append(cs)
            x = _mm_out(y, gdn_w_out, (r,), x, mod, 1, norm_post[i, 1], res_scale=1.0,
                        tm=256, **kw)
        x = ffn(x, 2, 1)
    stack = lambda xs: xs[0][None] if len(xs) == 1 else jnp.stack(xs)
    return x, stack(new_ret), stack(new_gdn), stack(new_conv)


def kernel(x_prompt, x_sample, c_prompt, c_sample, state_ret, state_gdn, state_conv, w_ada, b_ada,
           norm_pre, norm_post, ffn_w_gu, ffn_w_down, ret_w_in, ret_w_out, gdn_w_in, gdn_conv_w,
           gdn_a_log, gdn_dt_bias, gdn_norm_w, gdn_w_out):
    bp, lp, d = x_prompt.shape
    bs, ls, _ = x_sample.shape
    wts = (norm_pre, norm_post, ffn_w_gu.astype(BF16), ffn_w_down.astype(BF16),
           ret_w_in.astype(BF16), ret_w_out.astype(BF16),
           gdn_w_in.astype(BF16), jax.vmap(_gdn_ab_weight)(gdn_w_in),
           gdn_conv_w, gdn_a_log, gdn_dt_bias, gdn_norm_w, gdn_w_out.astype(BF16))

    cs_rows = c_sample
    cp_rows = jnp.pad(c_prompt, ((0, 2 * SUBLANES - bp), (0, 0)))
    mods = [_ada(cs_rows, cp_rows, w_ada, b_ada, i) for i in range(w_ada.shape[0])]

    y_p, ret_p, gdn_p, conv_p = _trunk(
        x_prompt.reshape(bp * lp, d), lambda i: mods[i][1], False, bp, lp, 0,
        None, None, None, wts, tm=1024)
    y_s, ret_s, gdn_s, conv_s = _trunk(
        x_sample.reshape(bs * ls, d), lambda i: mods[i][0], True, bs, ls, PAST_LEN,
        state_ret, state_gdn, state_conv, wts, tm=1024)
    return (y_p.reshape(bp, lp, d), y_s.reshape(bs, ls, d), ret_p, ret_s, gdn_p, gdn_s, conv_p, conv_s)
```

```python
import functools
import math

import jax
import jax.numpy as jnp
from jax import lax
from jax.experimental import pallas as pl
from jax.experimental.pallas import tpu as pltpu

F32 = jnp.float32
BF16 = jnp.bfloat16

EPS = 1e-6
ROPE_BASE = 10000.0
FFN_RES = 0.5
CONV_W = 4
CONV_HIST = CONV_W - 1
PAST_LEN = 16384

RET_HEADS = 8
RET_DK = 256
RET_DV = 512
GDN_K_HEADS = 16
GDN_V_HEADS = 32
GDN_DK = 128
GDN_DV = 128
GDN_QK = GDN_K_HEADS * GDN_DK
GDN_V = GDN_V_HEADS * GDN_DV
GDN_CONV_CH = 2 * GDN_QK + GDN_V

LANES = 128
SUBLANES = 8
VMEM_LIMIT = 56 * 1024 * 1024
COL_TILE = 1024
GLU_COL_TILE = 512


def _col_tile(n, cap):
    return max(t for t in range(LANES, cap + 1, LANES) if n % t == 0)


def _params(sem):
    return pltpu.CompilerParams(dimension_semantics=sem, vmem_limit_bytes=VMEM_LIMIT)


def _silu(x):
    return x * jax.nn.sigmoid(x)


def _dot(a, b):
    return jnp.dot(a.astype(BF16), b.astype(BF16), preferred_element_type=F32)


def _dot_nt(a, b):
    return lax.dot_general(a.astype(BF16), b.astype(BF16), (((1,), (1,)), ((), ())),
                           preferred_element_type=F32)


def _dot_tn(a, b):
    return lax.dot_general(a.astype(BF16), b.astype(BF16), (((0,), (0,)), ((), ())),
                           preferred_element_type=F32)


def _ada_kernel(cs_ref, cp_ref, w_ref, b_ref, os_ref, op_ref, as_scr, ap_scr):
    @pl.when(pl.program_id(0) == 0)
    def _():
        as_scr[...] = _silu(cs_ref[...]).astype(BF16)
        ap_scr[...] = _silu(cp_ref[...]).astype(BF16)

    w = w_ref[...].astype(BF16)
    b = b_ref[...]
    os_ref[...] = jnp.dot(as_scr[...], w, preferred_element_type=F32) + b
    op_ref[...] = jnp.dot(ap_scr[...], w, preferred_element_type=F32) + b


def _ada(cs, cp, w, b, layer, tn=COL_TILE):
    ms, d = cs.shape
    mp = cp.shape[0]
    depth, _, n = w.shape
    return pl.pallas_call(
        _ada_kernel,
        grid=(n // tn,),
        in_specs=[pl.BlockSpec((ms, d), lambda j: (0, 0)),
                  pl.BlockSpec((mp, d), lambda j: (0, 0)),
                  pl.BlockSpec((None, d, tn), lambda j: (layer, 0, j)),
                  pl.BlockSpec((None, 1, tn), lambda j: (layer, 0, j))],
        out_specs=[pl.BlockSpec((ms, tn), lambda j: (0, j)),
                   pl.BlockSpec((mp, tn), lambda j: (0, j))],
        out_shape=[jax.ShapeDtypeStruct((ms, n), F32), jax.ShapeDtypeStruct((mp, n), F32)],
        scratch_shapes=[pltpu.VMEM((ms, d), BF16), pltpu.VMEM((mp, d), BF16)],
        compiler_params=_params(("arbitrary",)),
        name="ada_table",
    )(cs, cp, w, b.reshape(depth, 1, n))


ROW_CHUNK = 128


def _mod_rows(ref, per_row, tiles_per_batch, chunk=None):
    if per_row:
        if chunk is None:
            return jnp.repeat(ref[...], per_row, axis=0)
        n = ROW_CHUNK // per_row
        return jnp.repeat(ref[pl.ds(pl.multiple_of(chunk * n, n), n), :], per_row, axis=0)
    return ref[pl.ds(pl.program_id(0) // tiles_per_batch, 1), :]


def _mod_mm_kernel(x_ref, sh_ref, sc_ref, nw_ref, *rest, glu, per_row, tiles_per_batch):
    if glu:
        wg_ref, wu_ref, o_ref, h_scr = rest
    else:
        w_ref, o_ref, h_scr = rest

    @pl.when(pl.program_id(1) == 0)
    def _():
        nw = nw_ref[...]

        def chunk(r, carry):
            start = pl.multiple_of(r * ROW_CHUNK, ROW_CHUNK)
            x = x_ref[pl.ds(start, ROW_CHUNK), :]
            y = x * lax.rsqrt(jnp.mean(x * x, -1, keepdims=True) + EPS) * nw
            sc = _mod_rows(sc_ref, per_row, tiles_per_batch, r)
            sh = _mod_rows(sh_ref, per_row, tiles_per_batch, r)
            h_scr[pl.ds(start, ROW_CHUNK), :] = (y * (1.0 + sc) + sh).astype(BF16)
            return carry

        lax.fori_loop(0, x_ref.shape[0] // ROW_CHUNK, chunk, 0)

    h = h_scr[...]
    if glu:
        g = jnp.dot(h, wg_ref[...], preferred_element_type=F32)
        u = jnp.dot(h, wu_ref[...], preferred_element_type=F32)
        o_ref[...] = (_silu(g) * u).astype(o_ref.dtype)
    else:
        o_ref[...] = jnp.dot(h, w_ref[...], preferred_element_type=F32).astype(o_ref.dtype)


def _mod_mm(x, mod, sub, nw, w, widx, *, glu, per_row, rows_per_batch, out_dtype, tm, tn, ncols=None):
    t, d = x.shape
    n = w.shape[-1] // 2 if glu else (ncols or w.shape[-1])
    nj = n // tn
    lead = (None,) * len(widx)
    tiles_per_batch = max(rows_per_batch // tm, 1)
    per_row = rows_per_batch if per_row else 0
    mrows = tm // rows_per_batch if per_row else mod.shape[0]

    def mod_spec(c):
        col = sub * 3 + c
        if per_row:
            return pl.BlockSpec((mrows, d), lambda i, j: (i, col))
        return pl.BlockSpec((mrows, d), lambda i, j: (0, col))

    in_specs = [pl.BlockSpec((tm, d), lambda i, j: (i, 0)),
                mod_spec(0), mod_spec(1),
                pl.BlockSpec((1, d), lambda i, j: (0, 0))]
    args = [x, mod, mod, nw.reshape(1, d)]
    if glu:
        in_specs += [pl.BlockSpec(lead + (d, tn), lambda i, j: widx + (0, j)),
                     pl.BlockSpec(lead + (d, tn), lambda i, j: widx + (0, j + nj))]
        args += [w, w]
    else:
        in_specs += [pl.BlockSpec(lead + (d, tn), lambda i, j: widx + (0, j))]
        args += [w]
    return pl.pallas_call(
        functools.partial(_mod_mm_kernel, glu=glu, per_row=per_row, tiles_per_batch=tiles_per_batch),
        grid=(t // tm, nj),
        in_specs=in_specs,
        out_specs=pl.BlockSpec((tm, tn), lambda i, j: (i, j)),
        out_shape=jax.ShapeDtypeStruct((t, n), out_dtype),
        scratch_shapes=[pltpu.VMEM((tm, d), BF16)],
        compiler_params=_params(("parallel", "arbitrary")),
        name="mod_mm_glu" if glu else "mod_mm",
    )(*args)


def _mm_out_kernel(a_ref, w_ref, x_ref, g_ref, nw_ref, o_ref, *, res_scale, per_row, tiles_per_batch):
    y = jnp.dot(a_ref[...].astype(BF16), w_ref[...], preferred_element_type=F32)
    yn = y * lax.rsqrt(jnp.mean(y * y, -1, keepdims=True) + EPS) * nw_ref[...]
    gate = _mod_rows(g_ref, per_row, tiles_per_batch)
    o_ref[...] = x_ref[...] + res_scale * (gate * yn)


def _mm_out(a, w, widx, x, mod, sub, nw, *, res_scale, per_row, rows_per_batch, tm):
    t, kdim = a.shape
    d = w.shape[-1]
    lead = (None,) * len(widx)
    tiles_per_batch = max(rows_per_batch // tm, 1)
    per_row = rows_per_batch if per_row else 0
    col = sub * 3 + 2
    if per_row:
        g_spec = pl.BlockSpec((tm // rows_per_batch, d), lambda i: (i, col))
    else:
        g_spec = pl.BlockSpec((mod.shape[0], d), lambda i: (0, col))
    return pl.pallas_call(
        functools.partial(_mm_out_kernel, res_scale=res_scale, per_row=per_row,
                          tiles_per_batch=tiles_per_batch),
        grid=(t // tm,),
        in_specs=[pl.BlockSpec((tm, kdim), lambda i: (i, 0)),
                  pl.BlockSpec(lead + (kdim, d), lambda i: widx + (0, 0), pipeline_mode=pl.Buffered(1)),
                  pl.BlockSpec((tm, d), lambda i: (i, 0)),
                  g_spec,
                  pl.BlockSpec((1, d), lambda i: (0, 0))],
        out_specs=pl.BlockSpec((tm, d), lambda i: (i, 0)),
        out_shape=jax.ShapeDtypeStruct((t, d), F32),
        compiler_params=_params(("parallel",)),
        name="mm_out",
    )(a, w, x, mod, nw.reshape(1, d))


def _rotate(x, cos, sin):
    half = x.shape[-1] // 2
    x1, x2 = x[:, :half], x[:, half:]
    return jnp.concatenate([x1 * cos - x2 * sin, x1 * sin + x2 * cos], axis=-1)


def _ret_kernel(dm_ref, qd_ref, kd_ref, cd_ref, cos_ref, sin_ref, q_ref, k_ref, v_ref, g_ref, *rest,
                heads, has_state, chunks, seqs):
    if has_state:
        s0_ref, o_ref, sout_ref = rest
    else:
        o_ref, sout_ref = rest
    dk, dv = RET_DK, RET_DV
    c = q_ref.shape[0] // seqs
    units = [(e, h) for e in range(seqs) for h in range(heads)]
    rows = lambda e: slice(e * c, (e + 1) * c)

    if has_state and chunks == 1:
        s = {u: s0_ref[u] for u in units}
    else:
        @pl.when(pl.program_id(2) == 0)
        def _():
            sout_ref[...] = s0_ref[...] if has_state else jnp.zeros_like(sout_ref)

        s = {u: sout_ref[u] for u in units}

    cos, sin = cos_ref[...], sin_ref[...]
    q = {(e, h): _rotate(q_ref[rows(e), h * dk:(h + 1) * dk].astype(F32), cos, sin) * (dk ** -0.5)
         for e, h in units}
    k = {(e, h): _rotate(k_ref[rows(e), h * dk:(h + 1) * dk].astype(F32), cos, sin) for e, h in units}
    v = {(e, h): v_ref[rows(e), h * dv:(h + 1) * dv].astype(BF16) for e, h in units}
    scores = {(e, h): _dot_nt(q[e, h], k[e, h]) * dm_ref[h] for e, h in units}
    cross = {u: _dot(q[u], s[u]) for u in units}
    inner = {u: _dot(scores[u], v[u]) for u in units}
    upd = {(e, h): _dot_tn(k[e, h] * jnp.concatenate([kd_ref[h]] * (dk // LANES), axis=1), v[e, h])
           for e, h in units}
    for e, h in units:
        sout_ref[e, h] = s[e, h] * cd_ref[h, 0:1, 0:1] + upd[e, h]
        o = inner[e, h] + cross[e, h] * jnp.concatenate([qd_ref[h]] * (dv // LANES), axis=1)
        on = o * lax.rsqrt(jnp.mean(o * o, -1, keepdims=True) + EPS)
        o_ref[rows(e), h * dv:(h + 1) * dv] = (
            _silu(g_ref[rows(e), h * dv:(h + 1) * dv].astype(F32)) * on).astype(o_ref.dtype)


def _retention(proj, s0, pos0, batch, seq, c, heads, seqs=1):
    nh, dk, dv = RET_HEADS, RET_DK, RET_DV
    nc = seq // c
    assert seqs == 1 or (nc == 1 and batch % seqs == 0)
    ng = nh // heads
    rb = seqs * c
    half = dk // 2
    log_g = jnp.log1p(-jnp.exp2(-5.0 - jnp.arange(nh, dtype=F32)))
    idx = jnp.arange(c, dtype=F32)
    diff = idx[:, None] - idx[None, :]
    causal = diff >= 0
    dmask = jnp.where(causal[None], jnp.exp(log_g[:, None, None] * jnp.where(causal, diff, 0.0)[None]), 0.0)
    lanes = lambda x: jnp.broadcast_to(x[:, :, None], (nh, x.shape[1], LANES))
    q_decay = lanes(jnp.exp(log_g[:, None] * (idx[None, :] + 1.0)))
    k_decay = lanes(jnp.exp(log_g[:, None] * (c - 1.0 - idx)[None, :]))
    chunk_decay = jnp.broadcast_to(jnp.exp(log_g * c)[:, None, None], (nh, SUBLANES, LANES))
    inv = 1.0 / (ROPE_BASE ** jnp.linspace(0.0, 1.0, half, dtype=F32))
    ang = (pos0 + jnp.arange(seq)).astype(F32)[:, None] * inv[None, :]
    cos, sin = jnp.cos(ang), jnp.sin(ang)
    has_state = s0 is not None
    tokb = lambda b, g, n: b * nc + n
    in_specs = [pl.BlockSpec((heads, c, c), lambda b, g, n: (g, 0, 0)),
                pl.BlockSpec((heads, c, LANES), lambda b, g, n: (g, 0, 0)),
                pl.BlockSpec((heads, c, LANES), lambda b, g, n: (g, 0, 0)),
                pl.BlockSpec((heads, SUBLANES, LANES), lambda b, g, n: (g, 0, 0)),
                pl.BlockSpec((c, half), lambda b, g, n: (n, 0)),
                pl.BlockSpec((c, half), lambda b, g, n: (n, 0)),
                pl.BlockSpec((rb, heads * dk), lambda b, g, n: (tokb(b, g, n), g)),
                pl.BlockSpec((rb, heads * dk), lambda b, g, n: (tokb(b, g, n), ng + g)),
                pl.BlockSpec((rb, heads * dv), lambda b, g, n: (tokb(b, g, n), ng + g)),
                pl.BlockSpec((rb, heads * dv), lambda b, g, n: (tokb(b, g, n), 2 * ng + g))]
    args = [dmask, q_decay, k_decay, chunk_decay, cos, sin, proj, proj, proj, proj]
    if has_state:
        in_specs.append(pl.BlockSpec((seqs, heads, dk, dv), lambda b, g, n: (b, g, 0, 0)))
        args.append(s0)
    return pl.pallas_call(
        functools.partial(_ret_kernel, heads=heads, has_state=has_state, chunks=nc, seqs=seqs),
        grid=(batch // seqs, ng, nc),
        in_specs=in_specs,
        out_specs=[pl.BlockSpec((rb, heads * dv), lambda b, g, n: (tokb(b, g, n), g)),
                   pl.BlockSpec((seqs, heads, dk, dv), lambda b, g, n: (b, g, 0, 0))],
        out_shape=[jax.ShapeDtypeStruct((batch * seq, nh * dv), BF16 if rb % (2 * SUBLANES) == 0 else F32),
                   jax.ShapeDtypeStruct((batch, nh, dk, dv), F32)],
        compiler_params=_params(("parallel", "parallel", "arbitrary")),
        name="retention",
    )(*args)


AB_GROUP = 4
AB_LANES = 4


def _gdn_ab_weight(w_in):
    d = w_in.shape[0]
    ab = w_in[:, GDN_CONV_CH + GDN_V:]
    a = ab[:, :GDN_V_HEADS].reshape(d, GDN_K_HEADS, 2)
    b = ab[:, GDN_V_HEADS:].reshape(d, GDN_K_HEADS, 2)
    groups = GDN_K_HEADS // AB_GROUP
    packed = jnp.concatenate([a, b], axis=-1).reshape(d, groups, AB_GROUP * AB_LANES)
    tail = jnp.concatenate([packed, jnp.zeros((d, groups, LANES - AB_GROUP * AB_LANES), w_in.dtype)], axis=-1)
    return tail.reshape(d, groups * LANES).astype(BF16)


def _ab_to_lane0(x, j):
    return pltpu.roll(x, (LANES - AB_LANES * (j % AB_GROUP)) % LANES, axis=1)


def _gdn_head_tables(a_log, dt_bias):
    def tab(x):
        row = jnp.concatenate([x.reshape(GDN_K_HEADS, 2), jnp.zeros((GDN_K_HEADS, LANES - 2), F32)], -1)
        return jnp.broadcast_to(row[:, None, :], (GDN_K_HEADS, SUBLANES, LANES))
    return tab(a_log), tab(dt_bias)


GDN_C = 64
GDN_SUB = 256


def _chunk_cumsum(x, c):
    pos = lax.broadcasted_iota(jnp.int32, x.shape, 0) % c
    s = 1
    while s < c:
        x = x + jnp.where(pos >= s, pltpu.roll(x, s, axis=0), 0.0)
        s *= 2
    return x


GDN_BASE = 8


def _inverse_masks(c, width):
    i = jnp.arange(c)[:, None]
    j = (jnp.arange(width) % c)[None, :]
    r = (jnp.arange(width) % (2 * c))[:, None]
    s = jnp.arange(2 * c)[None, :]
    lanes = [(i // GDN_BASE) == (j // GDN_BASE)]
    pair = [(r // c) == (s // c), (r // GDN_BASE) == (s // GDN_BASE)]
    b = GDN_BASE
    while b < c:
        lanes.append(((i // (2 * b)) == (j // (2 * b))) & ((i // b) % 2 == 1) & ((j // b) % 2 == 0))
        pair.append(((r // (2 * b)) == (s // (2 * b))) & ((r // b) % 2 == 1) & ((s // b) % 2 == 0))
        b *= 2
    causal = jnp.stack([pair[0] & (r >= s), pair[0] & (r > s)]).astype(F32)
    return jnp.stack(lanes).astype(F32), jnp.stack(pair).astype(BF16), causal


def _mask_dict(lanes_ref, pair_ref):
    masks = {"base_lanes": lanes_ref[0], "same_pair": pair_ref[0], "base_pair": pair_ref[1]}
    for n in range(1, lanes_ref.shape[0]):
        masks[f"lanes{GDN_BASE << (n - 1)}"] = lanes_ref[n]
        masks[f"pair{GDN_BASE << (n - 1)}"] = pair_ref[n + 1]
    return masks


def _pair_to_diag(p):
    width, pc = p.shape
    n = width // pc
    zero = jnp.zeros((pc, pc), p.dtype)
    return jnp.concatenate(
        [jnp.concatenate([p[i * pc:(i + 1) * pc] if j == i else zero for j in range(n)], axis=1)
         for i in range(n)], axis=0)


def _lanes_to_diag(p, masks):
    c, width = p.shape
    pb = p.astype(BF16)
    pair = jnp.concatenate([jnp.concatenate([pb[:, i * 2 * c:(i + 1) * 2 * c]] * 2, axis=0)
                            for i in range(width // (2 * c))], axis=0)
    return _pair_to_diag(pair * masks["same_pair"])


def _inverse_minus_eye_lanes(l_lbs, l_pairs, masks):
    c = l_lbs[0].shape[0]
    ps = range(len(l_lbs))
    mm = lambda a, b: jnp.dot(a.astype(BF16), b, preferred_element_type=F32)

    lb = [l_lbs[i] * masks["base_lanes"] for i in ps]
    x = [-lb[i] for i in ps]
    p = [mm(lb[i], _pair_to_diag(l_pairs[i] * masks["base_pair"])) for i in ps]
    n = 2
    while 2 * n < GDN_BASE:
        xp = [mm(jnp.concatenate([x[i], p[i]], axis=0), _lanes_to_diag(p[i], masks)) for i in ps]
        x = [x[i] + p[i] + xp[i][:c] for i in ps]
        p = [xp[i][c:] for i in ps]
        n *= 2
    xp = [mm(x[i], _lanes_to_diag(p[i], masks)) for i in ps]
    x = [x[i] + p[i] + xp[i] for i in ps]

    b = GDN_BASE
    while b < c:
        t = [l_lbs[i] * masks[f"lanes{b}"]
             + mm(x[i], _pair_to_diag(l_pairs[i] * masks[f"pair{b}"])) for i in ps]
        tx = [mm(t[i], _lanes_to_diag(x[i], masks)) for i in ps]
        x = [x[i] - t[i] - tx[i] for i in ps]
        b *= 2
    return x


def _gdn_prep_kernel(q_ref, k_ref, v_ref, ab_ref, qp_ref, kp_ref, vp_ref, wq_ref, wk_ref, wv_ref,
                     alog_ref, dt_ref, ml_ref, mp_ref, mc_ref, u_ref, w_ref, qg_ref, kgt_ref, at_ref, eg_ref,
                     bq, bk, bv, gc_scr, *, tb):
    c, sub = GDN_C, GDN_SUB
    first = pl.program_id(1) == 0

    def conv_silu(buf, prev_ref, x_ref, cw_ref):
        buf[pl.ds(0, SUBLANES), :] = jnp.where(first, 0.0, prev_ref[...])
        buf[pl.ds(SUBLANES, tb), :] = x_ref[...]
        cw = cw_ref[...]
        out = buf[pl.ds(SUBLANES - CONV_HIST, tb), :] * cw[0:1, :]
        for j in range(1, CONV_W):
            out = out + buf[pl.ds(SUBLANES - CONV_HIST + j, tb), :] * cw[j:j + 1, :]
        return _silu(out)

    q = conv_silu(bq, qp_ref, q_ref, wq_ref)
    k = conv_silu(bk, kp_ref, k_ref, wk_ref)
    v2 = conv_silu(bv, vp_ref, v_ref, wv_ref)
    q = q * lax.rsqrt(jnp.sum(q * q, -1, keepdims=True) + EPS) * (GDN_DK ** -0.5)
    k = k * lax.rsqrt(jnp.sum(k * k, -1, keepdims=True) + EPS)

    ab = _ab_to_lane0(ab_ref[...], pl.program_id(2))
    g_all = -jnp.exp(alog_ref[0:1, :]) * jax.nn.softplus(ab + dt_ref[0:1, :])
    beta_all = jax.nn.sigmoid(ab)
    gc_all = _chunk_cumsum(g_all, c)
    gc_t = gc_all.T

    masks = _mask_dict(ml_ref, mp_ref)
    tri = mc_ref[0] > 0.0
    strict = mc_ref[1] > 0.0

    nh, nchunk, pc = tb // sub, sub // c, 2 * c
    rows = [slice(hf * sub, (hf + 1) * sub) for hf in range(nh)]
    pairs = [slice(i * pc, (i + 1) * pc) for i in range(tb // pc)]
    kb, qb = k.astype(BF16), q.astype(BF16)
    kq = [_dot_nt(kb[pr], kb[pr]) for pr in pairs] + [_dot_nt(qb[pr], kb[pr]) for pr in pairs]
    npair = sub // pc
    kk = [jnp.concatenate([kq[hf * npair + i] for i in range(npair)], axis=0) for hf in range(nh)]
    qk = [jnp.concatenate([kq[len(pairs) + hf * npair + i] for i in range(npair)], axis=0) for hf in range(nh)]
    probs = [(hf, e) for hf in range(nh) for e in range(2)]
    gcb, betab, gl, l_pair, l_lb = [], [], [], [], []
    for p, (hf, e) in enumerate(probs):
        lanes = slice(e * GDN_DV, (e + 1) * GDN_DV)
        gcb.append(jnp.broadcast_to(gc_all[rows[hf], e:e + 1], (sub, LANES)))
        betab.append(jnp.broadcast_to(beta_all[rows[hf], 2 + e:3 + e], (sub, LANES)))
        gc_scr[p] = gcb[p]
        gl.append(gc_scr[p, pl.ds(c - 1, nchunk, stride=c), :])
        gc_cols = jnp.concatenate(
            [jnp.broadcast_to(gc_t[e:e + 1, hf * sub + i * pc:hf * sub + (i + 1) * pc], (pc, pc))
             for i in range(npair)], axis=0)
        decay = jnp.where(tri, jnp.exp(jnp.where(tri, gcb[p] - gc_cols, 0.0)), 0.0)
        lp = jnp.where(strict, kk[hf] * betab[p] * decay, 0.0)
        at_ref[rows[hf], lanes] = (qk[hf] * decay).astype(BF16)
        l_pair.append(lp.astype(BF16))
        l_lb.append(jnp.concatenate([lp[2 * i * c:(2 * i + 1) * c] + lp[(2 * i + 1) * c:(2 * i + 2) * c]
                                     for i in range(npair)], axis=1))
    x_lb = _inverse_minus_eye_lanes(l_lb, l_pair, masks)
    rhs = [jnp.concatenate([v2[rows[hf], e * GDN_DV:(e + 1) * GDN_DV] * betab[p],
                            k[rows[hf]] * betab[p] * jnp.exp(gcb[p])], axis=1)
           for p, (hf, e) in enumerate(probs)]
    corr = [jnp.dot(_lanes_to_diag(x_lb[p], masks), rhs[p].astype(BF16), preferred_element_type=F32)
            for p in range(len(probs))]
    for p, (hf, e) in enumerate(probs):
        lanes = slice(e * GDN_DV, (e + 1) * GDN_DV)
        sol = rhs[p] + corr[p]
        u_ref[rows[hf], lanes] = sol[:, :GDN_DV]
        w_ref[rows[hf], lanes] = sol[:, GDN_DV:].astype(BF16)
        qg_ref[rows[hf], lanes] = (q[rows[hf]] * jnp.exp(gcb[p])).astype(BF16)
        gl_b = jnp.concatenate([jnp.broadcast_to(gl[p][i:i + 1, :], (c, LANES)) for i in range(nchunk)], axis=0)
        kg = k[rows[hf]] * jnp.exp(gl_b - gcb[p])
        kgt_ref[rows[hf], lanes] = jnp.concatenate(
            [kg[i * pc:(i + 1) * pc].T for i in range(npair)], axis=0).astype(BF16)
        eg_ref[hf * nchunk:(hf + 1) * nchunk, lanes] = jnp.exp(gl[p])


def _gdn_scan_kernel(u_ref, w_ref, qg_ref, kgt_ref, at_ref, eg_ref, z_ref, nw_ref, o_ref, sout_ref,
                     *, tb, heads):
    c = GDN_C

    @pl.when(pl.program_id(2) == 0)
    def _():
        sout_ref[...] = jnp.zeros_like(sout_ref)

    nw = nw_ref[...]
    hl = [slice(h * GDN_DV, (h + 1) * GDN_DV) for h in range(heads)]
    s = [sout_ref[h] for h in range(heads)]
    zeros = jnp.zeros((c, GDN_DV), BF16)
    for i in range(tb // c):
        rows = slice(i * c, (i + 1) * c)
        pair_rows = slice((i // 2) * 2 * c, (i // 2 + 1) * 2 * c)
        ws = [jnp.dot(jnp.concatenate([w_ref[rows, hl[h]], qg_ref[rows, hl[h]]], axis=0),
                      s[h].astype(BF16), preferred_element_type=F32) for h in range(heads)]
        v_new = [(u_ref[rows, hl[h]] - ws[h][:c]).astype(BF16) for h in range(heads)]
        pair = [jnp.concatenate([v_new[h], zeros] if i % 2 == 0 else [zeros, v_new[h]], axis=0)
                for h in range(heads)]
        for h in range(heads):
            o = ws[h][c:] + jnp.dot(at_ref[rows, hl[h]], pair[h], preferred_element_type=F32)
            s[h] = s[h] * eg_ref[i:i + 1, hl[h]] + jnp.dot(kgt_ref[pair_rows, hl[h]], pair[h],
                                                          preferred_element_type=F32)
            on = o * lax.rsqrt(jnp.mean(o * o, -1, keepdims=True) + EPS) * nw
            o_ref[rows, hl[h]] = (on * _silu(z_ref[rows, hl[h]])).astype(o_ref.dtype)
    for h in range(heads):
        sout_ref[h] = s[h]


def _gdn_prompt(proj, proj_ab, conv_w, a_log, dt_bias, norm_w, batch, seq, tb_prep=1024, tb_scan=512,
                heads=8):
    hk, dk, dv = GDN_K_HEADS, GDN_DK, GDN_DV
    assert seq % tb_prep == 0 and seq % tb_scan == 0
    t = batch * seq
    tb = tb_prep
    nt = seq // tb
    kq0 = GDN_QK // dk
    v0 = 2 * GDN_QK // (2 * dv)
    alog_tab, dt_tab = _gdn_head_tables(a_log, dt_bias)
    m_lanes, m_pair, m_causal = _inverse_masks(GDN_C, GDN_SUB)
    tok = lambda b, n, j: b * nt + n
    prev = lambda b, n, j: jnp.maximum((b * seq + n * tb) // SUBLANES - 1, 0)
    u, w, qg, kg, at, eg = pl.pallas_call(
        functools.partial(_gdn_prep_kernel, tb=tb),
        grid=(batch, nt, hk),
        in_specs=[pl.BlockSpec((tb, dk), lambda b, n, j: (tok(b, n, j), j)),
                  pl.BlockSpec((tb, dk), lambda b, n, j: (tok(b, n, j), kq0 + j)),
                  pl.BlockSpec((tb, 2 * dv), lambda b, n, j: (tok(b, n, j), v0 + j)),
                  pl.BlockSpec((tb, LANES), lambda b, n, j: (tok(b, n, j), j // AB_GROUP)),
                  pl.BlockSpec((SUBLANES, dk), lambda b, n, j: (prev(b, n, j), j)),
                  pl.BlockSpec((SUBLANES, dk), lambda b, n, j: (prev(b, n, j), kq0 + j)),
                  pl.BlockSpec((SUBLANES, 2 * dv), lambda b, n, j: (prev(b, n, j), v0 + j)),
                  pl.BlockSpec((CONV_W, dk), lambda b, n, j: (0, j)),
                  pl.BlockSpec((CONV_W, dk), lambda b, n, j: (0, kq0 + j)),
                  pl.BlockSpec((CONV_W, 2 * dv), lambda b, n, j: (0, v0 + j)),
                  pl.BlockSpec((None, SUBLANES, LANES), lambda b, n, j: (j, 0, 0)),
                  pl.BlockSpec((None, SUBLANES, LANES), lambda b, n, j: (j, 0, 0)),
                  pl.BlockSpec(m_lanes.shape, lambda b, n, j: (0, 0, 0)),
                  pl.BlockSpec(m_pair.shape, lambda b, n, j: (0, 0, 0)),
                  pl.BlockSpec(m_causal.shape, lambda b, n, j: (0, 0, 0))],
        out_specs=[pl.BlockSpec((tb, 2 * dv), lambda b, n, j: (tok(b, n, j), j))] * 5
        + [pl.BlockSpec((tb // GDN_C, 2 * dv), lambda b, n, j: (tok(b, n, j), j))],
        out_shape=[jax.ShapeDtypeStruct((t, GDN_V), F32)]
        + [jax.ShapeDtypeStruct((t, GDN_V), BF16)] * 4
        + [jax.ShapeDtypeStruct((t // GDN_C, GDN_V), F32)],
        scratch_shapes=[pltpu.VMEM((SUBLANES + tb, dk), F32),
                        pltpu.VMEM((SUBLANES + tb, dk), F32),
                        pltpu.VMEM((SUBLANES + tb, 2 * dv), F32),
                        pltpu.VMEM((2 * tb // GDN_SUB, GDN_SUB, LANES), F32)],
        compiler_params=_params(("parallel", "parallel", "parallel")),
        name="gdn_prep",
    )(proj, proj, proj, proj_ab, proj, proj, proj, conv_w, conv_w, conv_w, alog_tab, dt_tab,
      m_lanes, m_pair, m_causal)

    tb = tb_scan
    nt = seq // tb
    wide = heads * dv
    z0 = GDN_CONV_CH // wide
    blk = lambda b, g, n: (b * nt + n, g)
    return pl.pallas_call(
        functools.partial(_gdn_scan_kernel, tb=tb, heads=heads),
        grid=(batch, GDN_V_HEADS // heads, nt),
        in_specs=[pl.BlockSpec((tb, wide), blk)] * 5
        + [pl.BlockSpec((tb // GDN_C, wide), blk),
           pl.BlockSpec((tb, wide), lambda b, g, n: (b * nt + n, z0 + g)),
           pl.BlockSpec((1, dv), lambda b, g, n: (0, 0))],
        out_specs=[pl.BlockSpec((tb, wide), blk),
                   pl.BlockSpec((None, heads, dk, dv), lambda b, g, n: (b, g, 0, 0))],
        out_shape=[jax.ShapeDtypeStruct((t, GDN_V), BF16),
                   jax.ShapeDtypeStruct((batch, GDN_V_HEADS, dk, dv), F32)],
        compiler_params=_params(("parallel", "parallel", "arbitrary")),
        name="gdn_scan",
    )(u, w, qg, kg, at, eg, proj, norm_w.reshape(1, dv))


def _gdn_step_kernel(q_ref, k_ref, v_ref, z_ref, ab_ref, cq_ref, ck_ref, cv_ref, wq_ref, wk_ref, wv_ref,
                     alog_ref, dt_ref, nw_ref, s0_ref, o_ref, sout_ref, cout_ref, bq, bk, bv, g_scr, *, seq, nb):
    hv, hk, dk, dv = GDN_V_HEADS, GDN_K_HEADS, GDN_DK, GDN_DV
    n = hv * seq
    rep = hv // hk
    es = range(nb)

    def conv_silu(e, buf, c_ref, x_ref, cw_ref, col):
        width = buf.shape[-1]
        buf[e, pl.ds(SUBLANES - CONV_HIST, CONV_HIST), :] = c_ref[e]
        buf[e, pl.ds(SUBLANES, seq), :] = x_ref[e * seq:(e + 1) * seq, :]
        cout_ref[e, :, col:col + width] = buf[e, pl.ds(SUBLANES + seq - CONV_HIST, CONV_HIST), :]
        cw = cw_ref[...]
        out = buf[e, pl.ds(SUBLANES - CONV_HIST, seq), :] * cw[0:1, :]
        for j in range(1, CONV_W):
            out = out + buf[e, pl.ds(SUBLANES - CONV_HIST + j, seq), :] * cw[j:j + 1, :]
        return _silu(out)

    def stack(x, width):
        return jnp.concatenate([x[:, h * width:(h + 1) * width] for h in range(x.shape[1] // width)], axis=0)

    def per_v_head(x):
        return jnp.concatenate([x[(h // rep) * seq:(h // rep + 1) * seq] for h in range(hv)], axis=0)

    def unit(x):
        return x * lax.rsqrt(jnp.sum(x * x, -1, keepdims=True) + EPS)

    def per_head_rows(x_st, lane0):
        cols = [jnp.broadcast_to(x_st[:, lane0 + e:lane0 + e + 1], (hk * seq, LANES)) for e in range(rep)]
        return jnp.concatenate([cols[h % rep][(h // rep) * seq:(h // rep + 1) * seq] for h in range(hv)], axis=0)

    q = [per_v_head(unit(stack(conv_silu(e, bq, cq_ref, q_ref, wq_ref, 0), dk))) * (dk ** -0.5) for e in es]
    k = [per_v_head(unit(stack(conv_silu(e, bk, ck_ref, k_ref, wk_ref, GDN_QK), dk))) for e in es]
    v = [stack(conv_silu(e, bv, cv_ref, v_ref, wv_ref, 2 * GDN_QK), dv) for e in es]

    ab = [jnp.concatenate(
        [_ab_to_lane0(ab_ref[e * seq:(e + 1) * seq, (j // AB_GROUP) * LANES:(j // AB_GROUP + 1) * LANES], j)
         for j in range(hk)], axis=0) for e in es]
    gc_st = [_chunk_cumsum(-jnp.exp(alog_ref[...]) * jax.nn.softplus(ab[e] + dt_ref[...]), seq) for e in es]
    gcb = [per_head_rows(gc_st[e], 0) for e in es]
    betab = [per_head_rows(jax.nn.sigmoid(ab[e]), rep) for e in es]
    gc_row = [gcb[e].T[0:1, :] for e in es]
    gl = []
    for e in es:
        g_scr[e] = gcb[e]
        gl.append(g_scr[e, pl.ds(seq - 1, hv, stride=seq), :])
    eg = [jnp.exp(gl[e]) for e in es]

    r = lax.broadcasted_iota(jnp.int32, (n, n), 0)
    s = lax.broadcasted_iota(jnp.int32, (n, n), 1)
    same = (r // seq) == (s // seq)
    tri = same & (r >= s)
    strict = same & (r > s)
    wide = lambda x: jnp.concatenate([x] * (n // LANES), axis=1)
    decay = [jnp.where(tri, jnp.exp(jnp.where(tri, wide(gcb[e]) - gc_row[e], 0.0)), 0.0) for e in es]
    kq = [_dot_nt(jnp.concatenate([k[e], q[e]], axis=0), k[e]) for e in es]
    l_bd = [jnp.where(strict, kq[e][:n] * wide(betab[e]) * decay[e], 0.0) for e in es]
    a_bd = [kq[e][n:] * decay[e] for e in es]

    x = [-l_bd[e] for e in es]
    p = [_dot(l_bd[e], l_bd[e]) for e in es]
    m = 2
    while 2 * m < seq:
        xp = [_dot(jnp.concatenate([x[e], p[e]], axis=0), p[e]) for e in es]
        x = [x[e] + p[e] + xp[e][:n] for e in es]
        p = [xp[e][n:] for e in es]
        m *= 2
    xp = [_dot(x[e], p[e]) for e in es]
    x = [x[e] + p[e] + xp[e] for e in es]

    rhs = [jnp.concatenate([v[e] * betab[e], k[e] * betab[e] * jnp.exp(gcb[e])], axis=1) for e in es]
    corr = [_dot(x[e], rhs[e]) for e in es]
    sol = [rhs[e] + corr[e] for e in es]
    qg = [q[e] * jnp.exp(gcb[e]) for e in es]
    kg = []
    for e in es:
        gl_b = jnp.concatenate([jnp.broadcast_to(gl[e][h:h + 1, :], (seq, LANES)) for h in range(hv)], axis=0)
        kg.append(k[e] * jnp.exp(gl_b - gcb[e]))

    eh = [(e, h) for e in es for h in range(hv)]
    hrows = [slice(h * seq, (h + 1) * seq) for h in range(hv)]
    ws = {(e, h): _dot(jnp.concatenate([sol[e][hrows[h], dv:], qg[e][hrows[h]]], axis=0), s0_ref[e, h])
          for e, h in eh}
    v_new = {(e, h): sol[e][hrows[h], :dv] - ws[e, h][:seq] for e, h in eh}
    upd = {(e, h): _dot_tn(kg[e][hrows[h]], v_new[e, h]) for e, h in eh}
    for e, h in eh:
        sout_ref[e, h] = s0_ref[e, h] * eg[e][h:h + 1, :] + upd[e, h]
    inner = [_dot(a_bd[e], jnp.concatenate([v_new[e, h] for h in range(hv)], axis=0)) for e in es]
    for e in es:
        o = jnp.concatenate([ws[e, h][seq:] for h in range(hv)], axis=0) + inner[e]
        z = stack(z_ref[e * seq:(e + 1) * seq, :], dv)
        on = o * lax.rsqrt(jnp.mean(o * o, -1, keepdims=True) + EPS) * nw_ref[...] * _silu(z)
        for h in range(hv):
            o_ref[e * seq:(e + 1) * seq, h * dv:(h + 1) * dv] = on[h * seq:(h + 1) * seq]


def _gdn_step(proj, proj_ab, conv_w, a_log, dt_bias, norm_w, s0, buf0, batch, seq, nb=4):
    hv, hk, dk, dv = GDN_V_HEADS, GDN_K_HEADS, GDN_DK, GDN_DV
    assert seq == SUBLANES and batch % nb == 0
    qw, vw, abw = GDN_QK, GDN_V, hk // AB_GROUP * LANES
    alog_tab, dt_tab = _gdn_head_tables(a_log, dt_bias)
    alog_tab = alog_tab.reshape(hk * seq, LANES)
    dt_tab = dt_tab.reshape(hk * seq, LANES)
    return pl.pallas_call(
        functools.partial(_gdn_step_kernel, seq=seq, nb=nb),
        grid=(batch // nb,),
        in_specs=[pl.BlockSpec((nb * seq, qw), lambda b: (b, 0)),
                  pl.BlockSpec((nb * seq, qw), lambda b: (b, 1)),
                  pl.BlockSpec((nb * seq, vw), lambda b: (b, 2 * qw // vw)),
                  pl.BlockSpec((nb * seq, vw), lambda b: (b, GDN_CONV_CH // vw)),
                  pl.BlockSpec((nb * seq, abw), lambda b: (b, 0)),
                  pl.BlockSpec((nb, CONV_W - 1, qw), lambda b: (b, 0, 0)),
                  pl.BlockSpec((nb, CONV_W - 1, qw), lambda b: (b, 0, 1)),
                  pl.BlockSpec((nb, CONV_W - 1, vw), lambda b: (b, 0, 2 * qw // vw)),
                  pl.BlockSpec((CONV_W, qw), lambda b: (0, 0)),
                  pl.BlockSpec((CONV_W, qw), lambda b: (0, 1)),
                  pl.BlockSpec((CONV_W, vw), lambda b: (0, 2 * qw // vw)),
                  pl.BlockSpec((hk * seq, LANES), lambda b: (0, 0)),
                  pl.BlockSpec((hk * seq, LANES), lambda b: (0, 0)),
                  pl.BlockSpec((1, dv), lambda b: (0, 0)),
                  pl.BlockSpec((nb, hv, dk, dv), lambda b: (b, 0, 0, 0))],
        out_specs=[pl.BlockSpec((nb * seq, vw), lambda b: (b, 0)),
                   pl.BlockSpec((nb, hv, dk, dv), lambda b: (b, 0, 0, 0)),
                   pl.BlockSpec((nb, CONV_W - 1, GDN_CONV_CH), lambda b: (b, 0, 0))],
        out_shape=[jax.ShapeDtypeStruct((batch * seq, GDN_V), F32),
                   jax.ShapeDtypeStruct((batch, hv, dk, dv), F32),
                   jax.ShapeDtypeStruct((batch, CONV_W - 1, GDN_CONV_CH), F32)],
        scratch_shapes=[pltpu.VMEM((nb, 2 * SUBLANES, qw), F32),
                        pltpu.VMEM((nb, 2 * SUBLANES, qw), F32),
                        pltpu.VMEM((nb, 2 * SUBLANES, vw), F32),
                        pltpu.VMEM((nb, hv * seq, LANES), F32)],
        compiler_params=_params(("parallel",)),
        name="gdn_step",
    )(proj, proj, proj, proj, proj_ab, buf0, buf0, buf0, conv_w, conv_w, conv_w, alog_tab, dt_tab,
      norm_w.reshape(1, dv), s0)


def _trunk(x, mod_fn, per_row, batch, seq, pos0, ret_state, gdn_state, conv_state, wts, tm):
    (norm_pre, norm_post, w_gu, w_down, ret_w_in, ret_w_out, gdn_w_in, gdn_w_ab, gdn_conv_w, gdn_a_log,
     gdn_dt_bias, gdn_norm_w, gdn_w_out) = wts
    kw = dict(per_row=per_row, rows_per_batch=seq)
    depth = norm_pre.shape[0]
    new_ret, new_gdn, new_conv = [], [], []
    for i in range(depth):
        mod = mod_fn(i)

        def ffn(x, sub, f):
            a = _mod_mm(x, mod, sub, norm_pre[i, sub], w_gu, (i, f), glu=True, out_dtype=BF16,
                        tm=tm, tn=_col_tile(w_gu.shape[-1] // 2, GLU_COL_TILE), **kw)
            return _mm_out(a, w_down, (i, f), x, mod, sub, norm_post[i, sub], res_scale=FFN_RES,
                           tm=256, **kw)

        x = ffn(x, 0, 0)
        r = i // 2
        if i % 2 == 0:
            proj = _mod_mm(x, mod, 1, norm_pre[i, 1], ret_w_in, (r,), glu=False, out_dtype=BF16,
                           tm=tm, tn=_col_tile(ret_w_in.shape[-1], COL_TILE), **kw)
            y, s = _retention(proj, None if ret_state is None else ret_state[r], pos0, batch, seq,
                              c=math.gcd(256, seq), heads=4 if seq >= 256 else RET_HEADS,
                              seqs=1 if seq >= 256 else 2)
            new_ret.append(s)
            x = _mm_out(y, ret_w_out, (r,), x, mod, 1, norm_post[i, 1], res_scale=1.0,
                        tm=256, **kw)
        else:
            wide = GDN_CONV_CH + GDN_V
            proj = _mod_mm(x, mod, 1, norm_pre[i, 1], gdn_w_in, (r,), glu=False, out_dtype=F32,
                           tm=tm, tn=_col_tile(wide, COL_TILE), ncols=wide, **kw)
            proj_ab = _mod_mm(x, mod, 1, norm_pre[i, 1], gdn_w_ab, (r,), glu=False, out_dtype=F32,
                              tm=tm, tn=_col_tile(gdn_w_ab.shape[-1], COL_TILE), **kw)
            if gdn_state is None:
                y, s = _gdn_prompt(proj, proj_ab, gdn_conv_w[r], gdn_a_log[r], gdn_dt_bias[r],
                                   gdn_norm_w[r], batch, seq)
                cs = proj.reshape(batch, seq, -1)[:, seq - (CONV_W - 1):, :GDN_CONV_CH]
            else:
                y, s, cs = _gdn_step(proj, proj_ab, gdn_conv_w[r], gdn_a_log[r], gdn_dt_bias[r],
                                     gdn_norm_w[r], gdn_state[r], conv_state[r], batch, seq)
            new_gdn.append(s)
            new_conv.append(cs)
            x = _mm_out(y, gdn_w_out, (r,), x, mod, 1, norm_post[i, 1], res_scale=1.0,
                        tm=256, **kw)
        x = ffn(x, 2, 1)
    stack = lambda xs: xs[0][None] if len(xs) == 1 else jnp.stack(xs)
    return x, stack(new_ret), stack(new_gdn), stack(new_conv)


def kernel(x_prompt, x_sample, c_prompt, c_sample, state_ret, state_gdn, state_conv, w_ada, b_ada,
           norm_pre, norm_post, ffn_w_gu, ffn_w_down, ret_w_in, ret_w_out, gdn_w_in, gdn_conv_w,
           gdn_a_log, gdn_dt_bias, gdn_norm_w, gdn_w_out):
    bp, lp, d = x_prompt.shape
    bs, ls, _ = x_sample.shape
    wts = (norm_pre, norm_post, ffn_w_gu.astype(BF16), ffn_w_down.astype(BF16),
           ret_w_in.astype(BF16), ret_w_out.astype(BF16),
           gdn_w_in.astype(BF16), jax.vmap(_gdn_ab_weight)(gdn_w_in),
           gdn_conv_w, gdn_a_log, gdn_dt_bias, gdn_norm_w, gdn_w_out.astype(BF16))

    cs_rows = c_sample
    cp_rows = jnp.pad(c_prompt, ((0, 2 * SUBLANES - bp), (0, 0)))
    mods = [_ada(cs_rows, cp_rows, w_ada, b_ada, i) for i in range(w_ada.shape[0])]

    y_p, ret_p, gdn_p, conv_p = _trunk(
        x_prompt.reshape(bp * lp, d), lambda i: mods[i][1], False, bp, lp, 0,
        None, None, None, wts, tm=1024)
    y_s, ret_s, gdn_s, conv_s = _trunk(
        x_sample.reshape(bs * ls, d), lambda i: mods[i][0], True, bs, ls, PAST_LEN,
        state_ret, state_gdn, state_conv, wts, tm=1024)
    return (y_p.reshape(bp, lp, d), y_s.reshape(bs, ls, d), ret_p, ret_s, gdn_p, gdn_s, conv_p, conv_s)
```
